```python
import math
import jax, jax.numpy as jnp
from jax import lax
import numpy as np

D_MODEL = 2048
BATCH = 8
SEQ = 2048
DEPTH = 1
DEC_BATCH = 1
DEC_SEQ = 8192
PAST_LEN = 128

DIFF_HEADS = 8
DIFF_HEAD_DIM = 64
DIFF_QK_WIDTH = DIFF_HEADS * 2 * DIFF_HEAD_DIM
DIFF_V_WIDTH = DIFF_HEADS * 2 * DIFF_HEAD_DIM
D_DIFF_OUT = DIFF_V_WIDTH
MLA_HEADS = 8
MLA_NOPE_DIM = 128
MLA_ROPE_DIM = 64
MLA_V_DIM = 128
Q_LORA_RANK = 512
KV_LORA_RANK = 256
ROPE_BASE = 10000.0
D_MLA_OUT = MLA_HEADS * MLA_V_DIM
D_MIX = D_DIFF_OUT + D_MLA_OUT
D_IN_PROJ = 2 * DIFF_QK_WIDTH + DIFF_V_WIDTH + Q_LORA_RANK + KV_LORA_RANK + MLA_ROPE_DIM
N_GROUPS = 4
EXPERTS_PER_GROUP = 4
N_EXPERTS = N_GROUPS * EXPERTS_PER_GROUP
TOP_K_IN_GROUP = 2
D_EXPERT = 512
N_MOD = 6
Q_BLOCK = 128
NORM_EPS = 1e-6

kernel_name = "hymba_diff_mla_hiermoe_encoder"


def _rms_norm(x, g):
    xf = x.astype(jnp.float32)
    y = xf * lax.rsqrt(jnp.mean(xf * xf, axis=-1, keepdims=True) + NORM_EPS)
    return (y * g.astype(jnp.float32)).astype(x.dtype)


def _rope(x, pos):
    half = MLA_ROPE_DIM // 2
    inv = ROPE_BASE ** (-jnp.arange(half, dtype=jnp.float32) / half)
    ang = pos.astype(jnp.float32)[:, None] * inv[None, :]
    cos, sin = jnp.cos(ang), jnp.sin(ang)
    xf = x.astype(jnp.float32)
    x1, x2 = xf[..., :half], xf[..., half:]
    return jnp.concatenate([x1 * cos - x2 * sin, x2 * cos + x1 * sin], axis=-1).astype(x.dtype)


def _alibi_slopes(n_heads):
    return jnp.array([2.0 ** (-8.0 * (i + 1) / n_heads) for i in range(n_heads)], dtype=jnp.float32)


def _map_query_blocks(fn, qs):
    b, h, s, _ = qs[0].shape
    nb = s // Q_BLOCK
    blocks = tuple(jnp.moveaxis(q.reshape(b, h, nb, Q_BLOCK, q.shape[-1]), 2, 0) for q in qs)
    out = lax.map(lambda a: fn(a[0], *a[1]), (jnp.arange(nb), blocks))
    return jnp.moveaxis(out, 0, 2).reshape(b, h, s, out.shape[-1])


def _diff_attention(q, k, v, lam, g_sub, lambda_init):
    b, s, _ = q.shape
    q = q.reshape(b, s, DIFF_HEADS, 2, DIFF_HEAD_DIM).transpose(0, 2, 3, 1, 4)
    kf = k.reshape(b, s, DIFF_HEADS, 2, DIFF_HEAD_DIM).transpose(0, 2, 3, 1, 4).astype(jnp.float32)
    k1, k2 = kf[:, :, 0], kf[:, :, 1]
    v = v.reshape(b, s, DIFF_HEADS, 2 * DIFF_HEAD_DIM).transpose(0, 2, 1, 3)
    slopes = _alibi_slopes(DIFF_HEADS)
    kpos = jnp.arange(s)
    scale = DIFF_HEAD_DIM ** -0.5

    def block(i, q1b, q2b):
        qpos = i * Q_BLOCK + jnp.arange(Q_BLOCK)
        bias = -slopes[:, None, None] * jnp.abs(qpos[:, None] - kpos[None, :]).astype(jnp.float32)
        s1 = jnp.einsum('bhqd,bhkd->bhqk', q1b.astype(jnp.float32), k1) * scale + bias
        s2 = jnp.einsum('bhqd,bhkd->bhqk', q2b.astype(jnp.float32), k2) * scale + bias
        a = jax.nn.softmax(s1, axis=-1) - lam * jax.nn.softmax(s2, axis=-1)
        return jnp.einsum('bhqk,bhkd->bhqd', a.astype(v.dtype), v)

    o = _map_query_blocks(block, (q[:, :, 0], q[:, :, 1]))
    o = _rms_norm(o, g_sub) * (1.0 - lambda_init)
    return o.transpose(0, 2, 1, 3).reshape(b, s, D_DIFF_OUT)


def _mla(c_q, c_kv, k_r, g_q_a, w_uq, g_kv_a, w_ukv):
    b, s, _ = c_q.shape
    pos = jnp.arange(s)
    q = (_rms_norm(c_q, g_q_a) @ w_uq).reshape(b, s, MLA_HEADS, MLA_NOPE_DIM + MLA_ROPE_DIM).transpose(0, 2, 1, 3)
    q = jnp.concatenate([q[..., :MLA_NOPE_DIM], _rope(q[..., MLA_NOPE_DIM:], pos)], axis=-1)
    kv = (_rms_norm(c_kv, g_kv_a) @ w_ukv).reshape(b, s, MLA_HEADS, MLA_NOPE_DIM + MLA_V_DIM).transpose(0, 2, 1, 3)
    k_nope, v = kv[..., :MLA_NOPE_DIM], kv[..., MLA_NOPE_DIM:]
    k_rope = jnp.broadcast_to(_rope(k_r, pos)[:, None], (b, MLA_HEADS, s, MLA_ROPE_DIM))
    kf = jnp.concatenate([k_nope, k_rope], axis=-1).astype(jnp.float32)
    scale = (MLA_NOPE_DIM + MLA_ROPE_DIM) ** -0.5

    def block(i, qb):
        p = jax.nn.softmax(jnp.einsum('bhqd,bhkd->bhqk', qb.astype(jnp.float32), kf) * scale, axis=-1)
        return jnp.einsum('bhqk,bhkd->bhqd', p.astype(v.dtype), v)

    o = _map_query_blocks(block, (q,))
    return o.transpose(0, 2, 1, 3).reshape(b, s, D_MLA_OUT)


def _hier_moe(h, w_rg, b_rg, w_re, b_re, w_gate, w_up, w_down):
    b, s, d = h.shape
    t = h.reshape(b * s, d)
    gl = (t @ w_rg).astype(jnp.float32) + b_rg.astype(jnp.float32)
    g_idx = jnp.argmax(gl, axis=-1)
    g_w = jnp.take_along_axis(jax.nn.softmax(gl, axis=-1), g_idx[:, None], axis=-1)[:, 0]
    el = jnp.einsum('td,dge->tge', t, w_re).astype(jnp.float32) + b_re.astype(jnp.float32)
    el_sel = jnp.take_along_axis(el, g_idx[:, None, None], axis=1)[:, 0]
    top_v, top_i = lax.top_k(el_sel, TOP_K_IN_GROUP)
    top_w = jax.nn.softmax(top_v, axis=-1) * g_w[:, None]
    eid = g_idx[:, None] * EXPERTS_PER_GROUP + top_i
    gates = jnp.sum(jax.nn.one_hot(eid, N_EXPERTS, dtype=jnp.float32) * top_w[..., None], axis=1)
    y = jnp.zeros((b * s, d), jnp.float32)
    for e in range(N_EXPERTS):
        a = jax.nn.silu(t @ w_gate[e]) * (t @ w_up[e])
        y = y + gates[:, e:e + 1] * (a @ w_down[e]).astype(jnp.float32)
    return y.astype(h.dtype).reshape(b, s, d)


def _layer(x, c, layer_idx, w_ada, b_ada, g_pre_mix, g_post_mix, w_in, lambda_q1, lambda_k1,
           lambda_q2, lambda_k2, g_diff_sub, g_q_a, w_uq, g_kv_a, w_ukv, w_o, g_pre_ffn, g_post_ffn,
           w_router_group, b_router_group, w_router_expert, b_router_expert, w_gate, w_up, w_down):
    mod = (jax.nn.silu(c.astype(jnp.float32)) @ w_ada.astype(jnp.float32) + b_ada.astype(jnp.float32))
    mod = mod.astype(x.dtype)[:, None, :]
    shift_a, scale_a, gate_a, shift_f, scale_f, gate_f = jnp.split(mod, N_MOD, axis=-1)

    h = _rms_norm(x, g_pre_mix) * (1 + scale_a) + shift_a
    z = h @ w_in
    cuts = np.cumsum([DIFF_QK_WIDTH, DIFF_QK_WIDTH, DIFF_V_WIDTH, Q_LORA_RANK, KV_LORA_RANK]).tolist()
    dq, dk, dv, c_q, c_kv, k_r = jnp.split(z, cuts, axis=-1)
    lambda_init = 0.8 - 0.6 * math.exp(-0.3 * layer_idx)
    lam = (jnp.exp(jnp.sum(lambda_q1.astype(jnp.float32) * lambda_k1.astype(jnp.float32)))
           - jnp.exp(jnp.sum(lambda_q2.astype(jnp.float32) * lambda_k2.astype(jnp.float32))) + lambda_init)
    o_a = _diff_attention(dq, dk, dv, lam, g_diff_sub, lambda_init)
    o_b = _mla(c_q, c_kv, k_r, g_q_a, w_uq, g_kv_a, w_ukv)
    mix = jnp.concatenate([o_a, o_b], axis=-1) @ w_o
    x = x + gate_a * _rms_norm(mix, g_post_mix)

    h = _rms_norm(x, g_pre_ffn) * (1 + scale_f) + shift_f
    f = _hier_moe(h, w_router_group, b_router_group, w_router_expert, b_router_expert, w_gate, w_up, w_down)
    return x + gate_f * _rms_norm(f, g_post_ffn)


def _trunk(x, c, w_ada, b_ada, g_pre_mix, g_post_mix, w_in, lambda_q1, lambda_k1, lambda_q2, lambda_k2,
           g_diff_sub, g_q_a, w_uq, g_kv_a, w_ukv, w_o, g_pre_ffn, g_post_ffn, w_router_group,
           b_router_group, w_router_expert, b_router_expert, w_gate, w_up, w_down):
    for l in range(DEPTH):
        x = _layer(x, c, l, w_ada[l], b_ada[l], g_pre_mix[l], g_post_mix[l], w_in[l], lambda_q1[l],
                   lambda_k1[l], lambda_q2[l], lambda_k2[l], g_diff_sub[l], g_q_a[l], w_uq[l], g_kv_a[l],
                   w_ukv[l], w_o[l], g_pre_ffn[l], g_post_ffn[l], w_router_group[l], b_router_group[l],
                   w_router_expert[l], b_router_expert[l], w_gate[l], w_up[l], w_down[l])
    return x


def setup_inputs(seed: int = 0) -> dict:
    key = jax.random.key(seed)
    ks = jax.random.split(key, 32)
    f32 = jnp.float32
    nrm = lambda k, shape, sc: jax.random.normal(k, shape, f32) * sc
    gain = lambda k, shape: 1.0 + 0.05 * jax.random.normal(k, shape, f32)
    L, D = DEPTH, D_MODEL
    return {
        "x_prompt": nrm(ks[0], (BATCH, SEQ, D), 1.0),
        "x_sample": nrm(ks[1], (DEC_BATCH, DEC_SEQ, D), 1.0),
        "c_prompt": nrm(ks[2], (BATCH, D), 1.0),
        "c_sample": nrm(ks[3], (DEC_BATCH, D), 1.0),
        "w_ada": nrm(ks[4], (L, D, N_MOD * D), 0.5 * D ** -0.5),
        "b_ada": nrm(ks[5], (L, N_MOD * D), 0.1),
        "g_pre_mix": gain(ks[6], (L, D)),
        "g_post_mix": gain(ks[7], (L, D)),
        "w_in": nrm(ks[8], (L, D, D_IN_PROJ), D ** -0.5),
        "lambda_q1": nrm(ks[9], (L, DIFF_HEAD_DIM), 0.1),
        "lambda_k1": nrm(ks[10], (L, DIFF_HEAD_DIM), 0.1),
        "lambda_q2": nrm(ks[11], (L, DIFF_HEAD_DIM), 0.1),
        "lambda_k2": nrm(ks[12], (L, DIFF_HEAD_DIM), 0.1),
        "g_diff_sub": gain(ks[13], (L, 2 * DIFF_HEAD_DIM)),
        "g_q_a": gain(ks[14], (L, Q_LORA_RANK)),
        "w_uq": nrm(ks[15], (L, Q_LORA_RANK, MLA_HEADS * (MLA_NOPE_DIM + MLA_ROPE_DIM)), Q_LORA_RANK ** -0.5),
        "g_kv_a": gain(ks[16], (L, KV_LORA_RANK)),
        "w_ukv": nrm(ks[17], (L, KV_LORA_RANK, MLA_HEADS * (MLA_NOPE_DIM + MLA_V_DIM)), KV_LORA_RANK ** -0.5),
        "w_o": nrm(ks[18], (L, D_MIX, D), D_MIX ** -0.5),
        "g_pre_ffn": gain(ks[19], (L, D)),
        "g_post_ffn": gain(ks[20], (L, D)),
        "w_router_group": nrm(ks[21], (L, D, N_GROUPS), D ** -0.5),
        "b_router_group": nrm(ks[22], (L, N_GROUPS), 0.01),
        "w_router_expert": nrm(ks[23], (L, D, N_GROUPS, EXPERTS_PER_GROUP), D ** -0.5),
        "b_router_expert": nrm(ks[24], (L, N_GROUPS, EXPERTS_PER_GROUP), 0.01),
        "w_gate": nrm(ks[25], (L, N_EXPERTS, D, D_EXPERT), D ** -0.5),
        "w_up": nrm(ks[26], (L, N_EXPERTS, D, D_EXPERT), D ** -0.5),
        "w_down": nrm(ks[27], (L, N_EXPERTS, D_EXPERT, D), D_EXPERT ** -0.5),
    }


def reference(x_prompt, x_sample, c_prompt, c_sample, w_ada, b_ada, g_pre_mix, g_post_mix, w_in,
              lambda_q1, lambda_k1, lambda_q2, lambda_k2, g_diff_sub, g_q_a, w_uq, g_kv_a, w_ukv, w_o,
              g_pre_ffn, g_post_ffn, w_router_group, b_router_group, w_router_expert, b_router_expert,
              w_gate, w_up, w_down):
    y_prompt = _trunk(x_prompt, c_prompt, w_ada, b_ada, g_pre_mix, g_post_mix, w_in, lambda_q1, lambda_k1,
                      lambda_q2, lambda_k2, g_diff_sub, g_q_a, w_uq, g_kv_a, w_ukv, w_o, g_pre_ffn, g_post_ffn,
                      w_router_group, b_router_group, w_router_expert, b_router_expert, w_gate, w_up, w_down)
    y_sample = _trunk(x_sample, c_sample, w_ada, b_ada, g_pre_mix, g_post_mix, w_in, lambda_q1, lambda_k1,
                      lambda_q2, lambda_k2, g_diff_sub, g_q_a, w_uq, g_kv_a, w_ukv, w_o, g_pre_ffn, g_post_ffn,
                      w_router_group, b_router_group, w_router_expert, b_router_expert, w_gate, w_up, w_down)
    return (y_prompt, y_sample)
```

```python
import functools
import math

import jax
import jax.numpy as jnp
from jax import lax
from jax.experimental import pallas as pl
from jax.experimental.pallas import tpu as pltpu

F32 = jnp.float32
BF16 = jnp.bfloat16

D_MODEL = 2048
DIFF_HEADS = 8
DIFF_HEAD_DIM = 64
DIFF_WIDTH = DIFF_HEADS * 2 * DIFF_HEAD_DIM
MLA_HEADS = 8
MLA_NOPE_DIM = 128
MLA_ROPE_DIM = 64
MLA_V_DIM = 128
MLA_QK_PAD = 256
Q_LORA_RANK = 512
KV_LORA_RANK = 256
ROPE_BASE = 10000.0
D_MLA_OUT = MLA_HEADS * MLA_V_DIM
N_GROUPS = 4
EXPERTS_PER_GROUP = 4
N_EXPERTS = N_GROUPS * EXPERTS_PER_GROUP
D_EXPERT = 512
N_MOD = 6
NORM_EPS = 1e-6
LANES = 128
ROUTER_LANE0 = N_GROUPS

C_DQ, C_DK, C_DV = 0, DIFF_WIDTH, 2 * DIFF_WIDTH
C_CQ = 3 * DIFF_WIDTH
C_CKV = C_CQ + Q_LORA_RANK
C_KR = C_CKV + KV_LORA_RANK
D_IN_AUG = C_KR + 2 * MLA_ROPE_DIM

VMEM_LIMIT = 56 * 1024 * 1024


def _rms(x, g):
    return x * lax.rsqrt(jnp.mean(x * x, axis=-1, keepdims=True) + NORM_EPS) * g


def _dot(a, b):
    return jnp.dot(a, b, preferred_element_type=F32)


def _dot_nt(a, b):
    return lax.dot_general(a, b, (((1,), (1,)), ((), ())), preferred_element_type=F32)


class _Rows:
    def __init__(self, bp, sp, bs, ss, tm):
        assert sp % tm == 0 and ss % tm == 0
        self.bp, self.sp, self.bs, self.ss, self.tm = bp, sp, bs, ss, tm
        self.n_p = bp * sp // tm
        self.n_s = bs * ss // tm
        self.n = self.n_p + self.n_s
        self.tp = bp * sp
        self.ts = bs * ss
        self.t = self.tp + self.ts

    def p_idx(self, i):
        return jnp.minimum(i, self.n_p - 1)

    def s_idx(self, i):
        return jnp.maximum(i - self.n_p, 0)

    def seq(self, i):
        return jnp.where(i < self.n_p, i // (self.sp // self.tm),
                         self.bp + (i - self.n_p) // (self.ss // self.tm))

    def pos(self, i):
        return jnp.where(i < self.n_p, i % (self.sp // self.tm),
                         (i - self.n_p) % (self.ss // self.tm))


def _ada_kernel(c_ref, w_ref, b_ref, o_ref):
    c = c_ref[...]
    a = c / (1.0 + jnp.exp(-c))
    o_ref[...] = jnp.dot(a, w_ref[...], preferred_element_type=F32,
                         precision=lax.Precision.HIGHEST) + b_ref[...]


def _ada(c_all, w_ada, b_ada):
    nb, d = c_all.shape
    n = w_ada.shape[1]
    tn = 1024
    return pl.pallas_call(
        _ada_kernel,
        grid=(n // tn,),
        in_specs=[pl.BlockSpec((nb, d), lambda j: (0, 0)),
                  pl.BlockSpec((d, tn), lambda j: (0, j)),
                  pl.BlockSpec((1, tn), lambda j: (0, j))],
        out_specs=pl.BlockSpec((nb, tn), lambda j: (0, j)),
        out_shape=jax.ShapeDtypeStruct((nb, n), F32),
        compiler_params=pltpu.CompilerParams(dimension_semantics=("arbitrary",),
                                             vmem_limit_bytes=VMEM_LIMIT),
        name="ada",
    )(c_all, w_ada, b_ada)


def _rope_half(x, cos, sin):
    return x * cos + pltpu.roll(x, MLA_ROPE_DIM, 1) * sin


def _inproj_kernel(n_p, xp_ref, xs_ref, mod_ref, gpre_ref, win_ref, gqa_ref, wq_ref, gkva_ref,
                   wkv_ref, cos_ref, sin_ref, dq_ref, dk_ref, dv_ref, qm_ref, km_ref, vm_ref):
    i = pl.program_id(0)
    x = jnp.where(i < n_p, xp_ref[...], xs_ref[...])
    shift = mod_ref[0, 0:1, :]
    scale = mod_ref[0, 1:2, :]
    hb = (_rms(x, gpre_ref[...]) * (1.0 + scale) + shift).astype(BF16)

    dq_ref[...] = (_dot(hb, win_ref[:, C_DQ:C_DQ + DIFF_WIDTH]) * (DIFF_HEAD_DIM ** -0.5)).astype(BF16)
    dk_ref[...] = _dot(hb, win_ref[:, C_DK:C_DK + DIFF_WIDTH]).astype(BF16)
    dv_ref[...] = _dot(hb, win_ref[:, C_DV:C_DV + DIFF_WIDTH]).astype(BF16)

    cos = cos_ref[...]
    sin = sin_ref[...]
    cq = _dot(hb, win_ref[:, C_CQ:C_CQ + Q_LORA_RANK])
    ckv = _dot(hb, win_ref[:, C_CKV:C_CKV + KV_LORA_RANK])
    kr = _rope_half(_dot(hb, win_ref[:, C_KR:C_KR + 2 * MLA_ROPE_DIM]), cos, sin).astype(BF16)

    q = _dot(_rms(cq, gqa_ref[...]).astype(BF16), wq_ref[...])
    qscale = (MLA_NOPE_DIM + MLA_ROPE_DIM) ** -0.5
    for h in range(MLA_HEADS):
        c0 = h * MLA_QK_PAD
        qm_ref[:, c0:c0 + LANES] = (q[:, c0:c0 + LANES] * qscale).astype(BF16)
        qm_ref[:, c0 + LANES:c0 + 2 * LANES] = (
            _rope_half(q[:, c0 + LANES:c0 + 2 * LANES], cos, sin) * qscale).astype(BF16)

    kv = _dot(_rms(ckv, gkva_ref[...]).astype(BF16), wkv_ref[...])
    for h in range(MLA_HEADS):
        c0 = h * MLA_QK_PAD
        km_ref[:, c0:c0 + LANES] = kv[:, c0:c0 + LANES].astype(BF16)
        km_ref[:, c0 + LANES:c0 + 2 * LANES] = kr
        vm_ref[:, h * MLA_V_DIM:(h + 1) * MLA_V_DIM] = kv[:, c0 + LANES:c0 + 2 * LANES].astype(BF16)


def _inproj(rows, xp, xs, mod, g_pre, w_in_aug, g_q_a, w_q, g_kv_a, w_kv, cos_t, sin_t):
    tm, d = rows.tm, D_MODEL
    const = lambda i: (0, 0)
    row = lambda i: (i, 0)
    one = pl.Buffered(1)
    out_w = (DIFF_WIDTH, DIFF_WIDTH, DIFF_WIDTH, MLA_HEADS * MLA_QK_PAD, MLA_HEADS * MLA_QK_PAD, D_MLA_OUT)
    return pl.pallas_call(
        functools.partial(_inproj_kernel, rows.n_p),
        grid=(rows.n,),
        in_specs=[pl.BlockSpec((tm, d), lambda i: (rows.p_idx(i), 0)),
                  pl.BlockSpec((tm, d), lambda i: (rows.s_idx(i), 0)),
                  pl.BlockSpec((1, N_MOD, d), lambda i: (rows.seq(i), 0, 0)),
                  pl.BlockSpec((1, d), const),
                  pl.BlockSpec((d, D_IN_AUG), const, pipeline_mode=one),
                  pl.BlockSpec((1, Q_LORA_RANK), const),
                  pl.BlockSpec((Q_LORA_RANK, MLA_HEADS * MLA_QK_PAD), const, pipeline_mode=one),
                  pl.BlockSpec((1, KV_LORA_RANK), const),
                  pl.BlockSpec((KV_LORA_RANK, MLA_HEADS * MLA_QK_PAD), const, pipeline_mode=one),
                  pl.BlockSpec((tm, LANES), lambda i: (rows.pos(i), 0)),
                  pl.BlockSpec((tm, LANES), lambda i: (rows.pos(i), 0))],
        out_specs=[pl.BlockSpec((tm, w), row) for w in out_w],
        out_shape=[jax.ShapeDtypeStruct((rows.t, w), BF16) for w in out_w],
        compiler_params=pltpu.CompilerParams(dimension_semantics=("arbitrary",),
                                             vmem_limit_bytes=VMEM_LIMIT),
        name="inproj",
    )(xp, xs, mod, g_pre, w_in_aug, g_q_a, w_q, g_kv_a, w_kv, cos_t, sin_t)


def _softmax_step(s, v, m_ref, l_ref, acc_ref):
    m_old = m_ref[...]
    m_new = jnp.maximum(m_old, jnp.max(s, axis=-1, keepdims=True))
    alpha = jnp.exp(m_old - m_new)
    p = jnp.exp(s - m_new)
    l_ref[...] = alpha * l_ref[...] + jnp.sum(p, axis=-1, keepdims=True)
    acc_ref[...] = alpha * acc_ref[...] + _dot(p.astype(BF16), v)
    m_ref[...] = m_new


def _diff_attn_kernel(lambda_init, tq, tk, slopes_ref, lam_ref, gsub_ref, q_ref, k_ref, v_ref, o_ref,
                      m1, l1, a1, m2, l2, a2):
    h = pl.program_id(1)
    qi = pl.program_id(2)
    ki = pl.program_id(3)

    @pl.when(ki == 0)
    def _():
        for m, l, a in ((m1, l1, a1), (m2, l2, a2)):
            m[...] = jnp.full(m.shape, -jnp.inf, F32)
            l[...] = jnp.zeros(l.shape, F32)
            a[...] = jnp.zeros(a.shape, F32)

    q = q_ref[...]
    lane = lax.broadcasted_iota(jnp.int32, q.shape, 1)
    zero = jnp.zeros_like(q)
    q1 = jnp.where(lane < DIFF_HEAD_DIM, q, zero)
    q2 = jnp.where(lane >= DIFF_HEAD_DIM, q, zero)
    k = k_ref[...]
    v = v_ref[...]
    rel = (lax.broadcasted_iota(jnp.int32, (tq, tk), 0) - lax.broadcasted_iota(jnp.int32, (tq, tk), 1)
           + (qi * tq - ki * tk))
    bias = jnp.abs(rel).astype(F32) * (-slopes_ref[h])
    _softmax_step(_dot_nt(q1, k) + bias, v, m1, l1, a1)
    _softmax_step(_dot_nt(q2, k) + bias, v, m2, l2, a2)

    @pl.when(ki == pl.num_programs(3) - 1)
    def _():
        lp = lam_ref[...]
        lam = (jnp.exp(jnp.sum(lp[0:1] * lp[1:2], axis=-1, keepdims=True))
               - jnp.exp(jnp.sum(lp[2:3] * lp[3:4], axis=-1, keepdims=True)) + lambda_init)
        o = a1[...] / l1[...] - lam * (a2[...] / l2[...])
        o_ref[...] = (_rms(o, gsub_ref[...]) * (1.0 - lambda_init)).astype(BF16)


def _diff_attn(dq, dk, dv, slopes, lam_p, g_sub, row0, b, s, lambda_init, tq, tk):
    nq, nk = s // tq, s // tk
    q0, k0 = row0 // tq, row0 // tk
    dh = 2 * DIFF_HEAD_DIM
    return pl.pallas_call(
        functools.partial(_diff_attn_kernel, lambda_init, tq, tk),
        grid=(b, DIFF_HEADS, nq, nk),
        in_specs=[pl.BlockSpec(memory_space=pltpu.SMEM),
                  pl.BlockSpec((8, LANES), lambda bi, h, qi, ki: (0, 0)),
                  pl.BlockSpec((1, dh), lambda bi, h, qi, ki: (0, 0)),
                  pl.BlockSpec((tq, dh), lambda bi, h, qi, ki: (q0 + bi * nq + qi, h)),
                  pl.BlockSpec((tk, dh), lambda bi, h, qi, ki: (k0 + bi * nk + ki, h)),
                  pl.BlockSpec((tk, dh), lambda bi, h, qi, ki: (k0 + bi * nk + ki, h))],
        out_specs=pl.BlockSpec((tq, dh), lambda bi, h, qi, ki: (bi * nq + qi, h)),
        out_shape=jax.ShapeDtypeStruct((b * s, DIFF_WIDTH), BF16),
        scratch_shapes=[pltpu.VMEM((tq, 1), F32), pltpu.VMEM((tq, 1), F32), pltpu.VMEM((tq, dh), F32),
                        pltpu.VMEM((tq, 1), F32), pltpu.VMEM((tq, 1), F32), pltpu.VMEM((tq, dh), F32)],
        compiler_params=pltpu.CompilerParams(
            dimension_semantics=("arbitrary", "arbitrary", "arbitrary", "arbitrary"),
            vmem_limit_bytes=VMEM_LIMIT),
        name="diff_attn",
    )(slopes, lam_p, g_sub, dq, dk, dv)


def _mla_attn_kernel(q_ref, k_ref, v_ref, o_ref, m, l, a):
    ki = pl.program_id(3)

    @pl.when(ki == 0)
    def _():
        m[...] = jnp.full(m.shape, -jnp.inf, F32)
        l[...] = jnp.zeros(l.shape, F32)
        a[...] = jnp.zeros(a.shape, F32)

    _softmax_step(_dot_nt(q_ref[...], k_ref[...]), v_ref[...], m, l, a)

    @pl.when(ki == pl.num_programs(3) - 1)
    def _():
        o_ref[...] = (a[...] / l[...]).astype(BF16)


def _mla_attn(qm, km, vm, row0, b, s, tq, tk):
    nq, nk = s // tq, s // tk
    q0, k0 = row0 // tq, row0 // tk
    return pl.pallas_call(
        _mla_attn_kernel,
        grid=(b, MLA_HEADS, nq, nk),
        in_specs=[pl.BlockSpec((tq, MLA_QK_PAD), lambda bi, h, qi, ki: (q0 + bi * nq + qi, h)),
                  pl.BlockSpec((tk, MLA_QK_PAD), lambda bi, h, qi, ki: (k0 + bi * nk + ki, h)),
                  pl.BlockSpec((tk, MLA_V_DIM), lambda bi, h, qi, ki: (k0 + bi * nk + ki, h))],
        out_specs=pl.BlockSpec((tq, MLA_V_DIM), lambda bi, h, qi, ki: (bi * nq + qi, h)),
        out_shape=jax.ShapeDtypeStruct((b * s, D_MLA_OUT), BF16),
        scratch_shapes=[pltpu.VMEM((tq, 1), F32), pltpu.VMEM((tq, 1), F32),
                        pltpu.VMEM((tq, MLA_V_DIM), F32)],
        compiler_params=pltpu.CompilerParams(
            dimension_semantics=("arbitrary", "arbitrary", "arbitrary", "arbitrary"),
            vmem_limit_bytes=VMEM_LIMIT),
        name="mla_attn",
    )(qm, km, vm)


def _route(logits):
    lane = lax.broadcasted_iota(jnp.int32, logits.shape, 1)
    neg = jnp.full(logits.shape, -jnp.inf, F32)
    big = jnp.full(logits.shape, LANES, jnp.int32)
    first = lambda mask: jnp.min(jnp.where(mask, lane, big), axis=-1, keepdims=True)

    gl = jnp.where(lane < N_GROUPS, logits, neg)
    gmax = jnp.max(gl, axis=-1, keepdims=True)
    g_idx = first(gl == gmax)
    g_w = 1.0 / jnp.sum(jnp.exp(gl - gmax), axis=-1, keepdims=True)

    lo = ROUTER_LANE0 + EXPERTS_PER_GROUP * g_idx
    el = jnp.where(lane >= lo, jnp.where(lane < lo + EXPERTS_PER_GROUP, logits, neg), neg)
    v1 = jnp.max(el, axis=-1, keepdims=True)
    i1 = first(el == v1)
    el2 = jnp.where(lane == i1, neg, el)
    v2 = jnp.max(el2, axis=-1, keepdims=True)
    i2 = first(el2 == v2)
    t = jnp.exp(v2 - v1)
    w1 = g_w / (1.0 + t)
    w2 = w1 * t
    zero = jnp.zeros_like(logits)
    return jnp.where(lane == i1, w1, zero) + jnp.where(lane == i2, w2, zero)


def _oproj_kernel(n_p, oap_ref, oas_ref, obp_ref, obs_ref, xp_ref, xs_ref, mod_ref, wo_ref, gpost_ref,
                  gpre_ref, wr_ref, br_ref, xmid_ref, h2_ref, gates_ref):
    i = pl.program_id(0)
    is_p = i < n_p
    oa = jnp.where(is_p, oap_ref[...], oas_ref[...])
    ob = jnp.where(is_p, obp_ref[...], obs_ref[...])
    x = jnp.where(is_p, xp_ref[...], xs_ref[...])
    mix = _dot(oa, wo_ref[0:DIFF_WIDTH, :]) + _dot(ob, wo_ref[DIFF_WIDTH:DIFF_WIDTH + D_MLA_OUT, :])
    x_mid = x + mod_ref[0, 2:3, :] * _rms(mix, gpost_ref[...])
    xmid_ref[...] = x_mid
    h2 = _rms(x_mid, gpre_ref[...]) * (1.0 + mod_ref[0, 4:5, :]) + mod_ref[0, 3:4, :]
    h2_ref[...] = h2.astype(BF16)
    logits = jnp.dot(h2, wr_ref[...], preferred_element_type=F32,
                     precision=lax.Precision.HIGHEST) + br_ref[...]
    gates_ref[...] = _route(logits)


def _oproj(rows, oa_p, oa_s, ob_p, ob_s, xp, xs, mod, w_o, g_post, g_pre, w_r, b_r):
    tm, d = rows.tm, D_MODEL
    const = lambda i: (0, 0)
    row = lambda i: (i, 0)
    prow = lambda i: (rows.p_idx(i), 0)
    srow = lambda i: (rows.s_idx(i), 0)
    return pl.pallas_call(
        functools.partial(_oproj_kernel, rows.n_p),
        grid=(rows.n,),
        in_specs=[pl.BlockSpec((tm, DIFF_WIDTH), prow), pl.BlockSpec((tm, DIFF_WIDTH), srow),
                  pl.BlockSpec((tm, D_MLA_OUT), prow), pl.BlockSpec((tm, D_MLA_OUT), srow),
                  pl.BlockSpec((tm, d), prow), pl.BlockSpec((tm, d), srow),
                  pl.BlockSpec((1, N_MOD, d), lambda i: (rows.seq(i), 0, 0)),
                  pl.BlockSpec((DIFF_WIDTH + D_MLA_OUT, d), const, pipeline_mode=pl.Buffered(1)),
                  pl.BlockSpec((1, d), const), pl.BlockSpec((1, d), const),
                  pl.BlockSpec((d, LANES), const), pl.BlockSpec((1, LANES), const)],
        out_specs=[pl.BlockSpec((tm, d), row), pl.BlockSpec((tm, d), row), pl.BlockSpec((tm, LANES), row)],
        out_shape=[jax.ShapeDtypeStruct((rows.t, d), F32), jax.ShapeDtypeStruct((rows.t, d), BF16),
                   jax.ShapeDtypeStruct((rows.t, LANES), F32)],
        compiler_params=pltpu.CompilerParams(dimension_semantics=("arbitrary",),
                                             vmem_limit_bytes=VMEM_LIMIT),
        name="oproj",
    )(oa_p, oa_s, ob_p, ob_s, xp, xs, mod, w_o, g_post, g_pre, w_r, b_r)


def _moe_kernel(n_p, h_ref, gates_ref, wg_ref, wu_ref, wd_ref, xmid_ref, mod_ref, gpost_ref,
                yp_ref, ys_ref, acc_ref):
    i = pl.program_id(0)
    e = pl.program_id(1)

    @pl.when(e == 0)
    def _():
        acc_ref[...] = jnp.zeros(acc_ref.shape, F32)

    x = h_ref[...]
    g = _dot(x, wg_ref[0])
    u = _dot(x, wu_ref[0])
    gates = gates_ref[...]
    lane = lax.broadcasted_iota(jnp.int32, gates.shape, 1)
    gate = jnp.sum(jnp.where(lane == e + ROUTER_LANE0, gates, jnp.zeros_like(gates)), axis=-1, keepdims=True)
    a = (g / (1.0 + jnp.exp(-g))) * u * gate
    acc_ref[...] += _dot(a.astype(BF16), wd_ref[0])

    last = e == pl.num_programs(1) - 1

    def final():
        return xmid_ref[...] + mod_ref[0, 5:6, :] * _rms(acc_ref[...], gpost_ref[...])

    @pl.when(jnp.logical_and(last, i < n_p))
    def _():
        yp_ref[...] = final()

    @pl.when(jnp.logical_and(last, i >= n_p))
    def _():
        ys_ref[...] = final()


def _moe(rows, h2, gates, w_gate, w_up, w_down, x_mid, mod, g_post):
    tm, d = rows.tm, D_MODEL
    return pl.pallas_call(
        functools.partial(_moe_kernel, rows.n_p),
        grid=(rows.n, N_EXPERTS),
        in_specs=[pl.BlockSpec((tm, d), lambda i, e: (i, 0)),
                  pl.BlockSpec((tm, LANES), lambda i, e: (i, 0)),
                  pl.BlockSpec((1, d, D_EXPERT), lambda i, e: (e, 0, 0)),
                  pl.BlockSpec((1, d, D_EXPERT), lambda i, e: (e, 0, 0)),
                  pl.BlockSpec((1, D_EXPERT, d), lambda i, e: (e, 0, 0)),
                  pl.BlockSpec((tm, d), lambda i, e: (i, 0)),
                  pl.BlockSpec((1, N_MOD, d), lambda i, e: (rows.seq(i), 0, 0)),
                  pl.BlockSpec((1, d), lambda i, e: (0, 0))],
        out_specs=[pl.BlockSpec((tm, d), lambda i, e: (rows.p_idx(i), 0)),
                   pl.BlockSpec((tm, d), lambda i, e: (rows.s_idx(i), 0))],
        out_shape=[jax.ShapeDtypeStruct((rows.tp, d), F32), jax.ShapeDtypeStruct((rows.ts, d), F32)],
        scratch_shapes=[pltpu.VMEM((tm, d), F32)],
        compiler_params=pltpu.CompilerParams(dimension_semantics=("arbitrary", "arbitrary"),
                                             vmem_limit_bytes=VMEM_LIMIT),
        name="moe",
    )(h2, gates, w_gate, w_up, w_down, x_mid, mod, g_post)


def _rotate_half_cols(w):
    half = MLA_ROPE_DIM // 2
    return jnp.concatenate([-w[..., half:], w[..., :half]], axis=-1)


def _rope_tables(s_max):
    half = MLA_ROPE_DIM // 2
    inv = ROPE_BASE ** (-jnp.arange(half, dtype=F32) / half)
    ang = jnp.arange(s_max, dtype=F32)[:, None] * inv[None, :]
    pad = jnp.zeros((s_max, LANES - MLA_ROPE_DIM), F32)
    cos, sin = jnp.cos(ang), jnp.sin(ang)
    return (jnp.concatenate([cos, cos, pad], axis=-1), jnp.concatenate([sin, sin, pad], axis=-1))


def _layer(x_prompt, x_sample, c_prompt, c_sample, layer_idx, w_ada, b_ada, g_pre_mix, g_post_mix, w_in,
           lambda_q1, lambda_k1, lambda_q2, lambda_k2, g_diff_sub, g_q_a, w_uq, g_kv_a, w_ukv, w_o,
           g_pre_ffn, g_post_ffn, w_router_group, b_router_group, w_router_expert, b_router_expert,
           w_gate, w_up, w_down, tm=256, tm_moe=512, tq=256, tk=512):
    bp, sp, d = x_prompt.shape
    bs, ss, _ = x_sample.shape
    rows = _Rows(bp, sp, bs, ss, tm)
    xp = x_prompt.reshape(bp * sp, d)
    xs = x_sample.reshape(bs * ss, d)
    lambda_init = 0.8 - 0.6 * math.exp(-0.3 * layer_idx)

    nb = bp + bs
    nb_pad = -(-nb // 8) * 8
    c_all = jnp.concatenate([c_prompt, c_sample, jnp.zeros((nb_pad - nb, d), F32)], axis=0)
    mod = _ada(c_all, w_ada, b_ada.reshape(1, -1)).reshape(nb_pad, N_MOD, d)

    w_in_aug = jnp.concatenate([w_in, _rotate_half_cols(w_in[:, C_KR:C_KR + MLA_ROPE_DIM])], axis=1).astype(BF16)
    wq3 = w_uq.reshape(Q_LORA_RANK, MLA_HEADS, MLA_NOPE_DIM + MLA_ROPE_DIM)
    wq_rope = wq3[..., MLA_NOPE_DIM:]
    w_q = jnp.concatenate([wq3[..., :MLA_NOPE_DIM], wq_rope, _rotate_half_cols(wq_rope)], axis=-1)
    w_q = w_q.reshape(Q_LORA_RANK, MLA_HEADS * MLA_QK_PAD).astype(BF16)
    w_kv = w_ukv.astype(BF16)
    cos_t, sin_t = _rope_tables(max(sp, ss))

    dq, dk, dv, qm, km, vm = _inproj(rows, xp, xs, mod, g_pre_mix.reshape(1, d), w_in_aug,
                                     g_q_a.reshape(1, -1), w_q, g_kv_a.reshape(1, -1), w_kv, cos_t, sin_t)

    slopes = jnp.array([2.0 ** (-8.0 * (i + 1) / DIFF_HEADS) for i in range(DIFF_HEADS)], dtype=F32)
    lam_p = jnp.zeros((8, LANES), F32).at[0:4, 0:DIFF_HEAD_DIM].set(
        jnp.stack([lambda_q1, lambda_k1, lambda_q2, lambda_k2]))
    g_sub = g_diff_sub.reshape(1, -1)
    oa_p = _diff_attn(dq, dk, dv, slopes, lam_p, g_sub, 0, bp, sp, lambda_init, tq, min(tk, sp))
    oa_s = _diff_attn(dq, dk, dv, slopes, lam_p, g_sub, rows.tp, bs, ss, lambda_init, tq, min(tk, ss))
    ob_p = _mla_attn(qm, km, vm, 0, bp, sp, tq, min(tk, sp))
    ob_s = _mla_attn(qm, km, vm, rows.tp, bs, ss, tq, min(tk, ss))

    w_r = jnp.concatenate([w_router_group, w_router_expert.reshape(d, N_EXPERTS),
                           jnp.zeros((d, LANES - N_GROUPS - N_EXPERTS), F32)], axis=1)
    b_r = jnp.concatenate([b_router_group, b_router_expert.reshape(N_EXPERTS),
                           jnp.zeros((LANES - N_GROUPS - N_EXPERTS,), F32)]).reshape(1, LANES)
    x_mid, h2, gates = _oproj(rows, oa_p, oa_s, ob_p, ob_s, xp, xs, mod, w_o.astype(BF16),
                              g_post_mix.reshape(1, d), g_pre_ffn.reshape(1, d), w_r, b_r)

    rows_moe = _Rows(bp, sp, bs, ss, tm_moe)
    yp, ys = _moe(rows_moe, h2, gates, w_gate.astype(BF16), w_up.astype(BF16), w_down.astype(BF16),
                  x_mid, mod, g_post_ffn.reshape(1, d))
    return yp.reshape(bp, sp, d), ys.reshape(bs, ss, d)


def kernel(x_prompt, x_sample, c_prompt, c_sample, w_ada, b_ada, g_pre_mix, g_post_mix, w_in, lambda_q1,
           lambda_k1, lambda_q2, lambda_k2, g_diff_sub, g_q_a, w_uq, g_kv_a, w_ukv, w_o, g_pre_ffn,
           g_post_ffn, w_router_group, b_router_group, w_router_expert, b_router_expert, w_gate, w_up,
           w_down):
    for l in range(w_ada.shape[0]):
        x_prompt, x_sample = _layer(
            x_prompt, x_sample, c_prompt, c_sample, l, w_ada[l], b_ada[l], g_pre_mix[l], g_post_mix[l],
            w_in[l], lambda_q1[l], lambda_k1[l], lambda_q2[l], lambda_k2[l], g_diff_sub[l], g_q_a[l],
            w_uq[l], g_kv_a[l], w_ukv[l], w_o[l], g_pre_ffn[l], g_post_ffn[l], w_router_group[l],
            b_router_group[l], w_router_expert[l], b_router_expert[l], w_gate[l], w_up[l], w_down[l])
    return x_prompt, x_sample
```

```python
import functools
import math

import jax
import jax.numpy as jnp
from jax import lax
from jax.experimental import pallas as pl
from jax.experimental.pallas import tpu as pltpu

F32 = jnp.float32
BF16 = jnp.bfloat16

D_MODEL = 2048
DIFF_HEADS = 8
DIFF_HEAD_DIM = 64
DIFF_WIDTH = DIFF_HEADS * 2 * DIFF_HEAD_DIM
MLA_HEADS = 8
MLA_NOPE_DIM = 128
MLA_ROPE_DIM = 64
MLA_V_DIM = 128
MLA_QK_PAD = 256
Q_LORA_RANK = 512
KV_LORA_RANK = 256
ROPE_BASE = 10000.0
D_MLA_OUT = MLA_HEADS * MLA_V_DIM
N_GROUPS = 4
EXPERTS_PER_GROUP = 4
N_EXPERTS = N_GROUPS * EXPERTS_PER_GROUP
D_EXPERT = 512
N_MOD = 6
NORM_EPS = 1e-6
LANES = 128
ROUTER_LANE0 = N_GROUPS

C_DQ, C_DK, C_DV = 0, DIFF_WIDTH, 2 * DIFF_WIDTH
C_CQ = 3 * DIFF_WIDTH
C_CKV = C_CQ + Q_LORA_RANK
C_KR = C_CKV + KV_LORA_RANK
D_IN_AUG = C_KR + 2 * MLA_ROPE_DIM

VMEM_LIMIT = 56 * 1024 * 1024
LOG2E = 1.4426950408889634
ATTN_TQ = 256
ATTN_TK = 512
ATTN_TQ_OUTER = 1024
ATTN_UNROLL = 4


def _rms(x, g):
    return x * lax.rsqrt(jnp.mean(x * x, axis=-1, keepdims=True) + NORM_EPS) * g


def _dot(a, b):
    return jnp.dot(a, b, preferred_element_type=F32)


def _dot_nt(a, b):
    return lax.dot_general(a, b, (((1,), (1,)), ((), ())), preferred_element_type=F32)


class _Rows:
    def __init__(self, bp, sp, bs, ss, tm):
        assert sp % tm == 0 and ss % tm == 0
        self.bp, self.sp, self.bs, self.ss, self.tm = bp, sp, bs, ss, tm
        self.n_p = bp * sp // tm
        self.n_s = bs * ss // tm
        self.n = self.n_p + self.n_s
        self.tp = bp * sp
        self.ts = bs * ss
        self.t = self.tp + self.ts

    def p_idx(self, i):
        return jnp.minimum(i, self.n_p - 1)

    def s_idx(self, i):
        return jnp.maximum(i - self.n_p, 0)

    def seq(self, i):
        return jnp.where(i < self.n_p, i // (self.sp // self.tm),
                         self.bp + (i - self.n_p) // (self.ss // self.tm))

    def pos(self, i):
        return jnp.where(i < self.n_p, i % (self.sp // self.tm),
                         (i - self.n_p) % (self.ss // self.tm))


def _ada_kernel(c_ref, w_ref, b_ref, o_ref):
    c = c_ref[...]
    a = c / (1.0 + jnp.exp(-c))
    o_ref[...] = jnp.dot(a, w_ref[...], preferred_element_type=F32,
                         precision=lax.Precision.HIGHEST) + b_ref[...]


def _ada(c_all, w_ada, b_ada):
    nb, d = c_all.shape
    n = w_ada.shape[1]
    tn = 1024
    return pl.pallas_call(
        _ada_kernel,
        grid=(n // tn,),
        in_specs=[pl.BlockSpec((nb, d), lambda j: (0, 0)),
                  pl.BlockSpec((d, tn), lambda j: (0, j)),
                  pl.BlockSpec((1, tn), lambda j: (0, j))],
        out_specs=pl.BlockSpec((nb, tn), lambda j: (0, j)),
        out_shape=jax.ShapeDtypeStruct((nb, n), F32),
        compiler_params=pltpu.CompilerParams(dimension_semantics=("arbitrary",),
                                             vmem_limit_bytes=VMEM_LIMIT),
        name="ada",
    )(c_all, w_ada, b_ada)


def _rope_half(x, cos, sin):
    return x * cos + pltpu.roll(x, MLA_ROPE_DIM, 1) * sin


def _inproj_kernel(n_p, xp_ref, xs_ref, mod_ref, gpre_ref, win_ref, gqa_ref, wq_ref, gkva_ref,
                   wkv_ref, cos_ref, sin_ref, dq_ref, dk_ref, dv_ref, qm_ref, km_ref, vm_ref):
    i = pl.program_id(0)
    x = jnp.where(i < n_p, xp_ref[...], xs_ref[...])
    shift = mod_ref[0, 0:1, :]
    scale = mod_ref[0, 1:2, :]
    hb = (_rms(x, gpre_ref[...]) * (1.0 + scale) + shift).astype(BF16)

    dq_ref[...] = (_dot(hb, win_ref[:, C_DQ:C_DQ + DIFF_WIDTH]) * (DIFF_HEAD_DIM ** -0.5 * LOG2E)).astype(BF16)
    dk_ref[...] = _dot(hb, win_ref[:, C_DK:C_DK + DIFF_WIDTH]).astype(BF16)
    dv_ref[...] = _dot(hb, win_ref[:, C_DV:C_DV + DIFF_WIDTH]).astype(BF16)

    cos = cos_ref[...]
    sin = sin_ref[...]
    cq = _dot(hb, win_ref[:, C_CQ:C_CQ + Q_LORA_RANK])
    ckv = _dot(hb, win_ref[:, C_CKV:C_CKV + KV_LORA_RANK])
    kr = _rope_half(_dot(hb, win_ref[:, C_KR:C_KR + 2 * MLA_ROPE_DIM]), cos, sin).astype(BF16)

    q = _dot(_rms(cq, gqa_ref[...]).astype(BF16), wq_ref[...])
    qscale = (MLA_NOPE_DIM + MLA_ROPE_DIM) ** -0.5 * LOG2E
    for h in range(MLA_HEADS):
        c0 = h * MLA_QK_PAD
        qm_ref[:, c0:c0 + LANES] = (q[:, c0:c0 + LANES] * qscale).astype(BF16)
        qm_ref[:, c0 + LANES:c0 + 2 * LANES] = (
            _rope_half(q[:, c0 + LANES:c0 + 2 * LANES], cos, sin) * qscale).astype(BF16)

    kv = _dot(_rms(ckv, gkva_ref[...]).astype(BF16), wkv_ref[...])
    for h in range(MLA_HEADS):
        c0 = h * MLA_QK_PAD
        km_ref[:, c0:c0 + LANES] = kv[:, c0:c0 + LANES].astype(BF16)
        km_ref[:, c0 + LANES:c0 + 2 * LANES] = kr
        vm_ref[:, h * MLA_V_DIM:(h + 1) * MLA_V_DIM] = kv[:, c0 + LANES:c0 + 2 * LANES].astype(BF16)


def _inproj(rows, xp, xs, mod, g_pre, w_in_aug, g_q_a, w_q, g_kv_a, w_kv, cos_t, sin_t):
    tm, d = rows.tm, D_MODEL
    const = lambda i: (0, 0)
    row = lambda i: (i, 0)
    one = pl.Buffered(1)
    out_w = (DIFF_WIDTH, DIFF_WIDTH, DIFF_WIDTH, MLA_HEADS * MLA_QK_PAD, MLA_HEADS * MLA_QK_PAD, D_MLA_OUT)
    return pl.pallas_call(
        functools.partial(_inproj_kernel, rows.n_p),
        grid=(rows.n,),
        in_specs=[pl.BlockSpec((tm, d), lambda i: (rows.p_idx(i), 0)),
                  pl.BlockSpec((tm, d), lambda i: (rows.s_idx(i), 0)),
                  pl.BlockSpec((1, N_MOD, d), lambda i: (rows.seq(i), 0, 0)),
                  pl.BlockSpec((1, d), const),
                  pl.BlockSpec((d, D_IN_AUG), const, pipeline_mode=one),
                  pl.BlockSpec((1, Q_LORA_RANK), const),
                  pl.BlockSpec((Q_LORA_RANK, MLA_HEADS * MLA_QK_PAD), const, pipeline_mode=one),
                  pl.BlockSpec((1, KV_LORA_RANK), const),
                  pl.BlockSpec((KV_LORA_RANK, MLA_HEADS * MLA_QK_PAD), const, pipeline_mode=one),
                  pl.BlockSpec((tm, LANES), lambda i: (rows.pos(i), 0)),
                  pl.BlockSpec((tm, LANES), lambda i: (rows.pos(i), 0))],
        out_specs=[pl.BlockSpec((tm, w), row) for w in out_w],
        out_shape=[jax.ShapeDtypeStruct((rows.t, w), BF16) for w in out_w],
        compiler_params=pltpu.CompilerParams(dimension_semantics=("arbitrary",),
                                             vmem_limit_bytes=VMEM_LIMIT),
        name="inproj",
    )(xp, xs, mod, g_pre, w_in_aug, g_q_a, w_q, g_kv_a, w_kv, cos_t, sin_t)


def _lane_groups(x, op):
    out = x[:, 0:LANES]
    for g in range(1, x.shape[1] // LANES):
        out = op(out, x[:, g * LANES:(g + 1) * LANES])
    return out


def _chunk(ref, j, tk):
    return ref[pl.ds(pl.multiple_of(j * tk, tk), tk), :]


def _two_pass(score_fn, shift_fn, v_ref, s_buf, nk, tk, tq):
    unroll = min(ATTN_UNROLL, nk)

    def pass_a(j, mpart):
        s = score_fn(j)
        s_buf[j] = s
        part = _lane_groups(s, jnp.maximum)
        if shift_fn is not None:
            part = part + shift_fn(j)
        return jnp.maximum(mpart, part)

    mpart = lax.fori_loop(0, nk, pass_a, jnp.full((tq, LANES), -jnp.inf, F32), unroll=unroll)
    m = jnp.max(mpart, axis=-1, keepdims=True)

    def pass_b(j, carry):
        lpart, acc = carry
        mj = m if shift_fn is None else m - shift_fn(j)
        p = jnp.exp2(s_buf[j] - mj)
        return lpart + _lane_groups(p, jnp.add), acc + _dot(p.astype(BF16), _chunk(v_ref, j, tk))

    zeros = jnp.zeros((tq, LANES), F32)
    lpart, acc = lax.fori_loop(0, nk, pass_b, (zeros, zeros), unroll=unroll)
    return acc / jnp.sum(lpart, axis=-1, keepdims=True)


def _diff_attn_kernel(lambda_init, tq, tk, nk, n_sub, slopes_ref, lam_ref, gsub_ref, q_ref, k_ref, v_ref,
                      o_ref, s_buf, bias_buf):
    h = pl.program_id(1)
    qo = pl.program_id(2)
    slope = slopes_ref[h] * LOG2E
    n_cross = tk // tq

    r_minus_c = (lax.broadcasted_iota(jnp.int32, (tq, tk), 0)
                 - lax.broadcasted_iota(jnp.int32, (tq, tk), 1)).astype(F32)
    bias_buf[0] = r_minus_c * (-slope)
    bias_buf[1] = r_minus_c * slope
    for c in range(n_cross):
        bias_buf[2 + c] = jnp.abs(r_minus_c + float(c * tq)) * (-slope)

    lp = lam_ref[...]
    lam = (jnp.exp(jnp.sum(lp[0:1] * lp[1:2], axis=-1, keepdims=True))
           - jnp.exp(jnp.sum(lp[2:3] * lp[3:4], axis=-1, keepdims=True)) + lambda_init)
    gsub = gsub_ref[...]

    def sub_tile(sub, carry):
        row = pl.multiple_of(sub * tq, tq)
        qbase = (qo * n_sub + sub) * tq
        jc = qbase // tk
        cross = 2 + (qbase - jc * tk) // tq
        q = q_ref[pl.ds(row, tq), :]
        lane = lax.broadcasted_iota(jnp.int32, q.shape, 1)
        shift = lambda j: jnp.where(j == jc, 0.0, -slope * jnp.abs(qbase - j * tk).astype(F32))
        which = lambda j: jnp.where(j < jc, 0, jnp.where(j > jc, 1, cross))
        outs = []
        for half in range(2):
            keep = (lane < DIFF_HEAD_DIM) if half == 0 else (lane >= DIFF_HEAD_DIM)
            qh = jnp.where(keep, q, jnp.zeros_like(q))
            score = lambda j: _dot_nt(qh, _chunk(k_ref, j, tk)) + bias_buf[which(j)]
            outs.append(_two_pass(score, shift, v_ref, s_buf, nk, tk, tq))
        o = outs[0] - lam * outs[1]
        o_ref[pl.ds(row, tq), :] = (_rms(o, gsub) * (1.0 - lambda_init)).astype(BF16)
        return carry

    lax.fori_loop(0, n_sub, sub_tile, 0)


def _attn_tiles(s):
    tq = min(ATTN_TQ, s)
    tk = min(ATTN_TK, s)
    tq_outer = min(ATTN_TQ_OUTER, s)
    assert s % tq_outer == 0 and tq_outer % tq == 0 and s % tk == 0 and tk % tq == 0
    return tq, tk, tq_outer


def _diff_attn(dq, dk, dv, slopes, lam_p, g_sub, row0, b, s, lambda_init):
    tq, tk, tq_outer = _attn_tiles(s)
    nk, n_sub, nqo = s // tk, tq_outer // tq, s // tq_outer
    dh = 2 * DIFF_HEAD_DIM
    return pl.pallas_call(
        functools.partial(_diff_attn_kernel, lambda_init, tq, tk, nk, n_sub),
        grid=(b, DIFF_HEADS, nqo),
        in_specs=[pl.BlockSpec(memory_space=pltpu.SMEM),
                  pl.BlockSpec((8, LANES), lambda bi, h, qi: (0, 0)),
                  pl.BlockSpec((1, dh), lambda bi, h, qi: (0, 0)),
                  pl.BlockSpec((tq_outer, dh), lambda bi, h, qi: (row0 // tq_outer + bi * nqo + qi, h)),
                  pl.BlockSpec((s, dh), lambda bi, h, qi: (row0 // s + bi, h)),
                  pl.BlockSpec((s, dh), lambda bi, h, qi: (row0 // s + bi, h))],
        out_specs=pl.BlockSpec((tq_outer, dh), lambda bi, h, qi: (bi * nqo + qi, h)),
        out_shape=jax.ShapeDtypeStruct((b * s, DIFF_WIDTH), BF16),
        scratch_shapes=[pltpu.VMEM((nk, tq, tk), F32), pltpu.VMEM((2 + tk // tq, tq, tk), F32)],
        compiler_params=pltpu.CompilerParams(
            dimension_semantics=("arbitrary", "arbitrary", "arbitrary"), vmem_limit_bytes=VMEM_LIMIT),
        name="diff_attn",
    )(slopes, lam_p, g_sub, dq, dk, dv)


def _mla_attn_kernel(tq, tk, nk, n_sub, q_ref, k_ref, v_ref, o_ref, s_buf):
    def sub_tile(sub, carry):
        row = pl.multiple_of(sub * tq, tq)
        q = q_ref[pl.ds(row, tq), :]
        score = lambda j: _dot_nt(q, _chunk(k_ref, j, tk))
        o_ref[pl.ds(row, tq), :] = _two_pass(score, None, v_ref, s_buf, nk, tk, tq).astype(BF16)
        return carry

    lax.fori_loop(0, n_sub, sub_tile, 0)


def _mla_attn(qm, km, vm, row0, b, s):
    tq, tk, tq_outer = _attn_tiles(s)
    nk, n_sub, nqo = s // tk, tq_outer // tq, s // tq_outer
    return pl.pallas_call(
        functools.partial(_mla_attn_kernel, tq, tk, nk, n_sub),
        grid=(b, MLA_HEADS, nqo),
        in_specs=[pl.BlockSpec((tq_outer, MLA_QK_PAD), lambda bi, h, qi: (row0 // tq_outer + bi * nqo + qi, h)),
                  pl.BlockSpec((s, MLA_QK_PAD), lambda bi, h, qi: (row0 // s + bi, h)),
                  pl.BlockSpec((s, MLA_V_DIM), lambda bi, h, qi: (row0 // s + bi, h))],
        out_specs=pl.BlockSpec((tq_outer, MLA_V_DIM), lambda bi, h, qi: (bi * nqo + qi, h)),
        out_shape=jax.ShapeDtypeStruct((b * s, D_MLA_OUT), BF16),
        scratch_shapes=[pltpu.VMEM((nk, tq, tk), F32)],
        compiler_params=pltpu.CompilerParams(
            dimension_semantics=("arbitrary", "arbitrary", "arbitrary"), vmem_limit_bytes=VMEM_LIMIT),
        name="mla_attn",
    )(qm, km, vm)


def _route(logits):
    lane = lax.broadcasted_iota(jnp.int32, logits.shape, 1)
    neg = jnp.full(logits.shape, -jnp.inf, F32)
    big = jnp.full(logits.shape, LANES, jnp.int32)
    first = lambda mask: jnp.min(jnp.where(mask, lane, big), axis=-1, keepdims=True)

    gl = jnp.where(lane < N_GROUPS, logits, neg)
    gmax = jnp.max(gl, axis=-1, keepdims=True)
    g_idx = first(gl == gmax)
    g_w = 1.0 / jnp.sum(jnp.exp(gl - gmax), axis=-1, keepdims=True)

    lo = ROUTER_LANE0 + EXPERTS_PER_GROUP * g_idx
    el = jnp.where(lane >= lo, jnp.where(lane < lo + EXPERTS_PER_GROUP, logits, neg), neg)
    v1 = jnp.max(el, axis=-1, keepdims=True)
    i1 = first(el == v1)
    el2 = jnp.where(lane == i1, neg, el)
    v2 = jnp.max(el2, axis=-1, keepdims=True)
    i2 = first(el2 == v2)
    t = jnp.exp(v2 - v1)
    w1 = g_w / (1.0 + t)
    w2 = w1 * t
    zero = jnp.zeros_like(logits)
    return jnp.where(lane == i1, w1, zero) + jnp.where(lane == i2, w2, zero)


def _oproj_kernel(n_p, oap_ref, oas_ref, obp_ref, obs_ref, xp_ref, xs_ref, mod_ref, wo_ref, gpost_ref,
                  gpre_ref, wr_ref, br_ref, xmid_ref, h2_ref, gates_ref):
    i = pl.program_id(0)
    is_p = i < n_p
    oa = jnp.where(is_p, oap_ref[...], oas_ref[...])
    ob = jnp.where(is_p, obp_ref[...], obs_ref[...])
    x = jnp.where(is_p, xp_ref[...], xs_ref[...])
    mix = _dot(oa, wo_ref[0:DIFF_WIDTH, :]) + _dot(ob, wo_ref[DIFF_WIDTH:DIFF_WIDTH + D_MLA_OUT, :])
    x_mid = x + mod_ref[0, 2:3, :] * _rms(mix, gpost_ref[...])
    xmid_ref[...] = x_mid
    h2 = _rms(x_mid, gpre_ref[...]) * (1.0 + mod_ref[0, 4:5, :]) + mod_ref[0, 3:4, :]
    h2_ref[...] = h2.astype(BF16)
    logits = jnp.dot(h2, wr_ref[...], preferred_element_type=F32,
                     precision=lax.Precision.HIGHEST) + br_ref[...]
    gates_ref[...] = _route(logits)


def _oproj(rows, oa_p, oa_s, ob_p, ob_s, xp, xs, mod, w_o, g_post, g_pre, w_r, b_r):
    tm, d = rows.tm, D_MODEL
    const = lambda i: (0, 0)
    row = lambda i: (i, 0)
    prow = lambda i: (rows.p_idx(i), 0)
    srow = lambda i: (rows.s_idx(i), 0)
    return pl.pallas_call(
        functools.partial(_oproj_kernel, rows.n_p),
        grid=(rows.n,),
        in_specs=[pl.BlockSpec((tm, DIFF_WIDTH), prow), pl.BlockSpec((tm, DIFF_WIDTH), srow),
                  pl.BlockSpec((tm, D_MLA_OUT), prow), pl.BlockSpec((tm, D_MLA_OUT), srow),
                  pl.BlockSpec((tm, d), prow), pl.BlockSpec((tm, d), srow),
                  pl.BlockSpec((1, N_MOD, d), lambda i: (rows.seq(i), 0, 0)),
                  pl.BlockSpec((DIFF_WIDTH + D_MLA_OUT, d), const, pipeline_mode=pl.Buffered(1)),
                  pl.BlockSpec((1, d), const), pl.BlockSpec((1, d), const),
                  pl.BlockSpec((d, LANES), const), pl.BlockSpec((1, LANES), const)],
        out_specs=[pl.BlockSpec((tm, d), row), pl.BlockSpec((tm, d), row), pl.BlockSpec((tm, LANES), row)],
        out_shape=[jax.ShapeDtypeStruct((rows.t, d), F32), jax.ShapeDtypeStruct((rows.t, d), BF16),
                   jax.ShapeDtypeStruct((rows.t, LANES), F32)],
        compiler_params=pltpu.CompilerParams(dimension_semantics=("arbitrary",),
                                             vmem_limit_bytes=VMEM_LIMIT),
        name="oproj",
    )(oa_p, oa_s, ob_p, ob_s, xp, xs, mod, w_o, g_post, g_pre, w_r, b_r)


def _moe_kernel(n_p, h_ref, gates_ref, wg_ref, wu_ref, wd_ref, xmid_ref, mod_ref, gpost_ref,
                yp_ref, ys_ref, acc_ref):
    i = pl.program_id(0)
    e = pl.program_id(1)

    @pl.when(e == 0)
    def _():
        acc_ref[...] = jnp.zeros(acc_ref.shape, F32)

    x = h_ref[...]
    g = _dot(x, wg_ref[0])
    u = _dot(x, wu_ref[0])
    gates = gates_ref[...]
    lane = lax.broadcasted_iota(jnp.int32, gates.shape, 1)
    gate = jnp.sum(jnp.where(lane == e + ROUTER_LANE0, gates, jnp.zeros_like(gates)), axis=-1, keepdims=True)
    a = (g / (1.0 + jnp.exp(-g))) * u * gate
    acc_ref[...] += _dot(a.astype(BF16), wd_ref[0])

    last = e == pl.num_programs(1) - 1

    def final():
        return xmid_ref[...] + mod_ref[0, 5:6, :] * _rms(acc_ref[...], gpost_ref[...])

    @pl.when(jnp.logical_and(last, i < n_p))
    def _():
        yp_ref[...] = final()

    @pl.when(jnp.logical_and(last, i >= n_p))
    def _():
        ys_ref[...] = final()


def _moe(rows, h2, gates, w_gate, w_up, w_down, x_mid, mod, g_post):
    tm, d = rows.tm, D_MODEL
    return pl.pallas_call(
        functools.partial(_moe_kernel, rows.n_p),
        grid=(rows.n, N_EXPERTS),
        in_specs=[pl.BlockSpec((tm, d), lambda i, e: (i, 0)),
                  pl.BlockSpec((tm, LANES), lambda i, e: (i, 0)),
                  pl.BlockSpec((1, d, D_EXPERT), lambda i, e: (e, 0, 0)),
                  pl.BlockSpec((1, d, D_EXPERT), lambda i, e: (e, 0, 0)),
                  pl.BlockSpec((1, D_EXPERT, d), lambda i, e: (e, 0, 0)),
                  pl.BlockSpec((tm, d), lambda i, e: (i, 0)),
                  pl.BlockSpec((1, N_MOD, d), lambda i, e: (rows.seq(i), 0, 0)),
                  pl.BlockSpec((1, d), lambda i, e: (0, 0))],
        out_specs=[pl.BlockSpec((tm, d), lambda i, e: (rows.p_idx(i), 0)),
                   pl.BlockSpec((tm, d), lambda i, e: (rows.s_idx(i), 0))],
        out_shape=[jax.ShapeDtypeStruct((rows.tp, d), F32), jax.ShapeDtypeStruct((rows.ts, d), F32)],
        scratch_shapes=[pltpu.VMEM((tm, d), F32)],
        compiler_params=pltpu.CompilerParams(dimension_semantics=("arbitrary", "arbitrary"),
                                             vmem_limit_bytes=VMEM_LIMIT),
        name="moe",
    )(h2, gates, w_gate, w_up, w_down, x_mid, mod, g_post)


def _rotate_half_cols(w):
    half = MLA_ROPE_DIM // 2
    return jnp.concatenate([-w[..., half:], w[..., :half]], axis=-1)


def _rope_tables(s_max):
    half = MLA_ROPE_DIM // 2
    inv = ROPE_BASE ** (-jnp.arange(half, dtype=F32) / half)
    ang = jnp.arange(s_max, dtype=F32)[:, None] * inv[None, :]
    pad = jnp.zeros((s_max, LANES - MLA_ROPE_DIM), F32)
    cos, sin = jnp.cos(ang), jnp.sin(ang)
    return (jnp.concatenate([cos, cos, pad], axis=-1), jnp.concatenate([sin, sin, pad], axis=-1))


def _layer(x_prompt, x_sample, c_prompt, c_sample, layer_idx, w_ada, b_ada, g_pre_mix, g_post_mix, w_in,
           lambda_q1, lambda_k1, lambda_q2, lambda_k2, g_diff_sub, g_q_a, w_uq, g_kv_a, w_ukv, w_o,
           g_pre_ffn, g_post_ffn, w_router_group, b_router_group, w_router_expert, b_router_expert,
           w_gate, w_up, w_down, tm=256, tm_moe=512):
    bp, sp, d = x_prompt.shape
    bs, ss, _ = x_sample.shape
    rows = _Rows(bp, sp, bs, ss, tm)
    xp = x_prompt.reshape(bp * sp, d)
    xs = x_sample.reshape(bs * ss, d)
    lambda_init = 0.8 - 0.6 * math.exp(-0.3 * layer_idx)

    nb = bp + bs
    nb_pad = -(-nb // 8) * 8
    c_all = jnp.concatenate([c_prompt, c_sample, jnp.zeros((nb_pad - nb, d), F32)], axis=0)
    mod = _ada(c_all, w_ada, b_ada.reshape(1, -1)).reshape(nb_pad, N_MOD, d)

    w_in_aug = jnp.concatenate([w_in, _rotate_half_cols(w_in[:, C_KR:C_KR + MLA_ROPE_DIM])], axis=1).astype(BF16)
    wq3 = w_uq.reshape(Q_LORA_RANK, MLA_HEADS, MLA_NOPE_DIM + MLA_ROPE_DIM)
    wq_rope = wq3[..., MLA_NOPE_DIM:]
    w_q = jnp.concatenate([wq3[..., :MLA_NOPE_DIM], wq_rope, _rotate_half_cols(wq_rope)], axis=-1)
    w_q = w_q.reshape(Q_LORA_RANK, MLA_HEADS * MLA_QK_PAD).astype(BF16)
    w_kv = w_ukv.astype(BF16)
    cos_t, sin_t = _rope_tables(max(sp, ss))

    dq, dk, dv, qm, km, vm = _inproj(rows, xp, xs, mod, g_pre_mix.reshape(1, d), w_in_aug,
                                     g_q_a.reshape(1, -1), w_q, g_kv_a.reshape(1, -1), w_kv, cos_t, sin_t)

    slopes = jnp.array([2.0 ** (-8.0 * (i + 1) / DIFF_HEADS) for i in range(DIFF_HEADS)], dtype=F32)
    lam_p = jnp.zeros((8, LANES), F32).at[0:4, 0:DIFF_HEAD_DIM].set(
        jnp.stack([lambda_q1, lambda_k1, lambda_q2, lambda_k2]))
    g_sub = g_diff_sub.reshape(1, -1)
    oa_p = _diff_attn(dq, dk, dv, slopes, lam_p, g_sub, 0, bp, sp, lambda_init)
    oa_s = _diff_attn(dq, dk, dv, slopes, lam_p, g_sub, rows.tp, bs, ss, lambda_init)
    ob_p = _mla_attn(qm, km, vm, 0, bp, sp)
    ob_s = _mla_attn(qm, km, vm, rows.tp, bs, ss)

    w_r = jnp.concatenate([w_router_group, w_router_expert.reshape(d, N_EXPERTS),
                           jnp.zeros((d, LANES - N_GROUPS - N_EXPERTS), F32)], axis=1)
    b_r = jnp.concatenate([b_router_group, b_router_expert.reshape(N_EXPERTS),
                           jnp.zeros((LANES - N_GROUPS - N_EXPERTS,), F32)]).reshape(1, LANES)
    x_mid, h2, gates = _oproj(rows, oa_p, oa_s, ob_p, ob_s, xp, xs, mod, w_o.astype(BF16),
                              g_post_mix.reshape(1, d), g_pre_ffn.reshape(1, d), w_r, b_r)

    rows_moe = _Rows(bp, sp, bs, ss, tm_moe)
    yp, ys = _moe(rows_moe, h2, gates, w_gate.astype(BF16), w_up.astype(BF16), w_down.astype(BF16),
                  x_mid, mod, g_post_ffn.reshape(1, d))
    return yp.reshape(bp, sp, d), ys.reshape(bs, ss, d)


def kernel(x_prompt, x_sample, c_prompt, c_sample, w_ada, b_ada, g_pre_mix, g_post_mix, w_in, lambda_q1,
           lambda_k1, lambda_q2, lambda_k2, g_diff_sub, g_q_a, w_uq, g_kv_a, w_ukv, w_o, g_pre_ffn,
           g_post_ffn, w_router_group, b_router_group, w_router_expert, b_router_expert, w_gate, w_up,
           w_down):
    for l in range(w_ada.shape[0]):
        x_prompt, x_sample = _layer(
            x_prompt, x_sample, c_prompt, c_sample, l, w_ada[l], b_ada[l], g_pre_mix[l], g_post_mix[l],
            w_in[l], lambda_q1[l], lambda_k1[l], lambda_q2[l], lambda_k2[l], g_diff_sub[l], g_q_a[l],
            w_uq[l], g_kv_a[l], w_ukv[l], w_o[l], g_pre_ffn[l], g_post_ffn[l], w_router_group[l],
            b_router_group[l], w_router_expert[l], b_router_expert[l], w_gate[l], w_up[l], w_down[l])
    return x_prompt, x_sample
```

```python
import functools
import math

import jax
import jax.numpy as jnp
from jax import lax
from jax.experimental import pallas as pl
from jax.experimental.pallas import tpu as pltpu

F32 = jnp.float32
BF16 = jnp.bfloat16

D_MODEL = 2048
DIFF_HEADS = 8
DIFF_HEAD_DIM = 64
DIFF_WIDTH = DIFF_HEADS * 2 * DIFF_HEAD_DIM
MLA_HEADS = 8
MLA_NOPE_DIM = 128
MLA_ROPE_DIM = 64
MLA_V_DIM = 128
MLA_QK_PAD = 256
Q_LORA_RANK = 512
KV_LORA_RANK = 256
ROPE_BASE = 10000.0
D_MLA_OUT = MLA_HEADS * MLA_V_DIM
N_GROUPS = 4
EXPERTS_PER_GROUP = 4
N_EXPERTS = N_GROUPS * EXPERTS_PER_GROUP
D_EXPERT = 512
N_MOD = 6
NORM_EPS = 1e-6
LANES = 128
ROUTER_LANE0 = N_GROUPS
N_PAIRS = 6
N_BUCKETS = N_GROUPS * N_PAIRS
HX_WIDTH = D_MODEL + LANES
MOE_TM = 256

C_DQ, C_DK, C_DV = 0, DIFF_WIDTH, 2 * DIFF_WIDTH
C_CQ = 3 * DIFF_WIDTH
C_CKV = C_CQ + Q_LORA_RANK
C_KR = C_CKV + KV_LORA_RANK
D_IN_AUG = C_KR + 2 * MLA_ROPE_DIM

VMEM_LIMIT = 56 * 1024 * 1024
LOG2E = 1.4426950408889634
ATTN_TQ = 256
ATTN_TK = 512
ATTN_TQ_OUTER = 1024
ATTN_UNROLL = 4


def _rms(x, g):
    return x * lax.rsqrt(jnp.mean(x * x, axis=-1, keepdims=True) + NORM_EPS) * g


def _dot(a, b):
    return jnp.dot(a, b, preferred_element_type=F32)


def _dot_nt(a, b):
    return lax.dot_general(a, b, (((1,), (1,)), ((), ())), preferred_element_type=F32)


class _Rows:
    def __init__(self, bp, sp, bs, ss, tm):
        assert sp % tm == 0 and ss % tm == 0
        self.bp, self.sp, self.bs, self.ss, self.tm = bp, sp, bs, ss, tm
        self.n_p = bp * sp // tm
        self.n_s = bs * ss // tm
        self.n = self.n_p + self.n_s
        self.tp = bp * sp
        self.ts = bs * ss
        self.t = self.tp + self.ts

    def p_idx(self, i):
        return jnp.minimum(i, self.n_p - 1)

    def s_idx(self, i):
        return jnp.maximum(i - self.n_p, 0)

    def seq(self, i):
        return jnp.where(i < self.n_p, i // (self.sp // self.tm),
                         self.bp + (i - self.n_p) // (self.ss // self.tm))

    def pos(self, i):
        return jnp.where(i < self.n_p, i % (self.sp // self.tm),
                         (i - self.n_p) % (self.ss // self.tm))


def _ada_kernel(c_ref, w_ref, b_ref, o_ref):
    c = c_ref[...]
    a = c / (1.0 + jnp.exp(-c))
    o_ref[...] = jnp.dot(a, w_ref[...], preferred_element_type=F32,
                         precision=lax.Precision.HIGHEST) + b_ref[...]


def _ada(c_all, w_ada, b_ada):
    nb, d = c_all.shape
    n = w_ada.shape[1]
    tn = 1024
    return pl.pallas_call(
        _ada_kernel,
        grid=(n // tn,),
        in_specs=[pl.BlockSpec((nb, d), lambda j: (0, 0)),
                  pl.BlockSpec((d, tn), lambda j: (0, j)),
                  pl.BlockSpec((1, tn), lambda j: (0, j))],
        out_specs=pl.BlockSpec((nb, tn), lambda j: (0, j)),
        out_shape=jax.ShapeDtypeStruct((nb, n), F32),
        compiler_params=pltpu.CompilerParams(dimension_semantics=("arbitrary",),
                                             vmem_limit_bytes=VMEM_LIMIT),
        name="ada",
    )(c_all, w_ada, b_ada)


def _rope_half(x, cos, sin):
    return x * cos + pltpu.roll(x, MLA_ROPE_DIM, 1) * sin


def _inproj_kernel(n_p, xp_ref, xs_ref, mod_ref, gpre_ref, win_ref, gqa_ref, wq_ref, gkva_ref,
                   wkv_ref, cos_ref, sin_ref, dq_ref, dk_ref, dv_ref, qm_ref, km_ref, vm_ref):
    i = pl.program_id(0)
    x = jnp.where(i < n_p, xp_ref[...], xs_ref[...])
    shift = mod_ref[0, 0:1, :]
    scale = mod_ref[0, 1:2, :]
    hb = (_rms(x, gpre_ref[...]) * (1.0 + scale) + shift).astype(BF16)

    dq_ref[...] = (_dot(hb, win_ref[:, C_DQ:C_DQ + DIFF_WIDTH]) * (DIFF_HEAD_DIM ** -0.5 * LOG2E)).astype(BF16)
    dk_ref[...] = _dot(hb, win_ref[:, C_DK:C_DK + DIFF_WIDTH]).astype(BF16)
    dv_ref[...] = _dot(hb, win_ref[:, C_DV:C_DV + DIFF_WIDTH]).astype(BF16)

    cos = cos_ref[...]
    sin = sin_ref[...]
    cq = _dot(hb, win_ref[:, C_CQ:C_CQ + Q_LORA_RANK])
    ckv = _dot(hb, win_ref[:, C_CKV:C_CKV + KV_LORA_RANK])
    kr = _rope_half(_dot(hb, win_ref[:, C_KR:C_KR + 2 * MLA_ROPE_DIM]), cos, sin).astype(BF16)

    q = _dot(_rms(cq, gqa_ref[...]).astype(BF16), wq_ref[...])
    qscale = (MLA_NOPE_DIM + MLA_ROPE_DIM) ** -0.5 * LOG2E
    for h in range(MLA_HEADS):
        c0 = h * MLA_QK_PAD
        qm_ref[:, c0:c0 + LANES] = (q[:, c0:c0 + LANES] * qscale).astype(BF16)
        qm_ref[:, c0 + LANES:c0 + 2 * LANES] = (
            _rope_half(q[:, c0 + LANES:c0 + 2 * LANES], cos, sin) * qscale).astype(BF16)

    kv = _dot(_rms(ckv, gkva_ref[...]).astype(BF16), wkv_ref[...])
    for h in range(MLA_HEADS):
        c0 = h * MLA_QK_PAD
        km_ref[:, c0:c0 + LANES] = kv[:, c0:c0 + LANES].astype(BF16)
        km_ref[:, c0 + LANES:c0 + 2 * LANES] = kr
        vm_ref[:, h * MLA_V_DIM:(h + 1) * MLA_V_DIM] = kv[:, c0 + LANES:c0 + 2 * LANES].astype(BF16)


def _inproj(rows, xp, xs, mod, g_pre, w_in_aug, g_q_a, w_q, g_kv_a, w_kv, cos_t, sin_t):
    tm, d = rows.tm, D_MODEL
    const = lambda i: (0, 0)
    row = lambda i: (i, 0)
    one = pl.Buffered(1)
    out_w = (DIFF_WIDTH, DIFF_WIDTH, DIFF_WIDTH, MLA_HEADS * MLA_QK_PAD, MLA_HEADS * MLA_QK_PAD, D_MLA_OUT)
    return pl.pallas_call(
        functools.partial(_inproj_kernel, rows.n_p),
        grid=(rows.n,),
        in_specs=[pl.BlockSpec((tm, d), lambda i: (rows.p_idx(i), 0)),
                  pl.BlockSpec((tm, d), lambda i: (rows.s_idx(i), 0)),
                  pl.BlockSpec((1, N_MOD, d), lambda i: (rows.seq(i), 0, 0)),
                  pl.BlockSpec((1, d), const),
                  pl.BlockSpec((d, D_IN_AUG), const, pipeline_mode=one),
                  pl.BlockSpec((1, Q_LORA_RANK), const),
                  pl.BlockSpec((Q_LORA_RANK, MLA_HEADS * MLA_QK_PAD), const, pipeline_mode=one),
                  pl.BlockSpec((1, KV_LORA_RANK), const),
                  pl.BlockSpec((KV_LORA_RANK, MLA_HEADS * MLA_QK_PAD), const, pipeline_mode=one),
                  pl.BlockSpec((tm, LANES), lambda i: (rows.pos(i), 0)),
                  pl.BlockSpec((tm, LANES), lambda i: (rows.pos(i), 0))],
        out_specs=[pl.BlockSpec((tm, w), row) for w in out_w],
        out_shape=[jax.ShapeDtypeStruct((rows.t, w), BF16) for w in out_w],
        compiler_params=pltpu.CompilerParams(dimension_semantics=("arbitrary",),
                                             vmem_limit_bytes=VMEM_LIMIT),
        name="inproj",
    )(xp, xs, mod, g_pre, w_in_aug, g_q_a, w_q, g_kv_a, w_kv, cos_t, sin_t)


def _lane_groups(x, op):
    out = x[:, 0:LANES]
    for g in range(1, x.shape[1] // LANES):
        out = op(out, x[:, g * LANES:(g + 1) * LANES])
    return out


def _chunk(ref, j, tk):
    return ref[pl.ds(pl.multiple_of(j * tk, tk), tk), :]


def _two_pass(score_fn, shift_fn, v_ref, s_buf, nk, tk, tq):
    unroll = min(ATTN_UNROLL, nk)

    def pass_a(j, mpart):
        s = score_fn(j)
        s_buf[j] = s
        part = _lane_groups(s, jnp.maximum)
        if shift_fn is not None:
            part = part + shift_fn(j)
        return jnp.maximum(mpart, part)

    mpart = lax.fori_loop(0, nk, pass_a, jnp.full((tq, LANES), -jnp.inf, F32), unroll=unroll)
    m = jnp.max(mpart, axis=-1, keepdims=True)

    def pass_b(j, carry):
        lpart, acc = carry
        mj = m if shift_fn is None else m - shift_fn(j)
        p = jnp.exp2(s_buf[j] - mj)
        return lpart + _lane_groups(p, jnp.add), acc + _dot(p.astype(BF16), _chunk(v_ref, j, tk))

    zeros = jnp.zeros((tq, LANES), F32)
    lpart, acc = lax.fori_loop(0, nk, pass_b, (zeros, zeros), unroll=unroll)
    return acc / jnp.sum(lpart, axis=-1, keepdims=True)


def _diff_attn_kernel(lambda_init, tq, tk, nk, n_sub, slopes_ref, lam_ref, gsub_ref, q_ref, k_ref, v_ref,
                      o_ref, s_buf, bias_buf):
    h = pl.program_id(1)
    qo = pl.program_id(2)
    slope = slopes_ref[h] * LOG2E
    n_cross = tk // tq

    r_minus_c = (lax.broadcasted_iota(jnp.int32, (tq, tk), 0)
                 - lax.broadcasted_iota(jnp.int32, (tq, tk), 1)).astype(F32)
    bias_buf[0] = r_minus_c * (-slope)
    bias_buf[1] = r_minus_c * slope
    for c in range(n_cross):
        bias_buf[2 + c] = jnp.abs(r_minus_c + float(c * tq)) * (-slope)

    lp = lam_ref[...]
    lam = (jnp.exp(jnp.sum(lp[0:1] * lp[1:2], axis=-1, keepdims=True))
           - jnp.exp(jnp.sum(lp[2:3] * lp[3:4], axis=-1, keepdims=True)) + lambda_init)
    gsub = gsub_ref[...]

    def sub_tile(sub, carry):
        row = pl.multiple_of(sub * tq, tq)
        qbase = (qo * n_sub + sub) * tq
        jc = qbase // tk
        cross = 2 + (qbase - jc * tk) // tq
        q = q_ref[pl.ds(row, tq), :]
        lane = lax.broadcasted_iota(jnp.int32, q.shape, 1)
        shift = lambda j: jnp.where(j == jc, 0.0, -slope * jnp.abs(qbase - j * tk).astype(F32))
        which = lambda j: jnp.where(j < jc, 0, jnp.where(j > jc, 1, cross))
        outs = []
        for half in range(2):
            keep = (lane < DIFF_HEAD_DIM) if half == 0 else (lane >= DIFF_HEAD_DIM)
            qh = jnp.where(keep, q, jnp.zeros_like(q))
            score = lambda j: _dot_nt(qh, _chunk(k_ref, j, tk)) + bias_buf[which(j)]
            outs.append(_two_pass(score, shift, v_ref, s_buf, nk, tk, tq))
        o = outs[0] - lam * outs[1]
        o_ref[pl.ds(row, tq), :] = (_rms(o, gsub) * (1.0 - lambda_init)).astype(BF16)
        return carry

    lax.fori_loop(0, n_sub, sub_tile, 0)


def _attn_tiles(s):
    tq = min(ATTN_TQ, s)
    tk = min(ATTN_TK, s)
    tq_outer = min(ATTN_TQ_OUTER, s)
    assert s % tq_outer == 0 and tq_outer % tq == 0 and s % tk == 0 and tk % tq == 0
    return tq, tk, tq_outer


def _diff_attn(dq, dk, dv, slopes, lam_p, g_sub, row0, b, s, lambda_init):
    tq, tk, tq_outer = _attn_tiles(s)
    nk, n_sub, nqo = s // tk, tq_outer // tq, s // tq_outer
    dh = 2 * DIFF_HEAD_DIM
    return pl.pallas_call(
        functools.partial(_diff_attn_kernel, lambda_init, tq, tk, nk, n_sub),
        grid=(b, DIFF_HEADS, nqo),
        in_specs=[pl.BlockSpec(memory_space=pltpu.SMEM),
                  pl.BlockSpec((8, LANES), lambda bi, h, qi: (0, 0)),
                  pl.BlockSpec((1, dh), lambda bi, h, qi: (0, 0)),
                  pl.BlockSpec((tq_outer, dh), lambda bi, h, qi: (row0 // tq_outer + bi * nqo + qi, h)),
                  pl.BlockSpec((s, dh), lambda bi, h, qi: (row0 // s + bi, h)),
                  pl.BlockSpec((s, dh), lambda bi, h, qi: (row0 // s + bi, h))],
        out_specs=pl.BlockSpec((tq_outer, dh), lambda bi, h, qi: (bi * nqo + qi, h)),
        out_shape=jax.ShapeDtypeStruct((b * s, DIFF_WIDTH), BF16),
        scratch_shapes=[pltpu.VMEM((nk, tq, tk), F32), pltpu.VMEM((2 + tk // tq, tq, tk), F32)],
        compiler_params=pltpu.CompilerParams(
            dimension_semantics=("arbitrary", "arbitrary", "arbitrary"), vmem_limit_bytes=VMEM_LIMIT),
        name="diff_attn",
    )(slopes, lam_p, g_sub, dq, dk, dv)


def _mla_attn_kernel(tq, tk, nk, n_sub, q_ref, k_ref, v_ref, o_ref, s_buf):
    def sub_tile(sub, carry):
        row = pl.multiple_of(sub * tq, tq)
        q = q_ref[pl.ds(row, tq), :]
        score = lambda j: _dot_nt(q, _chunk(k_ref, j, tk))
        o_ref[pl.ds(row, tq), :] = _two_pass(score, None, v_ref, s_buf, nk, tk, tq).astype(BF16)
        return carry

    lax.fori_loop(0, n_sub, sub_tile, 0)


def _mla_attn(qm, km, vm, row0, b, s):
    tq, tk, tq_outer = _attn_tiles(s)
    nk, n_sub, nqo = s // tk, tq_outer // tq, s // tq_outer
    return pl.pallas_call(
        functools.partial(_mla_attn_kernel, tq, tk, nk, n_sub),
        grid=(b, MLA_HEADS, nqo),
        in_specs=[pl.BlockSpec((tq_outer, MLA_QK_PAD), lambda bi, h, qi: (row0 // tq_outer + bi * nqo + qi, h)),
                  pl.BlockSpec((s, MLA_QK_PAD), lambda bi, h, qi: (row0 // s + bi, h)),
                  pl.BlockSpec((s, MLA_V_DIM), lambda bi, h, qi: (row0 // s + bi, h))],
        out_specs=pl.BlockSpec((tq_outer, MLA_V_DIM), lambda bi, h, qi: (bi * nqo + qi, h)),
        out_shape=jax.ShapeDtypeStruct((b * s, D_MLA_OUT), BF16),
        scratch_shapes=[pltpu.VMEM((nk, tq, tk), F32)],
        compiler_params=pltpu.CompilerParams(
            dimension_semantics=("arbitrary", "arbitrary", "arbitrary"), vmem_limit_bytes=VMEM_LIMIT),
        name="mla_attn",
    )(qm, km, vm)


def _route(logits):
    lane = lax.broadcasted_iota(jnp.int32, logits.shape, 1)
    neg = jnp.full(logits.shape, -jnp.inf, F32)
    big = jnp.full(logits.shape, LANES, jnp.int32)
    first = lambda mask: jnp.min(jnp.where(mask, lane, big), axis=-1, keepdims=True)

    gl = jnp.where(lane < N_GROUPS, logits, neg)
    gmax = jnp.max(gl, axis=-1, keepdims=True)
    g_idx = first(gl == gmax)
    g_w = 1.0 / jnp.sum(jnp.exp(gl - gmax), axis=-1, keepdims=True)

    lo = ROUTER_LANE0 + EXPERTS_PER_GROUP * g_idx
    el = jnp.where(lane >= lo, jnp.where(lane < lo + EXPERTS_PER_GROUP, logits, neg), neg)
    v1 = jnp.max(el, axis=-1, keepdims=True)
    i1 = first(el == v1)
    el2 = jnp.where(lane == i1, neg, el)
    v2 = jnp.max(el2, axis=-1, keepdims=True)
    i2 = first(el2 == v2)
    t = jnp.exp(v2 - v1)
    w1 = g_w / (1.0 + t)
    w2 = w1 * t
    first_low = i1 < i2
    ea = jnp.minimum(i1, i2) - lo
    eb = jnp.maximum(i1, i2) - lo
    pair = jnp.where(ea == 0, 0, jnp.where(ea == 1, 3, 5)) + eb - ea - 1
    bucket = g_idx * N_PAIRS + pair
    return bucket, jnp.where(first_low, w1, w2), jnp.where(first_low, w2, w1)


def _oproj_kernel(n_p, oap_ref, oas_ref, obp_ref, obs_ref, xp_ref, xs_ref, mod_ref, wo_ref, gpost_ref,
                  gpre_ref, wr_ref, br_ref, xmid_ref, hx_ref, meta_ref, counts_ref, cnt):
    i = pl.program_id(0)

    @pl.when(i == 0)
    def _():
        cnt[...] = jnp.zeros(cnt.shape, F32)

    is_p = i < n_p
    oa = jnp.where(is_p, oap_ref[...], oas_ref[...])
    ob = jnp.where(is_p, obp_ref[...], obs_ref[...])
    x = jnp.where(is_p, xp_ref[...], xs_ref[...])
    mix = _dot(oa, wo_ref[0:DIFF_WIDTH, :]) + _dot(ob, wo_ref[DIFF_WIDTH:DIFF_WIDTH + D_MLA_OUT, :])
    x_mid = x + mod_ref[0, 2:3, :] * _rms(mix, gpost_ref[...])
    xmid_ref[...] = x_mid
    h2 = _rms(x_mid, gpre_ref[...]) * (1.0 + mod_ref[0, 4:5, :]) + mod_ref[0, 3:4, :]
    h_hi = h2.astype(BF16)
    h_lo = (h2 - h_hi.astype(F32)).astype(BF16)
    w_r = wr_ref[...]
    w_hi = w_r.astype(BF16)
    w_lo = (w_r - w_hi.astype(F32)).astype(BF16)
    logits = _dot(h_hi, w_hi) + _dot(h_hi, w_lo) + _dot(h_lo, w_hi) + br_ref[...]
    bucket, wa, wb = _route(logits)

    lane = lax.broadcasted_iota(jnp.int32, logits.shape, 1)
    zero = jnp.zeros_like(logits)
    hx_ref[:, 0:D_MODEL] = h2
    hx_ref[:, D_MODEL:HX_WIDTH] = jnp.where(lane == 0, wa, jnp.where(lane == 1, wb, zero))

    tm = logits.shape[0]
    onehot = jnp.where(lane == bucket, 1.0, 0.0)
    earlier = (lax.broadcasted_iota(jnp.int32, (tm, tm), 0) > lax.broadcasted_iota(jnp.int32, (tm, tm), 1))
    before = _dot(jnp.where(earlier, 1.0, 0.0).astype(BF16), onehot.astype(BF16))
    rank = jnp.sum(onehot * (before + cnt[0:1, :]), axis=-1, keepdims=True)
    cnt[...] = cnt[...] + jnp.sum(onehot, axis=0, keepdims=True)
    counts_ref[...] = cnt[...]
    meta_ref[...] = jnp.where(lane == 0, bucket, jnp.where(lane == 1, rank.astype(jnp.int32), 0))


def _oproj(rows, oa_p, oa_s, ob_p, ob_s, xp, xs, mod, w_o, g_post, g_pre, w_r, b_r):
    tm, d = rows.tm, D_MODEL
    const = lambda i: (0, 0)
    row = lambda i: (i, 0)
    prow = lambda i: (rows.p_idx(i), 0)
    srow = lambda i: (rows.s_idx(i), 0)
    return pl.pallas_call(
        functools.partial(_oproj_kernel, rows.n_p),
        grid=(rows.n,),
        in_specs=[pl.BlockSpec((tm, DIFF_WIDTH), prow), pl.BlockSpec((tm, DIFF_WIDTH), srow),
                  pl.BlockSpec((tm, D_MLA_OUT), prow), pl.BlockSpec((tm, D_MLA_OUT), srow),
                  pl.BlockSpec((tm, d), prow), pl.BlockSpec((tm, d), srow),
                  pl.BlockSpec((1, N_MOD, d), lambda i: (rows.seq(i), 0, 0)),
                  pl.BlockSpec((DIFF_WIDTH + D_MLA_OUT, d), const, pipeline_mode=pl.Buffered(1)),
                  pl.BlockSpec((1, d), const), pl.BlockSpec((1, d), const),
                  pl.BlockSpec((d, LANES), const), pl.BlockSpec((1, LANES), const)],
        out_specs=[pl.BlockSpec((tm, d), row), pl.BlockSpec((tm, HX_WIDTH), row),
                   pl.BlockSpec((tm, LANES), row), pl.BlockSpec((8, LANES), const)],
        out_shape=[jax.ShapeDtypeStruct((rows.t, d), F32), jax.ShapeDtypeStruct((rows.t, HX_WIDTH), F32),
                   jax.ShapeDtypeStruct((rows.t, LANES), jnp.int32), jax.ShapeDtypeStruct((8, LANES), F32)],
        scratch_shapes=[pltpu.VMEM((8, LANES), F32)],
        compiler_params=pltpu.CompilerParams(dimension_semantics=("arbitrary",),
                                             vmem_limit_bytes=VMEM_LIMIT),
        name="oproj",
    )(oa_p, oa_s, ob_p, ob_s, xp, xs, mod, w_o, g_post, g_pre, w_r, b_r)


def _gather_rows(idx_ref, src_hbm, dst, sem):
    def body(r, carry):
        pltpu.make_async_copy(src_hbm.at[pl.ds(idx_ref[0, 0, r], 1), :], dst.at[pl.ds(r, 1), :], sem).start()
        return carry

    lax.fori_loop(0, dst.shape[0], body, 0, unroll=8)


def _wait_rows(src_hbm, dst, sem):
    pltpu.make_async_copy(src_hbm.at[pl.ds(0, dst.shape[0]), :], dst, sem).wait()


def _experts_kernel(ea_ref, eb_ref, nv_ref, inv_cur, inv_nxt, hx_hbm, wga, wua, wda, wgb, wub, wdb,
                    f_ref, xbuf, sem):
    i = pl.program_id(0)
    n_valid = nv_ref[0]
    slot = i % 2

    @pl.when(i == 0)
    def _():
        _gather_rows(inv_cur, hx_hbm, xbuf.at[0], sem.at[0])

    @pl.when(i + 1 < n_valid)
    def _():
        _gather_rows(inv_nxt, hx_hbm, xbuf.at[1 - slot], sem.at[1 - slot])

    @pl.when(i < n_valid)
    def _():
        _wait_rows(hx_hbm, xbuf.at[slot], sem.at[slot])
        x = xbuf[slot, :, 0:D_MODEL].astype(BF16)
        aux = xbuf[slot, :, D_MODEL:HX_WIDTH]

        def hidden(wg, wu, w):
            g = _dot(x, wg[0])
            return ((g / (1.0 + jnp.exp(-g))) * _dot(x, wu[0]) * w).astype(BF16)

        f_ref[...] = (_dot(hidden(wga, wua, aux[:, 0:1]), wda[0])
                      + _dot(hidden(wgb, wub, aux[:, 1:2]), wdb[0]))

    @pl.when(i >= n_valid)
    def _():
        f_ref[...] = jnp.zeros(f_ref.shape, F32)


def _experts(hx, inv3, tile_ea, tile_eb, n_valid, w_gate, w_up, w_down):
    n_tiles, _, tm = inv3.shape
    d = D_MODEL
    wspec = lambda shape, which: pl.BlockSpec(
        (1,) + shape, (lambda i, ea, eb, nv: (ea[i], 0, 0)) if which == 0 else (lambda i, ea, eb, nv: (eb[i], 0, 0)))
    smem = lambda imap: pl.BlockSpec((1, 1, tm), imap, memory_space=pltpu.SMEM)
    return pl.pallas_call(
        _experts_kernel,
        grid_spec=pltpu.PrefetchScalarGridSpec(
            num_scalar_prefetch=3,
            grid=(n_tiles,),
            in_specs=[smem(lambda i, ea, eb, nv: (i, 0, 0)),
                      smem(lambda i, ea, eb, nv: (jnp.minimum(i + 1, n_tiles - 1), 0, 0)),
                      pl.BlockSpec(memory_space=pl.ANY),
                      wspec((d, D_EXPERT), 0), wspec((d, D_EXPERT), 0), wspec((D_EXPERT, d), 0),
                      wspec((d, D_EXPERT), 1), wspec((d, D_EXPERT), 1), wspec((D_EXPERT, d), 1)],
            out_specs=pl.BlockSpec((tm, d), lambda i, ea, eb, nv: (i, 0)),
            scratch_shapes=[pltpu.VMEM((2, tm, HX_WIDTH), F32), pltpu.SemaphoreType.DMA((2,))]),
        out_shape=jax.ShapeDtypeStruct((n_tiles * tm, d), F32),
        compiler_params=pltpu.CompilerParams(dimension_semantics=("arbitrary",),
                                             vmem_limit_bytes=VMEM_LIMIT),
        name="experts",
    )(tile_ea, tile_eb, n_valid, inv3, inv3, hx, w_gate, w_up, w_down, w_gate, w_up, w_down)


def _final_kernel(n_p, pos_cur, pos_nxt, f_hbm, xmid_ref, mod_ref, gpost_ref, yp_ref, ys_ref, fbuf, sem):
    i = pl.program_id(0)
    slot = i % 2

    @pl.when(i == 0)
    def _():
        _gather_rows(pos_cur, f_hbm, fbuf.at[0], sem.at[0])

    @pl.when(i + 1 < pl.num_programs(0))
    def _():
        _gather_rows(pos_nxt, f_hbm, fbuf.at[1 - slot], sem.at[1 - slot])

    _wait_rows(f_hbm, fbuf.at[slot], sem.at[slot])
    y = xmid_ref[...] + mod_ref[0, 5:6, :] * _rms(fbuf[slot], gpost_ref[...])

    @pl.when(i < n_p)
    def _():
        yp_ref[...] = y

    @pl.when(i >= n_p)
    def _():
        ys_ref[...] = y


def _final(rows, pos3, f_sorted, x_mid, mod, g_post):
    tm, d = rows.tm, D_MODEL
    smem = lambda imap: pl.BlockSpec((1, 1, tm), imap, memory_space=pltpu.SMEM)
    return pl.pallas_call(
        functools.partial(_final_kernel, rows.n_p),
        grid=(rows.n,),
        in_specs=[smem(lambda i: (i, 0, 0)),
                  smem(lambda i: (jnp.minimum(i + 1, rows.n - 1), 0, 0)),
                  pl.BlockSpec(memory_space=pl.ANY),
                  pl.BlockSpec((tm, d), lambda i: (i, 0)),
                  pl.BlockSpec((1, N_MOD, d), lambda i: (rows.seq(i), 0, 0)),
                  pl.BlockSpec((1, d), lambda i: (0, 0))],
        out_specs=[pl.BlockSpec((tm, d), lambda i: (rows.p_idx(i), 0)),
                   pl.BlockSpec((tm, d), lambda i: (rows.s_idx(i), 0))],
        out_shape=[jax.ShapeDtypeStruct((rows.tp, d), F32), jax.ShapeDtypeStruct((rows.ts, d), F32)],
        scratch_shapes=[pltpu.VMEM((2, tm, d), F32), pltpu.SemaphoreType.DMA((2,))],
        compiler_params=pltpu.CompilerParams(dimension_semantics=("arbitrary",),
                                             vmem_limit_bytes=VMEM_LIMIT),
        name="final",
    )(pos3, pos3, f_sorted, x_mid, mod, g_post)


def _moe_plan(meta, counts, tm):
    t = meta.shape[0]
    n_tiles = t // tm + N_BUCKETS
    cnt = counts[0, :N_BUCKETS].astype(jnp.int32)
    padded = (cnt + tm - 1) // tm * tm
    ends = jnp.cumsum(padded)
    pos = (ends - padded)[meta[:, 0]] + meta[:, 1]
    inv = jnp.zeros((n_tiles * tm,), jnp.int32).at[pos].set(jnp.arange(t, dtype=jnp.int32))
    n_valid = ends[-1] // tm
    tile = jnp.minimum(jnp.arange(n_tiles, dtype=jnp.int32), n_valid - 1)
    bucket = jnp.searchsorted(ends, tile * tm, side="right").astype(jnp.int32)
    group, pair = bucket // N_PAIRS, bucket % N_PAIRS
    ea = jnp.array([0, 0, 0, 1, 1, 2], jnp.int32)[pair]
    eb = jnp.array([1, 2, 3, 2, 3, 3], jnp.int32)[pair]
    base = group * EXPERTS_PER_GROUP
    return pos, inv, base + ea, base + eb, n_valid.reshape(1).astype(jnp.int32)


def _rotate_half_cols(w):
    half = MLA_ROPE_DIM // 2
    return jnp.concatenate([-w[..., half:], w[..., :half]], axis=-1)


def _rope_tables(s_max):
    half = MLA_ROPE_DIM // 2
    inv = ROPE_BASE ** (-jnp.arange(half, dtype=F32) / half)
    ang = jnp.arange(s_max, dtype=F32)[:, None] * inv[None, :]
    pad = jnp.zeros((s_max, LANES - MLA_ROPE_DIM), F32)
    cos, sin = jnp.cos(ang), jnp.sin(ang)
    return (jnp.concatenate([cos, cos, pad], axis=-1), jnp.concatenate([sin, sin, pad], axis=-1))


def _layer(x_prompt, x_sample, c_prompt, c_sample, layer_idx, w_ada, b_ada, g_pre_mix, g_post_mix, w_in,
           lambda_q1, lambda_k1, lambda_q2, lambda_k2, g_diff_sub, g_q_a, w_uq, g_kv_a, w_ukv, w_o,
           g_pre_ffn, g_post_ffn, w_router_group, b_router_group, w_router_expert, b_router_expert,
           w_gate, w_up, w_down, tm=256):
    bp, sp, d = x_prompt.shape
    bs, ss, _ = x_sample.shape
    rows = _Rows(bp, sp, bs, ss, tm)
    xp = x_prompt.reshape(bp * sp, d)
    xs = x_sample.reshape(bs * ss, d)
    lambda_init = 0.8 - 0.6 * math.exp(-0.3 * layer_idx)

    nb = bp + bs
    nb_pad = -(-nb // 8) * 8
    c_all = jnp.concatenate([c_prompt, c_sample, jnp.zeros((nb_pad - nb, d), F32)], axis=0)
    mod = _ada(c_all, w_ada, b_ada.reshape(1, -1)).reshape(nb_pad, N_MOD, d)

    w_in_aug = jnp.concatenate([w_in, _rotate_half_cols(w_in[:, C_KR:C_KR + MLA_ROPE_DIM])], axis=1).astype(BF16)
    wq3 = w_uq.reshape(Q_LORA_RANK, MLA_HEADS, MLA_NOPE_DIM + MLA_ROPE_DIM)
    wq_rope = wq3[..., MLA_NOPE_DIM:]
    w_q = jnp.concatenate([wq3[..., :MLA_NOPE_DIM], wq_rope, _rotate_half_cols(wq_rope)], axis=-1)
    w_q = w_q.reshape(Q_LORA_RANK, MLA_HEADS * MLA_QK_PAD).astype(BF16)
    w_kv = w_ukv.astype(BF16)
    cos_t, sin_t = _rope_tables(max(sp, ss))

    dq, dk, dv, qm, km, vm = _inproj(rows, xp, xs, mod, g_pre_mix.reshape(1, d), w_in_aug,
                                     g_q_a.reshape(1, -1), w_q, g_kv_a.reshape(1, -1), w_kv, cos_t, sin_t)

    slopes = jnp.array([2.0 ** (-8.0 * (i + 1) / DIFF_HEADS) for i in range(DIFF_HEADS)], dtype=F32)
    lam_p = jnp.zeros((8, LANES), F32).at[0:4, 0:DIFF_HEAD_DIM].set(
        jnp.stack([lambda_q1, lambda_k1, lambda_q2, lambda_k2]))
    g_sub = g_diff_sub.reshape(1, -1)
    oa_p = _diff_attn(dq, dk, dv, slopes, lam_p, g_sub, 0, bp, sp, lambda_init)
    oa_s = _diff_attn(dq, dk, dv, slopes, lam_p, g_sub, rows.tp, bs, ss, lambda_init)
    ob_p = _mla_attn(qm, km, vm, 0, bp, sp)
    ob_s = _mla_attn(qm, km, vm, rows.tp, bs, ss)

    w_r = jnp.concatenate([w_router_group, w_router_expert.reshape(d, N_EXPERTS),
                           jnp.zeros((d, LANES - N_GROUPS - N_EXPERTS), F32)], axis=1)
    b_r = jnp.concatenate([b_router_group, b_router_expert.reshape(N_EXPERTS),
                           jnp.zeros((LANES - N_GROUPS - N_EXPERTS,), F32)]).reshape(1, LANES)
    x_mid, hx, meta, counts = _oproj(rows, oa_p, oa_s, ob_p, ob_s, xp, xs, mod, w_o.astype(BF16),
                                     g_post_mix.reshape(1, d), g_pre_ffn.reshape(1, d), w_r, b_r)

    pos, inv, tile_ea, tile_eb, n_valid = _moe_plan(meta, counts, MOE_TM)
    f_sorted = _experts(hx, inv.reshape(-1, 1, MOE_TM), tile_ea, tile_eb, n_valid,
                        w_gate.astype(BF16), w_up.astype(BF16), w_down.astype(BF16))
    yp, ys = _final(rows, pos.reshape(rows.n, 1, tm), f_sorted, x_mid, mod, g_post_ffn.reshape(1, d))
    return yp.reshape(bp, sp, d), ys.reshape(bs, ss, d)


def kernel(x_prompt, x_sample, c_prompt, c_sample, w_ada, b_ada, g_pre_mix, g_post_mix, w_in, lambda_q1,
           lambda_k1, lambda_q2, lambda_k2, g_diff_sub, g_q_a, w_uq, g_kv_a, w_ukv, w_o, g_pre_ffn,
           g_post_ffn, w_router_group, b_router_group, w_router_expert, b_router_expert, w_gate, w_up,
           w_down):
    for l in range(w_ada.shape[0]):
        x_prompt, x_sample = _layer(
            x_prompt, x_sample, c_prompt, c_sample, l, w_ada[l], b_ada[l], g_pre_mix[l], g_post_mix[l],
            w_in[l], lambda_q1[l], lambda_k1[l], lambda_q2[l], lambda_k2[l], g_diff_sub[l], g_q_a[l],
            w_uq[l], g_kv_a[l], w_ukv[l], w_o[l], g_pre_ffn[l], g_post_ffn[l], w_router_group[l],
            b_router_group[l], w_router_expert[l], b_router_expert[l], w_gate[l], w_up[l], w_down[l])
    return x_prompt, x_sample
```

```python
import functools
import math

import jax
import jax.numpy as jnp
from jax import lax
from jax.experimental import pallas as pl
from jax.experimental.pallas import tpu as pltpu

F32 = jnp.float32
BF16 = jnp.bfloat16

D_MODEL = 2048
DIFF_HEADS = 8
DIFF_HEAD_DIM = 64
DIFF_WIDTH = DIFF_HEADS * 2 * DIFF_HEAD_DIM
MLA_HEADS = 8
MLA_NOPE_DIM = 128
MLA_ROPE_DIM = 64
MLA_V_DIM = 128
MLA_QK_PAD = 256
Q_LORA_RANK = 512
KV_LORA_RANK = 256
ROPE_BASE = 10000.0
D_MLA_OUT = MLA_HEADS * MLA_V_DIM
N_GROUPS = 4
EXPERTS_PER_GROUP = 4
N_EXPERTS = N_GROUPS * EXPERTS_PER_GROUP
D_EXPERT = 512
N_MOD = 6
NORM_EPS = 1e-6
LANES = 128
ROUTER_LANE0 = N_GROUPS
N_PAIRS = 6
N_BUCKETS = N_GROUPS * N_PAIRS
HX_WIDTH = D_MODEL + LANES
MOE_TM = 256

C_DQ, C_DK, C_DV = 0, DIFF_WIDTH, 2 * DIFF_WIDTH
C_CQ = 3 * DIFF_WIDTH
C_CKV = C_CQ + Q_LORA_RANK
C_KR = C_CKV + KV_LORA_RANK
D_IN_AUG = C_KR + 2 * MLA_ROPE_DIM

VMEM_LIMIT = 56 * 1024 * 1024
LOG2E = 1.4426950408889634
ATTN_TQ = 256
ATTN_TK = 512
ATTN_TQ_OUTER = 1024
ATTN_TKB = 2048
ATTN_UNROLL = 2


def _rms(x, g):
    return x * lax.rsqrt(jnp.mean(x * x, axis=-1, keepdims=True) + NORM_EPS) * g


def _dot(a, b):
    return jnp.dot(a, b, preferred_element_type=F32)


def _dot_nt(a, b):
    return lax.dot_general(a, b, (((1,), (1,)), ((), ())), preferred_element_type=F32)


class _Rows:
    def __init__(self, bp, sp, bs, ss, tm):
        assert sp % tm == 0 and ss % tm == 0
        self.bp, self.sp, self.bs, self.ss, self.tm = bp, sp, bs, ss, tm
        self.n_p = bp * sp // tm
        self.n_s = bs * ss // tm
        self.n = self.n_p + self.n_s
        self.tp = bp * sp
        self.ts = bs * ss
        self.t = self.tp + self.ts

    def p_idx(self, i):
        return jnp.minimum(i, self.n_p - 1)

    def s_idx(self, i):
        return jnp.maximum(i - self.n_p, 0)

    def seq(self, i):
        return jnp.where(i < self.n_p, i // (self.sp // self.tm),
                         self.bp + (i - self.n_p) // (self.ss // self.tm))

    def pos(self, i):
        return jnp.where(i < self.n_p, i % (self.sp // self.tm),
                         (i - self.n_p) % (self.ss // self.tm))


def _ada_kernel(c_ref, w_ref, b_ref, o_ref):
    c = c_ref[...]
    a = c / (1.0 + jnp.exp(-c))
    o_ref[...] = jnp.dot(a, w_ref[...], preferred_element_type=F32,
                         precision=lax.Precision.HIGHEST) + b_ref[...]


def _ada(c_all, w_ada, b_ada):
    nb, d = c_all.shape
    n = w_ada.shape[1]
    tn = 1024
    return pl.pallas_call(
        _ada_kernel,
        grid=(n // tn,),
        in_specs=[pl.BlockSpec((nb, d), lambda j: (0, 0)),
                  pl.BlockSpec((d, tn), lambda j: (0, j)),
                  pl.BlockSpec((1, tn), lambda j: (0, j))],
        out_specs=pl.BlockSpec((nb, tn), lambda j: (0, j)),
        out_shape=jax.ShapeDtypeStruct((nb, n), F32),
        compiler_params=pltpu.CompilerParams(dimension_semantics=("arbitrary",),
                                             vmem_limit_bytes=VMEM_LIMIT),
        name="ada",
    )(c_all, w_ada, b_ada)


def _rope_half(x, cos, sin):
    return x * cos + pltpu.roll(x, MLA_ROPE_DIM, 1) * sin


def _inproj_kernel(n_p, xp_ref, xs_ref, mod_ref, gpre_ref, win_ref, gqa_ref, wq_ref, gkva_ref,
                   wkv_ref, cos_ref, sin_ref, dq_ref, dk_ref, dv_ref, qm_ref, km_ref, vm_ref):
    i = pl.program_id(0)
    x = jnp.where(i < n_p, xp_ref[...], xs_ref[...])
    shift = mod_ref[0, 0:1, :]
    scale = mod_ref[0, 1:2, :]
    hb = (_rms(x, gpre_ref[...]) * (1.0 + scale) + shift).astype(BF16)

    dq_ref[...] = (_dot(hb, win_ref[:, C_DQ:C_DQ + DIFF_WIDTH]) * (DIFF_HEAD_DIM ** -0.5 * LOG2E)).astype(BF16)
    dk_ref[...] = _dot(hb, win_ref[:, C_DK:C_DK + DIFF_WIDTH]).astype(BF16)
    dv_ref[...] = _dot(hb, win_ref[:, C_DV:C_DV + DIFF_WIDTH]).astype(BF16)

    cos = cos_ref[...]
    sin = sin_ref[...]
    cq = _dot(hb, win_ref[:, C_CQ:C_CQ + Q_LORA_RANK])
    ckv = _dot(hb, win_ref[:, C_CKV:C_CKV + KV_LORA_RANK])
    kr = _rope_half(_dot(hb, win_ref[:, C_KR:C_KR + 2 * MLA_ROPE_DIM]), cos, sin).astype(BF16)

    q = _dot(_rms(cq, gqa_ref[...]).astype(BF16), wq_ref[...])
    qscale = (MLA_NOPE_DIM + MLA_ROPE_DIM) ** -0.5 * LOG2E
    for h in range(MLA_HEADS):
        c0 = h * MLA_QK_PAD
        qm_ref[:, c0:c0 + LANES] = (q[:, c0:c0 + LANES] * qscale).astype(BF16)
        qm_ref[:, c0 + LANES:c0 + 2 * LANES] = (
            _rope_half(q[:, c0 + LANES:c0 + 2 * LANES], cos, sin) * qscale).astype(BF16)

    kv = _dot(_rms(ckv, gkva_ref[...]).astype(BF16), wkv_ref[...])
    for h in range(MLA_HEADS):
        c0 = h * MLA_QK_PAD
        km_ref[:, c0:c0 + LANES] = kv[:, c0:c0 + LANES].astype(BF16)
        km_ref[:, c0 + LANES:c0 + 2 * LANES] = kr
        vm_ref[:, h * MLA_V_DIM:(h + 1) * MLA_V_DIM] = kv[:, c0 + LANES:c0 + 2 * LANES].astype(BF16)


def _inproj(rows, xp, xs, mod, g_pre, w_in_aug, g_q_a, w_q, g_kv_a, w_kv, cos_t, sin_t):
    tm, d = rows.tm, D_MODEL
    const = lambda i: (0, 0)
    row = lambda i: (i, 0)
    one = pl.Buffered(1)
    out_w = (DIFF_WIDTH, DIFF_WIDTH, DIFF_WIDTH, MLA_HEADS * MLA_QK_PAD, MLA_HEADS * MLA_QK_PAD, D_MLA_OUT)
    return pl.pallas_call(
        functools.partial(_inproj_kernel, rows.n_p),
        grid=(rows.n,),
        in_specs=[pl.BlockSpec((tm, d), lambda i: (rows.p_idx(i), 0)),
                  pl.BlockSpec((tm, d), lambda i: (rows.s_idx(i), 0)),
                  pl.BlockSpec((1, N_MOD, d), lambda i: (rows.seq(i), 0, 0)),
                  pl.BlockSpec((1, d), const),
                  pl.BlockSpec((d, D_IN_AUG), const, pipeline_mode=one),
                  pl.BlockSpec((1, Q_LORA_RANK), const),
                  pl.BlockSpec((Q_LORA_RANK, MLA_HEADS * MLA_QK_PAD), const, pipeline_mode=one),
                  pl.BlockSpec((1, KV_LORA_RANK), const),
                  pl.BlockSpec((KV_LORA_RANK, MLA_HEADS * MLA_QK_PAD), const, pipeline_mode=one),
                  pl.BlockSpec((tm, LANES), lambda i: (rows.pos(i), 0)),
                  pl.BlockSpec((tm, LANES), lambda i: (rows.pos(i), 0))],
        out_specs=[pl.BlockSpec((tm, w), row) for w in out_w],
        out_shape=[jax.ShapeDtypeStruct((rows.t, w), BF16) for w in out_w],
        compiler_params=pltpu.CompilerParams(dimension_semantics=("arbitrary",),
                                             vmem_limit_bytes=VMEM_LIMIT),
        name="inproj",
    )(xp, xs, mod, g_pre, w_in_aug, g_q_a, w_q, g_kv_a, w_kv, cos_t, sin_t)


def _lane_groups(x, op):
    out = x[:, 0:LANES]
    for g in range(1, x.shape[1] // LANES):
        out = op(out, x[:, g * LANES:(g + 1) * LANES])
    return out


def _aligned(x, m):
    return x if isinstance(x, int) else pl.multiple_of(x, m)


def _chunk(ref, j, tk):
    return ref[pl.ds(_aligned(j * tk, tk), tk), :]


def _score_block(unit, s_view, jb, cpb, tk, mpart):
    score_fn, shift_fn = unit
    for c in range(cpb):
        j = jb * cpb + c
        s = score_fn(j)
        s_view[jb, :, c * tk:(c + 1) * tk] = s
        part = _lane_groups(s, jnp.maximum)
        if shift_fn is not None:
            part = part + shift_fn(j)
        mpart = jnp.maximum(mpart, part)
    return mpart


def _value_block(unit, m, v_ref, s_view, jb, cpb, tk, lpart, acc):
    _, shift_fn = unit
    for c in range(cpb):
        j = jb * cpb + c
        mj = m if shift_fn is None else m - shift_fn(j)
        p = jnp.exp2(s_view[jb, :, c * tk:(c + 1) * tk] - mj)
        lpart = lpart + _lane_groups(p, jnp.add)
        acc = acc + _dot(p.astype(BF16), _chunk(v_ref, j, tk))
    return lpart, acc


def _attn_phase(geom, v_ref, score=None, value=None):
    tq, tk, cpb, nkb = geom
    zeros = jnp.zeros((tq, LANES), F32)

    def body(jb, carry):
        mpart, lpart, acc = carry
        if score is not None:
            mpart = _score_block(score[0], score[1], jb, cpb, tk, mpart)
        if value is not None:
            lpart, acc = _value_block(value[0], value[1], v_ref, value[2], jb, cpb, tk, lpart, acc)
        return mpart, lpart, acc

    mpart, lpart, acc = lax.fori_loop(0, nkb, body, (jnp.full((tq, LANES), -jnp.inf, F32), zeros, zeros),
                                      unroll=min(ATTN_UNROLL, nkb))
    m = None if score is None else jnp.max(mpart, axis=-1, keepdims=True)
    out = None if value is None else acc / jnp.sum(lpart, axis=-1, keepdims=True)
    return m, out


def _diff_attn_kernel(lambda_init, geom, n_sub, slopes_ref, lam_ref, gsub_ref, q_ref, k_ref, v_ref,
                      o_ref, s_buf, bias_buf):
    tq, tk, _, _ = geom
    h = pl.program_id(1)
    qo = pl.program_id(2)
    slope = slopes_ref[h] * LOG2E
    n_cross = tk // tq

    r_minus_c = (lax.broadcasted_iota(jnp.int32, (tq, tk), 0)
                 - lax.broadcasted_iota(jnp.int32, (tq, tk), 1)).astype(F32)
    bias_buf[0] = r_minus_c * (-slope)
    bias_buf[1] = r_minus_c * slope
    for c in range(n_cross):
        bias_buf[2 + c] = jnp.abs(r_minus_c + float(c * tq)) * (-slope)

    lp = lam_ref[...]
    lam = (jnp.exp(jnp.sum(lp[0:1] * lp[1:2], axis=-1, keepdims=True))
           - jnp.exp(jnp.sum(lp[2:3] * lp[3:4], axis=-1, keepdims=True)) + lambda_init)
    gsub = gsub_ref[...]

    def unit(sub, half):
        qbase = (qo * n_sub + sub) * tq
        jc = qbase // tk
        cross = 2 + (qbase - jc * tk) // tq
        q = q_ref[pl.ds(_aligned(sub * tq, tq), tq), :]
        lane = lax.broadcasted_iota(jnp.int32, q.shape, 1)
        keep = (lane < DIFF_HEAD_DIM) if half == 0 else (lane >= DIFF_HEAD_DIM)
        qh = jnp.where(keep, q, jnp.zeros_like(q))
        shift = lambda j: jnp.where(j == jc, 0.0, -slope * jnp.abs(qbase - j * tk).astype(F32))
        which = lambda j: jnp.where(j < jc, 0, jnp.where(j > jc, 1, cross))
        return (lambda j: _dot_nt(qh, _chunk(k_ref, j, tk)) + bias_buf[which(j)]), shift

    buf0, buf1 = s_buf.at[0], s_buf.at[1]

    def step(sub, m0, has_next):
        u0, u1 = unit(sub, 0), unit(sub, 1)
        m1, o0 = _attn_phase(geom, v_ref, score=(u1, buf1), value=(u0, m0, buf0))
        nxt = (unit(sub + 1, 0), buf0) if has_next else None
        m0_next, o1 = _attn_phase(geom, v_ref, score=nxt, value=(u1, m1, buf1))
        o = o0 - lam * o1
        o_ref[pl.ds(_aligned(sub * tq, tq), tq), :] = (_rms(o, gsub) * (1.0 - lambda_init)).astype(BF16)
        return m0_next

    m0, _ = _attn_phase(geom, v_ref, score=(unit(0, 0), buf0))
    m0 = lax.fori_loop(0, n_sub - 1, lambda sub, m: step(sub, m, True), m0)
    step(n_sub - 1, m0, False)


def _attn_tiles(s, row0):
    tq = min(ATTN_TQ, s)
    tk = min(ATTN_TK, s)
    tkb = min(ATTN_TKB, s)
    tq_outer = min(ATTN_TQ_OUTER, s)
    assert s % tq_outer == 0 and tq_outer % tq == 0 and s % tkb == 0 and tkb % tk == 0 and tk % tq == 0
    assert row0 % s == 0
    return (tq, tk, tkb // tk, s // tkb), tq_outer, (2, s // tkb, tq, tkb)


def _diff_attn(dq, dk, dv, slopes, lam_p, g_sub, row0, b, s, lambda_init):
    geom, tq_outer, s_buf_shape = _attn_tiles(s, row0)
    tq, tk = geom[0], geom[1]
    n_sub, nqo = tq_outer // tq, s // tq_outer
    dh = 2 * DIFF_HEAD_DIM
    return pl.pallas_call(
        functools.partial(_diff_attn_kernel, lambda_init, geom, n_sub),
        grid=(b, DIFF_HEADS, nqo),
        in_specs=[pl.BlockSpec(memory_space=pltpu.SMEM),
                  pl.BlockSpec((8, LANES), lambda bi, h, qi: (0, 0)),
                  pl.BlockSpec((1, dh), lambda bi, h, qi: (0, 0)),
                  pl.BlockSpec((tq_outer, dh), lambda bi, h, qi: (row0 // tq_outer + bi * nqo + qi, h)),
                  pl.BlockSpec((s, dh), lambda bi, h, qi: (row0 // s + bi, h)),
                  pl.BlockSpec((s, dh), lambda bi, h, qi: (row0 // s + bi, h))],
        out_specs=pl.BlockSpec((tq_outer, dh), lambda bi, h, qi: (bi * nqo + qi, h)),
        out_shape=jax.ShapeDtypeStruct((b * s, DIFF_WIDTH), BF16),
        scratch_shapes=[pltpu.VMEM(s_buf_shape, F32), pltpu.VMEM((2 + tk // tq, tq, tk), F32)],
        compiler_params=pltpu.CompilerParams(
            dimension_semantics=("arbitrary", "arbitrary", "arbitrary"), vmem_limit_bytes=VMEM_LIMIT),
        name="diff_attn",
    )(slopes, lam_p, g_sub, dq, dk, dv)


def _mla_attn_kernel(geom, n_sub, q_ref, k_ref, v_ref, o_ref, s_buf):
    tq, tk, _, _ = geom

    def unit(sub):
        q = q_ref[pl.ds(_aligned(sub * tq, tq), tq), :]
        return (lambda j: _dot_nt(q, _chunk(k_ref, j, tk))), None

    def store(sub, o):
        o_ref[pl.ds(_aligned(sub * tq, tq), tq), :] = o.astype(BF16)

    def step(sub, m):
        slot = sub % 2
        m_next, o = _attn_phase(geom, v_ref, score=(unit(sub + 1), s_buf.at[1 - slot]),
                                value=(unit(sub), m, s_buf.at[slot]))
        store(sub, o)
        return m_next

    def unpipelined(sub, carry):
        m, _ = _attn_phase(geom, v_ref, score=(unit(sub), s_buf.at[0]))
        _, o = _attn_phase(geom, v_ref, value=(unit(sub), m, s_buf.at[0]))
        store(sub, o)
        return carry

    if geom[3] == 1:
        lax.fori_loop(0, n_sub, unpipelined, 0)
        return
    m, _ = _attn_phase(geom, v_ref, score=(unit(0), s_buf.at[0]))
    m = lax.fori_loop(0, n_sub - 1, step, m)
    last = n_sub - 1
    _, o = _attn_phase(geom, v_ref, value=(unit(last), m, s_buf.at[last % 2]))
    store(last, o)


def _mla_attn(qm, km, vm, row0, b, s):
    geom, tq_outer, s_buf_shape = _attn_tiles(s, row0)
    n_sub, nqo = tq_outer // geom[0], s // tq_outer
    return pl.pallas_call(
        functools.partial(_mla_attn_kernel, geom, n_sub),
        grid=(b, MLA_HEADS, nqo),
        in_specs=[pl.BlockSpec((tq_outer, MLA_QK_PAD), lambda bi, h, qi: (row0 // tq_outer + bi * nqo + qi, h)),
                  pl.BlockSpec((s, MLA_QK_PAD), lambda bi, h, qi: (row0 // s + bi, h)),
                  pl.BlockSpec((s, MLA_V_DIM), lambda bi, h, qi: (row0 // s + bi, h))],
        out_specs=pl.BlockSpec((tq_outer, MLA_V_DIM), lambda bi, h, qi: (bi * nqo + qi, h)),
        out_shape=jax.ShapeDtypeStruct((b * s, D_MLA_OUT), BF16),
        scratch_shapes=[pltpu.VMEM(s_buf_shape, F32)],
        compiler_params=pltpu.CompilerParams(
            dimension_semantics=("arbitrary", "arbitrary", "arbitrary"), vmem_limit_bytes=VMEM_LIMIT),
        name="mla_attn",
    )(qm, km, vm)


def _route(logits):
    lane = lax.broadcasted_iota(jnp.int32, logits.shape, 1)
    neg = jnp.full(logits.shape, -jnp.inf, F32)
    big = jnp.full(logits.shape, LANES, jnp.int32)
    first = lambda mask: jnp.min(jnp.where(mask, lane, big), axis=-1, keepdims=True)

    gl = jnp.where(lane < N_GROUPS, logits, neg)
    gmax = jnp.max(gl, axis=-1, keepdims=True)
    g_idx = first(gl == gmax)
    g_w = 1.0 / jnp.sum(jnp.exp(gl - gmax), axis=-1, keepdims=True)

    lo = ROUTER_LANE0 + EXPERTS_PER_GROUP * g_idx
    el = jnp.where(lane >= lo, jnp.where(lane < lo + EXPERTS_PER_GROUP, logits, neg), neg)
    v1 = jnp.max(el, axis=-1, keepdims=True)
    i1 = first(el == v1)
    el2 = jnp.where(lane == i1, neg, el)
    v2 = jnp.max(el2, axis=-1, keepdims=True)
    i2 = first(el2 == v2)
    t = jnp.exp(v2 - v1)
    w1 = g_w / (1.0 + t)
    w2 = w1 * t
    first_low = i1 < i2
    ea = jnp.minimum(i1, i2) - lo
    eb = jnp.maximum(i1, i2) - lo
    pair = jnp.where(ea == 0, 0, jnp.where(ea == 1, 3, 5)) + eb - ea - 1
    bucket = g_idx * N_PAIRS + pair
    return bucket, jnp.where(first_low, w1, w2), jnp.where(first_low, w2, w1)


def _oproj_kernel(n_p, oap_ref, oas_ref, obp_ref, obs_ref, xp_ref, xs_ref, mod_ref, wo_ref, gpost_ref,
                  gpre_ref, wr_ref, br_ref, xmid_ref, hx_ref, meta_ref, counts_ref, cnt):
    i = pl.program_id(0)

    @pl.when(i == 0)
    def _():
        cnt[...] = jnp.zeros(cnt.shape, F32)

    is_p = i < n_p
    oa = jnp.where(is_p, oap_ref[...], oas_ref[...])
    ob = jnp.where(is_p, obp_ref[...], obs_ref[...])
    x = jnp.where(is_p, xp_ref[...], xs_ref[...])
    mix = _dot(oa, wo_ref[0:DIFF_WIDTH, :]) + _dot(ob, wo_ref[DIFF_WIDTH:DIFF_WIDTH + D_MLA_OUT, :])
    x_mid = x + mod_ref[0, 2:3, :] * _rms(mix, gpost_ref[...])
    xmid_ref[...] = x_mid
    h2 = _rms(x_mid, gpre_ref[...]) * (1.0 + mod_ref[0, 4:5, :]) + mod_ref[0, 3:4, :]
    h_hi = h2.astype(BF16)
    h_lo = (h2 - h_hi.astype(F32)).astype(BF16)
    w_r = wr_ref[...]
    w_hi = w_r.astype(BF16)
    w_lo = (w_r - w_hi.astype(F32)).astype(BF16)
    logits = _dot(h_hi, w_hi) + _dot(h_hi, w_lo) + _dot(h_lo, w_hi) + br_ref[...]
    bucket, wa, wb = _route(logits)

    lane = lax.broadcasted_iota(jnp.int32, logits.shape, 1)
    zero = jnp.zeros_like(logits)
    hx_ref[:, 0:D_MODEL] = h2
    hx_ref[:, D_MODEL:HX_WIDTH] = jnp.where(lane == 0, wa, jnp.where(lane == 1, wb, zero))

    tm = logits.shape[0]
    onehot = jnp.where(lane == bucket, 1.0, 0.0)
    earlier = (lax.broadcasted_iota(jnp.int32, (tm, tm), 0) > lax.broadcasted_iota(jnp.int32, (tm, tm), 1))
    before = _dot(jnp.where(earlier, 1.0, 0.0).astype(BF16), onehot.astype(BF16))
    rank = jnp.sum(onehot * (before + cnt[0:1, :]), axis=-1, keepdims=True)
    cnt[...] = cnt[...] + jnp.sum(onehot, axis=0, keepdims=True)
    counts_ref[...] = cnt[...]
    meta_ref[...] = jnp.where(lane == 0, bucket, jnp.where(lane == 1, rank.astype(jnp.int32), 0))


def _oproj(rows, oa_p, oa_s, ob_p, ob_s, xp, xs, mod, w_o, g_post, g_pre, w_r, b_r):
    tm, d = rows.tm, D_MODEL
    const = lambda i: (0, 0)
    row = lambda i: (i, 0)
    prow = lambda i: (rows.p_idx(i), 0)
    srow = lambda i: (rows.s_idx(i), 0)
    return pl.pallas_call(
        functools.partial(_oproj_kernel, rows.n_p),
        grid=(rows.n,),
        in_specs=[pl.BlockSpec((tm, DIFF_WIDTH), prow), pl.BlockSpec((tm, DIFF_WIDTH), srow),
                  pl.BlockSpec((tm, D_MLA_OUT), prow), pl.BlockSpec((tm, D_MLA_OUT), srow),
                  pl.BlockSpec((tm, d), prow), pl.BlockSpec((tm, d), srow),
                  pl.BlockSpec((1, N_MOD, d), lambda i: (rows.seq(i), 0, 0)),
                  pl.BlockSpec((DIFF_WIDTH + D_MLA_OUT, d), const, pipeline_mode=pl.Buffered(1)),
                  pl.BlockSpec((1, d), const), pl.BlockSpec((1, d), const),
                  pl.BlockSpec((d, LANES), const), pl.BlockSpec((1, LANES), const)],
        out_specs=[pl.BlockSpec((tm, d), row), pl.BlockSpec((tm, HX_WIDTH), row),
                   pl.BlockSpec((tm, LANES), row), pl.BlockSpec((8, LANES), const)],
        out_shape=[jax.ShapeDtypeStruct((rows.t, d), F32), jax.ShapeDtypeStruct((rows.t, HX_WIDTH), F32),
                   jax.ShapeDtypeStruct((rows.t, LANES), jnp.int32), jax.ShapeDtypeStruct((8, LANES), F32)],
        scratch_shapes=[pltpu.VMEM((8, LANES), F32)],
        compiler_params=pltpu.CompilerParams(dimension_semantics=("arbitrary",),
                                             vmem_limit_bytes=VMEM_LIMIT),
        name="oproj",
    )(oa_p, oa_s, ob_p, ob_s, xp, xs, mod, w_o, g_post, g_pre, w_r, b_r)


def _gather_rows(idx_ref, src_hbm, dst, sem):
    def body(r, carry):
        pltpu.make_async_copy(src_hbm.at[pl.ds(idx_ref[0, 0, r], 1), :], dst.at[pl.ds(r, 1), :], sem).start()
        return carry

    lax.fori_loop(0, dst.shape[0], body, 0, unroll=8)


def _wait_rows(src_hbm, dst, sem):
    pltpu.make_async_copy(src_hbm.at[pl.ds(0, dst.shape[0]), :], dst, sem).wait()


def _experts_kernel(ea_ref, eb_ref, nv_ref, inv_cur, inv_nxt, hx_hbm, wga, wua, wda, wgb, wub, wdb,
                    f_ref, xbuf, sem):
    i = pl.program_id(0)
    n_valid = nv_ref[0]
    slot = i % 2

    @pl.when(i == 0)
    def _():
        _gather_rows(inv_cur, hx_hbm, xbuf.at[0], sem.at[0])

    @pl.when(i + 1 < n_valid)
    def _():
        _gather_rows(inv_nxt, hx_hbm, xbuf.at[1 - slot], sem.at[1 - slot])

    @pl.when(i < n_valid)
    def _():
        _wait_rows(hx_hbm, xbuf.at[slot], sem.at[slot])
        x = xbuf[slot, :, 0:D_MODEL].astype(BF16)
        aux = xbuf[slot, :, D_MODEL:HX_WIDTH]

        def hidden(wg, wu, w):
            g = _dot(x, wg[0])
            return ((g / (1.0 + jnp.exp(-g))) * _dot(x, wu[0]) * w).astype(BF16)

        f_ref[...] = (_dot(hidden(wga, wua, aux[:, 0:1]), wda[0])
                      + _dot(hidden(wgb, wub, aux[:, 1:2]), wdb[0]))

    @pl.when(i >= n_valid)
    def _():
        f_ref[...] = jnp.zeros(f_ref.shape, F32)


def _experts(hx, inv3, tile_ea, tile_eb, n_valid, w_gate, w_up, w_down):
    n_tiles, _, tm = inv3.shape
    d = D_MODEL
    wspec = lambda shape, which: pl.BlockSpec(
        (1,) + shape, (lambda i, ea, eb, nv: (ea[i], 0, 0)) if which == 0 else (lambda i, ea, eb, nv: (eb[i], 0, 0)))
    smem = lambda imap: pl.BlockSpec((1, 1, tm), imap, memory_space=pltpu.SMEM)
    return pl.pallas_call(
        _experts_kernel,
        grid_spec=pltpu.PrefetchScalarGridSpec(
            num_scalar_prefetch=3,
            grid=(n_tiles,),
            in_specs=[smem(lambda i, ea, eb, nv: (i, 0, 0)),
                      smem(lambda i, ea, eb, nv: (jnp.minimum(i + 1, n_tiles - 1), 0, 0)),
                      pl.BlockSpec(memory_space=pl.ANY),
                      wspec((d, D_EXPERT), 0), wspec((d, D_EXPERT), 0), wspec((D_EXPERT, d), 0),
                      wspec((d, D_EXPERT), 1), wspec((d, D_EXPERT), 1), wspec((D_EXPERT, d), 1)],
            out_specs=pl.BlockSpec((tm, d), lambda i, ea, eb, nv: (i, 0)),
            scratch_shapes=[pltpu.VMEM((2, tm, HX_WIDTH), F32), pltpu.SemaphoreType.DMA((2,))]),
        out_shape=jax.ShapeDtypeStruct((n_tiles * tm, d), F32),
        compiler_params=pltpu.CompilerParams(dimension_semantics=("arbitrary",),
                                             vmem_limit_bytes=VMEM_LIMIT),
        name="experts",
    )(tile_ea, tile_eb, n_valid, inv3, inv3, hx, w_gate, w_up, w_down, w_gate, w_up, w_down)


def _final_kernel(n_p, pos_cur, pos_nxt, f_hbm, xmid_ref, mod_ref, gpost_ref, yp_ref, ys_ref, fbuf, sem):
    i = pl.program_id(0)
    slot = i % 2

    @pl.when(i == 0)
    def _():
        _gather_rows(pos_cur, f_hbm, fbuf.at[0], sem.at[0])

    @pl.when(i + 1 < pl.num_programs(0))
    def _():
        _gather_rows(pos_nxt, f_hbm, fbuf.at[1 - slot], sem.at[1 - slot])

    _wait_rows(f_hbm, fbuf.at[slot], sem.at[slot])
    y = xmid_ref[...] + mod_ref[0, 5:6, :] * _rms(fbuf[slot], gpost_ref[...])

    @pl.when(i < n_p)
    def _():
        yp_ref[...] = y

    @pl.when(i >= n_p)
    def _():
        ys_ref[...] = y


def _final(rows, pos3, f_sorted, x_mid, mod, g_post):
    tm, d = rows.tm, D_MODEL
    smem = lambda imap: pl.BlockSpec((1, 1, tm), imap, memory_space=pltpu.SMEM)
    return pl.pallas_call(
        functools.partial(_final_kernel, rows.n_p),
        grid=(rows.n,),
        in_specs=[smem(lambda i: (i, 0, 0)),
                  smem(lambda i: (jnp.minimum(i + 1, rows.n - 1), 0, 0)),
                  pl.BlockSpec(memory_space=pl.ANY),
                  pl.BlockSpec((tm, d), lambda i: (i, 0)),
                  pl.BlockSpec((1, N_MOD, d), lambda i: (rows.seq(i), 0, 0)),
                  pl.BlockSpec((1, d), lambda i: (0, 0))],
        out_specs=[pl.BlockSpec((tm, d), lambda i: (rows.p_idx(i), 0)),
                   pl.BlockSpec((tm, d), lambda i: (rows.s_idx(i), 0))],
        out_shape=[jax.ShapeDtypeStruct((rows.tp, d), F32), jax.ShapeDtypeStruct((rows.ts, d), F32)],
        scratch_shapes=[pltpu.VMEM((2, tm, d), F32), pltpu.SemaphoreType.DMA((2,))],
        compiler_params=pltpu.CompilerParams(dimension_semantics=("arbitrary",),
                                             vmem_limit_bytes=VMEM_LIMIT),
        name="final",
    )(pos3, pos3, f_sorted, x_mid, mod, g_post)


def _moe_plan(meta, counts, tm):
    t = meta.shape[0]
    n_tiles = t // tm + N_BUCKETS
    cnt = counts[0, :N_BUCKETS].astype(jnp.int32)
    padded = (cnt + tm - 1) // tm * tm
    ends = jnp.cumsum(padded)
    pos = (ends - padded)[meta[:, 0]] + meta[:, 1]
    inv = jnp.zeros((n_tiles * tm,), jnp.int32).at[pos].set(jnp.arange(t, dtype=jnp.int32))
    n_valid = ends[-1] // tm
    tile = jnp.minimum(jnp.arange(n_tiles, dtype=jnp.int32), n_valid - 1)
    bucket = jnp.searchsorted(ends, tile * tm, side="right").astype(jnp.int32)
    group, pair = bucket // N_PAIRS, bucket % N_PAIRS
    ea = jnp.array([0, 0, 0, 1, 1, 2], jnp.int32)[pair]
    eb = jnp.array([1, 2, 3, 2, 3, 3], jnp.int32)[pair]
    base = group * EXPERTS_PER_GROUP
    return pos, inv, base + ea, base + eb, n_valid.reshape(1).astype(jnp.int32)


def _rotate_half_cols(w):
    half = MLA_ROPE_DIM // 2
    return jnp.concatenate([-w[..., half:], w[..., :half]], axis=-1)


def _rope_tables(s_max):
    half = MLA_ROPE_DIM // 2
    inv = ROPE_BASE ** (-jnp.arange(half, dtype=F32) / half)
    ang = jnp.arange(s_max, dtype=F32)[:, None] * inv[None, :]
    pad = jnp.zeros((s_max, LANES - MLA_ROPE_DIM), F32)
    cos, sin = jnp.cos(ang), jnp.sin(ang)
    return (jnp.concatenate([cos, cos, pad], axis=-1), jnp.concatenate([sin, sin, pad], axis=-1))


def _layer(x_prompt, x_sample, c_prompt, c_sample, layer_idx, w_ada, b_ada, g_pre_mix, g_post_mix, w_in,
           lambda_q1, lambda_k1, lambda_q2, lambda_k2, g_diff_sub, g_q_a, w_uq, g_kv_a, w_ukv, w_o,
           g_pre_ffn, g_post_ffn, w_router_group, b_router_group, w_router_expert, b_router_expert,
           w_gate, w_up, w_down, tm=256):
    bp, sp, d = x_prompt.shape
    bs, ss, _ = x_sample.shape
    rows = _Rows(bp, sp, bs, ss, tm)
    xp = x_prompt.reshape(bp * sp, d)
    xs = x_sample.reshape(bs * ss, d)
    lambda_init = 0.8 - 0.6 * math.exp(-0.3 * layer_idx)

    nb = bp + bs
    nb_pad = -(-nb // 8) * 8
    c_all = jnp.concatenate([c_prompt, c_sample, jnp.zeros((nb_pad - nb, d), F32)], axis=0)
    mod = _ada(c_all, w_ada, b_ada.reshape(1, -1)).reshape(nb_pad, N_MOD, d)

    w_in_aug = jnp.concatenate([w_in, _rotate_half_cols(w_in[:, C_KR:C_KR + MLA_ROPE_DIM])], axis=1).astype(BF16)
    wq3 = w_uq.reshape(Q_LORA_RANK, MLA_HEADS, MLA_NOPE_DIM + MLA_ROPE_DIM)
    wq_rope = wq3[..., MLA_NOPE_DIM:]
    w_q = jnp.concatenate([wq3[..., :MLA_NOPE_DIM], wq_rope, _rotate_half_cols(wq_rope)], axis=-1)
    w_q = w_q.reshape(Q_LORA_RANK, MLA_HEADS * MLA_QK_PAD).astype(BF16)
    w_kv = w_ukv.astype(BF16)
    cos_t, sin_t = _rope_tables(max(sp, ss))

    dq, dk, dv, qm, km, vm = _inproj(rows, xp, xs, mod, g_pre_mix.reshape(1, d), w_in_aug,
                                     g_q_a.reshape(1, -1), w_q, g_kv_a.reshape(1, -1), w_kv, cos_t, sin_t)

    slopes = jnp.array([2.0 ** (-8.0 * (i + 1) / DIFF_HEADS) for i in range(DIFF_HEADS)], dtype=F32)
    lam_p = jnp.zeros((8, LANES), F32).at[0:4, 0:DIFF_HEAD_DIM].set(
        jnp.stack([lambda_q1, lambda_k1, lambda_q2, lambda_k2]))
    g_sub = g_diff_sub.reshape(1, -1)
    oa_p = _diff_attn(dq, dk, dv, slopes, lam_p, g_sub, 0, bp, sp, lambda_init)
    oa_s = _diff_attn(dq, dk, dv, slopes, lam_p, g_sub, rows.tp, bs, ss, lambda_init)
    ob_p = _mla_attn(qm, km, vm, 0, bp, sp)
    ob_s = _mla_attn(qm, km, vm, rows.tp, bs, ss)

    w_r = jnp.concatenate([w_router_group, w_router_expert.reshape(d, N_EXPERTS),
                           jnp.zeros((d, LANES - N_GROUPS - N_EXPERTS), F32)], axis=1)
    b_r = jnp.concatenate([b_router_group, b_router_expert.reshape(N_EXPERTS),
                           jnp.zeros((LANES - N_GROUPS - N_EXPERTS,), F32)]).reshape(1, LANES)
    x_mid, hx, meta, counts = _oproj(rows, oa_p, oa_s, ob_p, ob_s, xp, xs, mod, w_o.astype(BF16),
                                     g_post_mix.reshape(1, d), g_pre_ffn.reshape(1, d), w_r, b_r)

    pos, inv, tile_ea, tile_eb, n_valid = _moe_plan(meta, counts, MOE_TM)
    f_sorted = _experts(hx, inv.reshape(-1, 1, MOE_TM), tile_ea, tile_eb, n_valid,
                        w_gate.astype(BF16), w_up.astype(BF16), w_down.astype(BF16))
    yp, ys = _final(rows, pos.reshape(rows.n, 1, tm), f_sorted, x_mid, mod, g_post_ffn.reshape(1, d))
    return yp.reshape(bp, sp, d), ys.reshape(bs, ss, d)


def kernel(x_prompt, x_sample, c_prompt, c_sample, w_ada, b_ada, g_pre_mix, g_post_mix, w_in, lambda_q1,
           lambda_k1, lambda_q2, lambda_k2, g_diff_sub, g_q_a, w_uq, g_kv_a, w_ukv, w_o, g_pre_ffn,
           g_post_ffn, w_router_group, b_router_group, w_router_expert, b_router_expert, w_gate, w_up,
           w_down):
    for l in range(w_ada.shape[0]):
        x_prompt, x_sample = _layer(
            x_prompt, x_sample, c_prompt, c_sample, l, w_ada[l], b_ada[l], g_pre_mix[l], g_post_mix[l],
            w_in[l], lambda_q1[l], lambda_k1[l], lambda_q2[l], lambda_k2[l], g_diff_sub[l], g_q_a[l],
            w_uq[l], g_kv_a[l], w_ukv[l], w_o[l], g_pre_ffn[l], g_post_ffn[l], w_router_group[l],
            b_router_group[l], w_router_expert[l], b_router_expert[l], w_gate[l], w_up[l], w_down[l])
    return x_prompt, x_sample
```

```python
import functools
import math

import jax
import jax.numpy as jnp
from jax import lax
from jax.experimental import pallas as pl
from jax.experimental.pallas import tpu as pltpu

F32 = jnp.float32
BF16 = jnp.bfloat16

D_MODEL = 2048
DIFF_HEADS = 8
DIFF_HEAD_DIM = 64
DIFF_WIDTH = DIFF_HEADS * 2 * DIFF_HEAD_DIM
MLA_HEADS = 8
MLA_NOPE_DIM = 128
MLA_ROPE_DIM = 64
MLA_V_DIM = 128
MLA_QK_PAD = 256
Q_LORA_RANK = 512
KV_LORA_RANK = 256
ROPE_BASE = 10000.0
D_MLA_OUT = MLA_HEADS * MLA_V_DIM
N_GROUPS = 4
EXPERTS_PER_GROUP = 4
N_EXPERTS = N_GROUPS * EXPERTS_PER_GROUP
D_EXPERT = 512
N_MOD = 6
NORM_EPS = 1e-6
LANES = 128
ROUTER_LANE0 = N_GROUPS
N_PAIRS = 6
N_BUCKETS = N_GROUPS * N_PAIRS
HX_WIDTH = D_MODEL + LANES
MOE_TM = 256

C_DQ, C_DK, C_DV = 0, DIFF_WIDTH, 2 * DIFF_WIDTH
C_CQ = 3 * DIFF_WIDTH
C_CKV = C_CQ + Q_LORA_RANK
C_KR = C_CKV + KV_LORA_RANK
D_IN_AUG = C_KR + 2 * MLA_ROPE_DIM

VMEM_LIMIT = 56 * 1024 * 1024
LOG2E = 1.4426950408889634
ATTN_TQ = 256
ATTN_TK = 512
ATTN_TQ_OUTER = 1024
ATTN_TKB = 2048
ATTN_UNROLL = 2


def _rms(x, g):
    return x * lax.rsqrt(jnp.mean(x * x, axis=-1, keepdims=True) + NORM_EPS) * g


def _dot(a, b):
    return jnp.dot(a, b, preferred_element_type=F32)


def _dot_nt(a, b):
    return lax.dot_general(a, b, (((1,), (1,)), ((), ())), preferred_element_type=F32)


class _Rows:
    def __init__(self, bp, sp, bs, ss, tm):
        assert sp % tm == 0 and ss % tm == 0
        self.bp, self.sp, self.bs, self.ss, self.tm = bp, sp, bs, ss, tm
        self.n_p = bp * sp // tm
        self.n_s = bs * ss // tm
        self.n = self.n_p + self.n_s
        self.tp = bp * sp
        self.ts = bs * ss
        self.t = self.tp + self.ts

    def p_idx(self, i):
        return jnp.minimum(i, self.n_p - 1)

    def s_idx(self, i):
        return jnp.maximum(i - self.n_p, 0)

    def seq(self, i):
        return jnp.where(i < self.n_p, i // (self.sp // self.tm),
                         self.bp + (i - self.n_p) // (self.ss // self.tm))

    def pos(self, i):
        return jnp.where(i < self.n_p, i % (self.sp // self.tm),
                         (i - self.n_p) % (self.ss // self.tm))


def _ada_kernel(c_ref, w_ref, b_ref, o_ref):
    c = c_ref[...]
    a = c / (1.0 + jnp.exp(-c))
    o_ref[...] = jnp.dot(a, w_ref[...], preferred_element_type=F32,
                         precision=lax.Precision.HIGHEST) + b_ref[...]


def _ada(c_all, w_ada, b_ada):
    nb, d = c_all.shape
    n = w_ada.shape[1]
    tn = 1024
    return pl.pallas_call(
        _ada_kernel,
        grid=(n // tn,),
        in_specs=[pl.BlockSpec((nb, d), lambda j: (0, 0)),
                  pl.BlockSpec((d, tn), lambda j: (0, j)),
                  pl.BlockSpec((1, tn), lambda j: (0, j))],
        out_specs=pl.BlockSpec((nb, tn), lambda j: (0, j)),
        out_shape=jax.ShapeDtypeStruct((nb, n), F32),
        compiler_params=pltpu.CompilerParams(dimension_semantics=("arbitrary",),
                                             vmem_limit_bytes=VMEM_LIMIT),
        name="ada",
    )(c_all, w_ada, b_ada)


def _rope_half(x, cos, sin):
    return x * cos + pltpu.roll(x, MLA_ROPE_DIM, 1) * sin


def _inproj_kernel(n_p, xp_ref, xs_ref, mod_ref, gpre_ref, win_ref, gqa_ref, wq_ref, gkva_ref,
                   wkv_ref, cos_ref, sin_ref, dq_ref, dk_ref, dv_ref, qm_ref, km_ref, vm_ref):
    i = pl.program_id(0)
    x = jnp.where(i < n_p, xp_ref[...], xs_ref[...])
    shift = mod_ref[0, 0:1, :]
    scale = mod_ref[0, 1:2, :]
    hb = (_rms(x, gpre_ref[...]) * (1.0 + scale) + shift).astype(BF16)

    dq_ref[...] = (_dot(hb, win_ref[:, C_DQ:C_DQ + DIFF_WIDTH]) * (DIFF_HEAD_DIM ** -0.5 * LOG2E)).astype(BF16)
    dk_ref[...] = _dot(hb, win_ref[:, C_DK:C_DK + DIFF_WIDTH]).astype(BF16)
    dv_ref[...] = _dot(hb, win_ref[:, C_DV:C_DV + DIFF_WIDTH]).astype(BF16)

    cos = cos_ref[...]
    sin = sin_ref[...]
    cq = _dot(hb, win_ref[:, C_CQ:C_CQ + Q_LORA_RANK])
    ckv = _dot(hb, win_ref[:, C_CKV:C_CKV + KV_LORA_RANK])
    kr = _rope_half(_dot(hb, win_ref[:, C_KR:C_KR + 2 * MLA_ROPE_DIM]), cos, sin).astype(BF16)

    q = _dot(_rms(cq, gqa_ref[...]).astype(BF16), wq_ref[...])
    qscale = (MLA_NOPE_DIM + MLA_ROPE_DIM) ** -0.5 * LOG2E
    for h in range(MLA_HEADS):
        c0 = h * MLA_QK_PAD
        qm_ref[:, c0:c0 + LANES] = (q[:, c0:c0 + LANES] * qscale).astype(BF16)
        qm_ref[:, c0 + LANES:c0 + 2 * LANES] = (
            _rope_half(q[:, c0 + LANES:c0 + 2 * LANES], cos, sin) * qscale).astype(BF16)

    kv = _dot(_rms(ckv, gkva_ref[...]).astype(BF16), wkv_ref[...])
    for h in range(MLA_HEADS):
        c0 = h * MLA_QK_PAD
        km_ref[:, c0:c0 + LANES] = kv[:, c0:c0 + LANES].astype(BF16)
        km_ref[:, c0 + LANES:c0 + 2 * LANES] = kr
        vm_ref[:, h * MLA_V_DIM:(h + 1) * MLA_V_DIM] = kv[:, c0 + LANES:c0 + 2 * LANES].astype(BF16)


def _inproj(rows, xp, xs, mod, g_pre, w_in_aug, g_q_a, w_q, g_kv_a, w_kv, cos_t, sin_t):
    tm, d = rows.tm, D_MODEL
    const = lambda i: (0, 0)
    row = lambda i: (i, 0)
    one = pl.Buffered(1)
    out_w = (DIFF_WIDTH, DIFF_WIDTH, DIFF_WIDTH, MLA_HEADS * MLA_QK_PAD, MLA_HEADS * MLA_QK_PAD, D_MLA_OUT)
    return pl.pallas_call(
        functools.partial(_inproj_kernel, rows.n_p),
        grid=(rows.n,),
        in_specs=[pl.BlockSpec((tm, d), lambda i: (rows.p_idx(i), 0)),
                  pl.BlockSpec((tm, d), lambda i: (rows.s_idx(i), 0)),
                  pl.BlockSpec((1, N_MOD, d), lambda i: (rows.seq(i), 0, 0)),
                  pl.BlockSpec((1, d), const),
                  pl.BlockSpec((d, D_IN_AUG), const, pipeline_mode=one),
                  pl.BlockSpec((1, Q_LORA_RANK), const),
                  pl.BlockSpec((Q_LORA_RANK, MLA_HEADS * MLA_QK_PAD), const, pipeline_mode=one),
                  pl.BlockSpec((1, KV_LORA_RANK), const),
                  pl.BlockSpec((KV_LORA_RANK, MLA_HEADS * MLA_QK_PAD), const, pipeline_mode=one),
                  pl.BlockSpec((tm, LANES), lambda i: (rows.pos(i), 0)),
                  pl.BlockSpec((tm, LANES), lambda i: (rows.pos(i), 0))],
        out_specs=[pl.BlockSpec((tm, w), row) for w in out_w],
        out_shape=[jax.ShapeDtypeStruct((rows.t, w), BF16) for w in out_w],
        compiler_params=pltpu.CompilerParams(dimension_semantics=("arbitrary",),
                                             vmem_limit_bytes=VMEM_LIMIT),
        name="inproj",
    )(xp, xs, mod, g_pre, w_in_aug, g_q_a, w_q, g_kv_a, w_kv, cos_t, sin_t)


def _lane_groups(x, op):
    out = x[:, 0:LANES]
    for g in range(1, x.shape[1] // LANES):
        out = op(out, x[:, g * LANES:(g + 1) * LANES])
    return out


def _aligned(x, m):
    return x if isinstance(x, int) else pl.multiple_of(x, m)


def _chunk(ref, j, tk):
    return ref[pl.ds(_aligned(j * tk, tk), tk), :]


def _score_block(unit, s_view, jb, cpb, tk, mpart):
    score_fn, shift_fn = unit
    for c in range(cpb):
        j = jb * cpb + c
        s = score_fn(j)
        s_view[jb, :, c * tk:(c + 1) * tk] = s
        part = _lane_groups(s, jnp.maximum)
        if shift_fn is not None:
            part = part + shift_fn(j)
        mpart = jnp.maximum(mpart, part)
    return mpart


def _value_block(unit, m, v_ref, s_view, jb, cpb, tk, lpart, acc):
    _, shift_fn = unit
    for c in range(cpb):
        j = jb * cpb + c
        mj = m if shift_fn is None else m - shift_fn(j)
        p = jnp.exp2(s_view[jb, :, c * tk:(c + 1) * tk] - mj)
        lpart = lpart + _lane_groups(p, jnp.add)
        acc = acc + _dot(p.astype(BF16), _chunk(v_ref, j, tk))
    return lpart, acc


def _attn_phase(geom, v_ref, score=None, value=None):
    tq, tk, cpb, nkb = geom
    zeros = jnp.zeros((tq, LANES), F32)

    def body(jb, carry):
        mpart, lpart, acc = carry
        if score is not None:
            mpart = _score_block(score[0], score[1], jb, cpb, tk, mpart)
        if value is not None:
            lpart, acc = _value_block(value[0], value[1], v_ref, value[2], jb, cpb, tk, lpart, acc)
        return mpart, lpart, acc

    mpart, lpart, acc = lax.fori_loop(0, nkb, body, (jnp.full((tq, LANES), -jnp.inf, F32), zeros, zeros),
                                      unroll=min(ATTN_UNROLL, nkb))
    m = None if score is None else jnp.max(mpart, axis=-1, keepdims=True)
    out = None if value is None else acc / jnp.sum(lpart, axis=-1, keepdims=True)
    return m, out


def _diff_attn_kernel(lambda_init, geom, n_sub, slopes_ref, lam_ref, gsub_ref, q_ref, k_ref, v_ref,
                      o_ref, s_buf, bias_buf):
    tq, tk, _, _ = geom
    h = pl.program_id(1)
    qo = pl.program_id(2)
    slope = slopes_ref[h] * LOG2E
    n_cross = tk // tq

    r_minus_c = (lax.broadcasted_iota(jnp.int32, (tq, tk), 0)
                 - lax.broadcasted_iota(jnp.int32, (tq, tk), 1)).astype(F32)
    bias_buf[0] = r_minus_c * (-slope)
    bias_buf[1] = r_minus_c * slope
    for c in range(n_cross):
        bias_buf[2 + c] = jnp.abs(r_minus_c + float(c * tq)) * (-slope)

    lp = lam_ref[...]
    lam = (jnp.exp(jnp.sum(lp[0:1] * lp[1:2], axis=-1, keepdims=True))
           - jnp.exp(jnp.sum(lp[2:3] * lp[3:4], axis=-1, keepdims=True)) + lambda_init)
    gsub = gsub_ref[...]

    def unit(sub, half):
        qbase = (qo * n_sub + sub) * tq
        jc = qbase // tk
        cross = 2 + (qbase - jc * tk) // tq
        q = q_ref[pl.ds(_aligned(sub * tq, tq), tq), :]
        lane = lax.broadcasted_iota(jnp.int32, q.shape, 1)
        keep = (lane < DIFF_HEAD_DIM) if half == 0 else (lane >= DIFF_HEAD_DIM)
        qh = jnp.where(keep, q, jnp.zeros_like(q))
        shift = lambda j: jnp.where(j == jc, 0.0, -slope * jnp.abs(qbase - j * tk).astype(F32))
        which = lambda j: jnp.where(j < jc, 0, jnp.where(j > jc, 1, cross))
        return (lambda j: _dot_nt(qh, _chunk(k_ref, j, tk)) + bias_buf[which(j)]), shift

    buf0, buf1 = s_buf.at[0], s_buf.at[1]

    def step(sub, m0, has_next):
        u0, u1 = unit(sub, 0), unit(sub, 1)
        m1, o0 = _attn_phase(geom, v_ref, score=(u1, buf1), value=(u0, m0, buf0))
        nxt = (unit(sub + 1, 0), buf0) if has_next else None
        m0_next, o1 = _attn_phase(geom, v_ref, score=nxt, value=(u1, m1, buf1))
        o = o0 - lam * o1
        o_ref[pl.ds(_aligned(sub * tq, tq), tq), :] = (_rms(o, gsub) * (1.0 - lambda_init)).astype(BF16)
        return m0_next

    m0, _ = _attn_phase(geom, v_ref, score=(unit(0, 0), buf0))
    m0 = lax.fori_loop(0, n_sub - 1, lambda sub, m: step(sub, m, True), m0)
    step(n_sub - 1, m0, False)


def _attn_tiles(s, row0):
    tq = min(ATTN_TQ, s)
    tk = min(ATTN_TK, s)
    tkb = min(ATTN_TKB, s)
    tq_outer = min(ATTN_TQ_OUTER, s)
    assert s % tq_outer == 0 and tq_outer % tq == 0 and s % tkb == 0 and tkb % tk == 0 and tk % tq == 0
    assert row0 % s == 0
    return (tq, tk, tkb // tk, s // tkb), tq_outer, (2, s // tkb, tq, tkb)


def _diff_attn(dq, dk, dv, slopes, lam_p, g_sub, row0, b, s, lambda_init):
    geom, tq_outer, s_buf_shape = _attn_tiles(s, row0)
    tq, tk = geom[0], geom[1]
    n_sub, nqo = tq_outer // tq, s // tq_outer
    dh = 2 * DIFF_HEAD_DIM
    return pl.pallas_call(
        functools.partial(_diff_attn_kernel, lambda_init, geom, n_sub),
        grid=(b, DIFF_HEADS, nqo),
        in_specs=[pl.BlockSpec(memory_space=pltpu.SMEM),
                  pl.BlockSpec((8, LANES), lambda bi, h, qi: (0, 0)),
                  pl.BlockSpec((1, dh), lambda bi, h, qi: (0, 0)),
                  pl.BlockSpec((tq_outer, dh), lambda bi, h, qi: (row0 // tq_outer + bi * nqo + qi, h)),
                  pl.BlockSpec((s, dh), lambda bi, h, qi: (row0 // s + bi, h)),
                  pl.BlockSpec((s, dh), lambda bi, h, qi: (row0 // s + bi, h))],
        out_specs=pl.BlockSpec((tq_outer, dh), lambda bi, h, qi: (bi * nqo + qi, h)),
        out_shape=jax.ShapeDtypeStruct((b * s, DIFF_WIDTH), BF16),
        scratch_shapes=[pltpu.VMEM(s_buf_shape, F32), pltpu.VMEM((2 + tk // tq, tq, tk), F32)],
        compiler_params=pltpu.CompilerParams(
            dimension_semantics=("arbitrary", "arbitrary", "arbitrary"), vmem_limit_bytes=VMEM_LIMIT),
        name="diff_attn",
    )(slopes, lam_p, g_sub, dq, dk, dv)


def _mla_attn_kernel(geom, n_sub, q_ref, k_ref, v_ref, o_ref, s_buf):
    tq, tk, _, _ = geom

    def unit(sub):
        q = q_ref[pl.ds(_aligned(sub * tq, tq), tq), :]
        return (lambda j: _dot_nt(q, _chunk(k_ref, j, tk))), None

    def store(sub, o):
        o_ref[pl.ds(_aligned(sub * tq, tq), tq), :] = o.astype(BF16)

    def step(sub, m):
        slot = sub % 2
        m_next, o = _attn_phase(geom, v_ref, score=(unit(sub + 1), s_buf.at[1 - slot]),
                                value=(unit(sub), m, s_buf.at[slot]))
        store(sub, o)
        return m_next

    def unpipelined(sub, carry):
        m, _ = _attn_phase(geom, v_ref, score=(unit(sub), s_buf.at[0]))
        _, o = _attn_phase(geom, v_ref, value=(unit(sub), m, s_buf.at[0]))
        store(sub, o)
        return carry

    if geom[3] == 1:
        lax.fori_loop(0, n_sub, unpipelined, 0)
        return
    m, _ = _attn_phase(geom, v_ref, score=(unit(0), s_buf.at[0]))
    m = lax.fori_loop(0, n_sub - 1, step, m)
    last = n_sub - 1
    _, o = _attn_phase(geom, v_ref, value=(unit(last), m, s_buf.at[last % 2]))
    store(last, o)


def _mla_attn(qm, km, vm, row0, b, s):
    geom, tq_outer, s_buf_shape = _attn_tiles(s, row0)
    n_sub, nqo = tq_outer // geom[0], s // tq_outer
    return pl.pallas_call(
        functools.partial(_mla_attn_kernel, geom, n_sub),
        grid=(b, MLA_HEADS, nqo),
        in_specs=[pl.BlockSpec((tq_outer, MLA_QK_PAD), lambda bi, h, qi: (row0 // tq_outer + bi * nqo + qi, h)),
                  pl.BlockSpec((s, MLA_QK_PAD), lambda bi, h, qi: (row0 // s + bi, h)),
                  pl.BlockSpec((s, MLA_V_DIM), lambda bi, h, qi: (row0 // s + bi, h))],
        out_specs=pl.BlockSpec((tq_outer, MLA_V_DIM), lambda bi, h, qi: (bi * nqo + qi, h)),
        out_shape=jax.ShapeDtypeStruct((b * s, D_MLA_OUT), BF16),
        scratch_shapes=[pltpu.VMEM(s_buf_shape, F32)],
        compiler_params=pltpu.CompilerParams(
            dimension_semantics=("arbitrary", "arbitrary", "arbitrary"), vmem_limit_bytes=VMEM_LIMIT),
        name="mla_attn",
    )(qm, km, vm)


def _route(logits):
    lane = lax.broadcasted_iota(jnp.int32, logits.shape, 1)
    neg = jnp.full(logits.shape, -jnp.inf, F32)
    big = jnp.full(logits.shape, LANES, jnp.int32)
    first = lambda mask: jnp.min(jnp.where(mask, lane, big), axis=-1, keepdims=True)

    gl = jnp.where(lane < N_GROUPS, logits, neg)
    gmax = jnp.max(gl, axis=-1, keepdims=True)
    g_idx = first(gl == gmax)
    g_w = 1.0 / jnp.sum(jnp.exp(gl - gmax), axis=-1, keepdims=True)

    lo = ROUTER_LANE0 + EXPERTS_PER_GROUP * g_idx
    el = jnp.where(lane >= lo, jnp.where(lane < lo + EXPERTS_PER_GROUP, logits, neg), neg)
    v1 = jnp.max(el, axis=-1, keepdims=True)
    i1 = first(el == v1)
    el2 = jnp.where(lane == i1, neg, el)
    v2 = jnp.max(el2, axis=-1, keepdims=True)
    i2 = first(el2 == v2)
    t = jnp.exp(v2 - v1)
    w1 = g_w / (1.0 + t)
    w2 = w1 * t
    first_low = i1 < i2
    ea = jnp.minimum(i1, i2) - lo
    eb = jnp.maximum(i1, i2) - lo
    pair = jnp.where(ea == 0, 0, jnp.where(ea == 1, 3, 5)) + eb - ea - 1
    bucket = g_idx * N_PAIRS + pair
    return bucket, jnp.where(first_low, w1, w2), jnp.where(first_low, w2, w1)


def _oproj_kernel(n_p, oap_ref, oas_ref, obp_ref, obs_ref, xp_ref, xs_ref, mod_ref, wo_ref, gpost_ref,
                  gpre_ref, wr_ref, br_ref, xmid_ref, hx_ref, meta_ref, counts_ref, cnt):
    i = pl.program_id(0)

    @pl.when(i == 0)
    def _():
        cnt[...] = jnp.zeros(cnt.shape, F32)

    is_p = i < n_p
    oa = jnp.where(is_p, oap_ref[...], oas_ref[...])
    ob = jnp.where(is_p, obp_ref[...], obs_ref[...])
    x = jnp.where(is_p, xp_ref[...], xs_ref[...])
    mix = _dot(oa, wo_ref[0:DIFF_WIDTH, :]) + _dot(ob, wo_ref[DIFF_WIDTH:DIFF_WIDTH + D_MLA_OUT, :])
    x_mid = x + mod_ref[0, 2:3, :] * _rms(mix, gpost_ref[...])
    xmid_ref[...] = x_mid
    h2 = _rms(x_mid, gpre_ref[...]) * (1.0 + mod_ref[0, 4:5, :]) + mod_ref[0, 3:4, :]
    h_hi = h2.astype(BF16)
    h_lo = (h2 - h_hi.astype(F32)).astype(BF16)
    w_r = wr_ref[...]
    w_hi = w_r.astype(BF16)
    w_lo = (w_r - w_hi.astype(F32)).astype(BF16)
    logits = _dot(h_hi, w_hi) + _dot(h_hi, w_lo) + _dot(h_lo, w_hi) + br_ref[...]
    bucket, wa, wb = _route(logits)

    lane = lax.broadcasted_iota(jnp.int32, logits.shape, 1)
    zero = jnp.zeros_like(logits)
    hx_ref[:, 0:D_MODEL] = h2
    hx_ref[:, D_MODEL:HX_WIDTH] = jnp.where(lane == 0, wa, jnp.where(lane == 1, wb, zero))

    tm = logits.shape[0]
    onehot = jnp.where(lane == bucket, 1.0, 0.0)
    earlier = (lax.broadcasted_iota(jnp.int32, (tm, tm), 0) > lax.broadcasted_iota(jnp.int32, (tm, tm), 1))
    before = _dot(jnp.where(earlier, 1.0, 0.0).astype(BF16), onehot.astype(BF16))
    rank = jnp.sum(onehot * (before + cnt[0:1, :]), axis=-1, keepdims=True)
    cnt[...] = cnt[...] + jnp.sum(onehot, axis=0, keepdims=True)
    counts_ref[...] = cnt[...]
    meta_ref[...] = jnp.where(lane == 0, bucket, jnp.where(lane == 1, rank.astype(jnp.int32), 0))


def _oproj(rows, oa_p, oa_s, ob_p, ob_s, xp, xs, mod, w_o, g_post, g_pre, w_r, b_r):
    tm, d = rows.tm, D_MODEL
    const = lambda i: (0, 0)
    row = lambda i: (i, 0)
    prow = lambda i: (rows.p_idx(i), 0)
    srow = lambda i: (rows.s_idx(i), 0)
    return pl.pallas_call(
        functools.partial(_oproj_kernel, rows.n_p),
        grid=(rows.n,),
        in_specs=[pl.BlockSpec((tm, DIFF_WIDTH), prow), pl.BlockSpec((tm, DIFF_WIDTH), srow),
                  pl.BlockSpec((tm, D_MLA_OUT), prow), pl.BlockSpec((tm, D_MLA_OUT), srow),
                  pl.BlockSpec((tm, d), prow), pl.BlockSpec((tm, d), srow),
                  pl.BlockSpec((1, N_MOD, d), lambda i: (rows.seq(i), 0, 0)),
                  pl.BlockSpec((DIFF_WIDTH + D_MLA_OUT, d), const, pipeline_mode=pl.Buffered(1)),
                  pl.BlockSpec((1, d), const), pl.BlockSpec((1, d), const),
                  pl.BlockSpec((d, LANES), const), pl.BlockSpec((1, LANES), const)],
        out_specs=[pl.BlockSpec((tm, d), row), pl.BlockSpec((tm, HX_WIDTH), row),
                   pl.BlockSpec((tm, LANES), row), pl.BlockSpec((8, LANES), const)],
        out_shape=[jax.ShapeDtypeStruct((rows.t, d), F32), jax.ShapeDtypeStruct((rows.t, HX_WIDTH), F32),
                   jax.ShapeDtypeStruct((rows.t, LANES), jnp.int32), jax.ShapeDtypeStruct((8, LANES), F32)],
        scratch_shapes=[pltpu.VMEM((8, LANES), F32)],
        compiler_params=pltpu.CompilerParams(dimension_semantics=("arbitrary",),
                                             vmem_limit_bytes=VMEM_LIMIT),
        name="oproj",
    )(oa_p, oa_s, ob_p, ob_s, xp, xs, mod, w_o, g_post, g_pre, w_r, b_r)


def _gather_rows(idx_ref, src_hbm, dst, sem, inline=False):
    def body(r, carry):
        pltpu.make_async_copy(src_hbm.at[pl.ds(idx_ref[0, 0, r], 1), :], dst.at[pl.ds(r, 1), :], sem).start()
        return carry

    if inline:
        for r in range(dst.shape[0]):
            body(r, 0)
    else:
        lax.fori_loop(0, dst.shape[0], body, 0, unroll=8)


def _wait_rows(src_hbm, dst, sem):
    pltpu.make_async_copy(src_hbm.at[pl.ds(0, dst.shape[0]), :], dst, sem).wait()


def _experts_kernel(ea_ref, eb_ref, nv_ref, inv_cur, inv_nxt, hx_hbm, wga, wua, wda, wgb, wub, wdb,
                    f_ref, xbuf, sem):
    i = pl.program_id(0)
    n_valid = nv_ref[0]
    slot = i % 2

    @pl.when(i == 0)
    def _():
        _gather_rows(inv_cur, hx_hbm, xbuf.at[0], sem.at[0])

    def compute():
        _wait_rows(hx_hbm, xbuf.at[slot], sem.at[slot])
        x = xbuf[slot, :, 0:D_MODEL].astype(BF16)
        aux = xbuf[slot, :, D_MODEL:HX_WIDTH]

        def hidden(wg, wu, w):
            g = _dot(x, wg[0])
            return ((g / (1.0 + jnp.exp(-g))) * _dot(x, wu[0]) * w).astype(BF16)

        f_ref[...] = (_dot(hidden(wga, wua, aux[:, 0:1]), wda[0])
                      + _dot(hidden(wgb, wub, aux[:, 1:2]), wdb[0]))

    @pl.when(i + 1 < n_valid)
    def _():
        _gather_rows(inv_nxt, hx_hbm, xbuf.at[1 - slot], sem.at[1 - slot], inline=True)
        compute()

    @pl.when(i + 1 == n_valid)
    def _():
        compute()

    @pl.when(i >= n_valid)
    def _():
        f_ref[...] = jnp.zeros(f_ref.shape, F32)


def _experts(hx, inv3, tile_ea, tile_eb, n_valid, w_gate, w_up, w_down):
    n_tiles, _, tm = inv3.shape
    d = D_MODEL
    wspec = lambda shape, which: pl.BlockSpec(
        (1,) + shape, (lambda i, ea, eb, nv: (ea[i], 0, 0)) if which == 0 else (lambda i, ea, eb, nv: (eb[i], 0, 0)))
    smem = lambda imap: pl.BlockSpec((1, 1, tm), imap, memory_space=pltpu.SMEM)
    return pl.pallas_call(
        _experts_kernel,
        grid_spec=pltpu.PrefetchScalarGridSpec(
            num_scalar_prefetch=3,
            grid=(n_tiles,),
            in_specs=[smem(lambda i, ea, eb, nv: (i, 0, 0)),
                      smem(lambda i, ea, eb, nv: (jnp.minimum(i + 1, n_tiles - 1), 0, 0)),
                      pl.BlockSpec(memory_space=pl.ANY),
                      wspec((d, D_EXPERT), 0), wspec((d, D_EXPERT), 0), wspec((D_EXPERT, d), 0),
                      wspec((d, D_EXPERT), 1), wspec((d, D_EXPERT), 1), wspec((D_EXPERT, d), 1)],
            out_specs=pl.BlockSpec((tm, d), lambda i, ea, eb, nv: (i, 0)),
            scratch_shapes=[pltpu.VMEM((2, tm, HX_WIDTH), F32), pltpu.SemaphoreType.DMA((2,))]),
        out_shape=jax.ShapeDtypeStruct((n_tiles * tm, d), F32),
        compiler_params=pltpu.CompilerParams(dimension_semantics=("arbitrary",),
                                             vmem_limit_bytes=VMEM_LIMIT),
        name="experts",
    )(tile_ea, tile_eb, n_valid, inv3, inv3, hx, w_gate, w_up, w_down, w_gate, w_up, w_down)


def _final_kernel(n_p, pos_cur, pos_nxt, f_hbm, xmid_ref, mod_ref, gpost_ref, yp_ref, ys_ref, fbuf, sem):
    i = pl.program_id(0)
    slot = i % 2

    @pl.when(i == 0)
    def _():
        _gather_rows(pos_cur, f_hbm, fbuf.at[0], sem.at[0])

    def compute():
        _wait_rows(f_hbm, fbuf.at[slot], sem.at[slot])
        y = xmid_ref[...] + mod_ref[0, 5:6, :] * _rms(fbuf[slot], gpost_ref[...])

        @pl.when(i < n_p)
        def _():
            yp_ref[...] = y

        @pl.when(i >= n_p)
        def _():
            ys_ref[...] = y

    @pl.when(i + 1 < pl.num_programs(0))
    def _():
        _gather_rows(pos_nxt, f_hbm, fbuf.at[1 - slot], sem.at[1 - slot], inline=True)
        compute()

    @pl.when(i + 1 == pl.num_programs(0))
    def _():
        compute()


def _final(rows, pos3, f_sorted, x_mid, mod, g_post):
    tm, d = rows.tm, D_MODEL
    smem = lambda imap: pl.BlockSpec((1, 1, tm), imap, memory_space=pltpu.SMEM)
    return pl.pallas_call(
        functools.partial(_final_kernel, rows.n_p),
        grid=(rows.n,),
        in_specs=[smem(lambda i: (i, 0, 0)),
                  smem(lambda i: (jnp.minimum(i + 1, rows.n - 1), 0, 0)),
                  pl.BlockSpec(memory_space=pl.ANY),
                  pl.BlockSpec((tm, d), lambda i: (i, 0)),
                  pl.BlockSpec((1, N_MOD, d), lambda i: (rows.seq(i), 0, 0)),
                  pl.BlockSpec((1, d), lambda i: (0, 0))],
        out_specs=[pl.BlockSpec((tm, d), lambda i: (rows.p_idx(i), 0)),
                   pl.BlockSpec((tm, d), lambda i: (rows.s_idx(i), 0))],
        out_shape=[jax.ShapeDtypeStruct((rows.tp, d), F32), jax.ShapeDtypeStruct((rows.ts, d), F32)],
        scratch_shapes=[pltpu.VMEM((2, tm, d), F32), pltpu.SemaphoreType.DMA((2,))],
        compiler_params=pltpu.CompilerParams(dimension_semantics=("arbitrary",),
                                             vmem_limit_bytes=VMEM_LIMIT),
        name="final",
    )(pos3, pos3, f_sorted, x_mid, mod, g_post)


def _moe_plan(meta, counts, tm):
    t = meta.shape[0]
    n_tiles = t // tm + N_BUCKETS
    cnt = counts[0, :N_BUCKETS].astype(jnp.int32)
    padded = (cnt + tm - 1) // tm * tm
    ends = jnp.cumsum(padded)
    buckets = jnp.arange(N_BUCKETS, dtype=jnp.int32)
    start_of = jnp.sum(jnp.where(meta[:, 0:1] == buckets[None, :], (ends - padded)[None, :], 0), axis=1)
    pos = start_of + meta[:, 1]
    inv = jnp.zeros((n_tiles * tm,), jnp.int32).at[pos].set(jnp.arange(t, dtype=jnp.int32))
    n_valid = ends[-1] // tm
    tile = jnp.minimum(jnp.arange(n_tiles, dtype=jnp.int32), n_valid - 1)
    bucket = jnp.sum((tile[:, None] * tm >= ends[None, :]).astype(jnp.int32), axis=1)
    group, pair = bucket // N_PAIRS, bucket % N_PAIRS
    ea = jnp.array([0, 0, 0, 1, 1, 2], jnp.int32)[pair]
    eb = jnp.array([1, 2, 3, 2, 3, 3], jnp.int32)[pair]
    base = group * EXPERTS_PER_GROUP
    return pos, inv, base + ea, base + eb, n_valid.reshape(1).astype(jnp.int32)


def _rotate_half_cols(w):
    half = MLA_ROPE_DIM // 2
    return jnp.concatenate([-w[..., half:], w[..., :half]], axis=-1)


def _rope_tables(s_max):
    half = MLA_ROPE_DIM // 2
    inv = ROPE_BASE ** (-jnp.arange(half, dtype=F32) / half)
    ang = jnp.arange(s_max, dtype=F32)[:, None] * inv[None, :]
    pad = jnp.zeros((s_max, LANES - MLA_ROPE_DIM), F32)
    cos, sin = jnp.cos(ang), jnp.sin(ang)
    return (jnp.concatenate([cos, cos, pad], axis=-1), jnp.concatenate([sin, sin, pad], axis=-1))


def _layer(x_prompt, x_sample, c_prompt, c_sample, layer_idx, w_ada, b_ada, g_pre_mix, g_post_mix, w_in,
           lambda_q1, lambda_k1, lambda_q2, lambda_k2, g_diff_sub, g_q_a, w_uq, g_kv_a, w_ukv, w_o,
           g_pre_ffn, g_post_ffn, w_router_group, b_router_group, w_router_expert, b_router_expert,
           w_gate, w_up, w_down, tm=256):
    bp, sp, d = x_prompt.shape
    bs, ss, _ = x_sample.shape
    rows = _Rows(bp, sp, bs, ss, tm)
    xp = x_prompt.reshape(bp * sp, d)
    xs = x_sample.reshape(bs * ss, d)
    lambda_init = 0.8 - 0.6 * math.exp(-0.3 * layer_idx)

    nb = bp + bs
    nb_pad = -(-nb // 8) * 8
    c_all = jnp.concatenate([c_prompt, c_sample, jnp.zeros((nb_pad - nb, d), F32)], axis=0)
    mod = _ada(c_all, w_ada, b_ada.reshape(1, -1)).reshape(nb_pad, N_MOD, d)

    w_in_aug = jnp.concatenate([w_in, _rotate_half_cols(w_in[:, C_KR:C_KR + MLA_ROPE_DIM])], axis=1).astype(BF16)
    wq3 = w_uq.reshape(Q_LORA_RANK, MLA_HEADS, MLA_NOPE_DIM + MLA_ROPE_DIM)
    wq_rope = wq3[..., MLA_NOPE_DIM:]
    w_q = jnp.concatenate([wq3[..., :MLA_NOPE_DIM], wq_rope, _rotate_half_cols(wq_rope)], axis=-1)
    w_q = w_q.reshape(Q_LORA_RANK, MLA_HEADS * MLA_QK_PAD).astype(BF16)
    w_kv = w_ukv.astype(BF16)
    cos_t, sin_t = _rope_tables(max(sp, ss))

    dq, dk, dv, qm, km, vm = _inproj(rows, xp, xs, mod, g_pre_mix.reshape(1, d), w_in_aug,
                                     g_q_a.reshape(1, -1), w_q, g_kv_a.reshape(1, -1), w_kv, cos_t, sin_t)

    slopes = jnp.array([2.0 ** (-8.0 * (i + 1) / DIFF_HEADS) for i in range(DIFF_HEADS)], dtype=F32)
    lam_p = jnp.zeros((8, LANES), F32).at[0:4, 0:DIFF_HEAD_DIM].set(
        jnp.stack([lambda_q1, lambda_k1, lambda_q2, lambda_k2]))
    g_sub = g_diff_sub.reshape(1, -1)
    oa_p = _diff_attn(dq, dk, dv, slopes, lam_p, g_sub, 0, bp, sp, lambda_init)
    oa_s = _diff_attn(dq, dk, dv, slopes, lam_p, g_sub, rows.tp, bs, ss, lambda_init)
    ob_p = _mla_attn(qm, km, vm, 0, bp, sp)
    ob_s = _mla_attn(qm, km, vm, rows.tp, bs, ss)

    w_r = jnp.concatenate([w_router_group, w_router_expert.reshape(d, N_EXPERTS),
                           jnp.zeros((d, LANES - N_GROUPS - N_EXPERTS), F32)], axis=1)
    b_r = jnp.concatenate([b_router_group, b_router_expert.reshape(N_EXPERTS),
                           jnp.zeros((LANES - N_GROUPS - N_EXPERTS,), F32)]).reshape(1, LANES)
    x_mid, hx, meta, counts = _oproj(rows, oa_p, oa_s, ob_p, ob_s, xp, xs, mod, w_o.astype(BF16),
                                     g_post_mix.reshape(1, d), g_pre_ffn.reshape(1, d), w_r, b_r)

    pos, inv, tile_ea, tile_eb, n_valid = _moe_plan(meta, counts, MOE_TM)
    f_sorted = _experts(hx, inv.reshape(-1, 1, MOE_TM), tile_ea, tile_eb, n_valid,
                        w_gate.astype(BF16), w_up.astype(BF16), w_down.astype(BF16))
    yp, ys = _final(rows, pos.reshape(rows.n, 1, tm), f_sorted, x_mid, mod, g_post_ffn.reshape(1, d))
    return yp.reshape(bp, sp, d), ys.reshape(bs, ss, d)


def kernel(x_prompt, x_sample, c_prompt, c_sample, w_ada, b_ada, g_pre_mix, g_post_mix, w_in, lambda_q1,
           lambda_k1, lambda_q2, lambda_k2, g_diff_sub, g_q_a, w_uq, g_kv_a, w_ukv, w_o, g_pre_ffn,
           g_post_ffn, w_router_group, b_router_group, w_router_expert, b_router_expert, w_gate, w_up,
           w_down):
    for l in range(w_ada.shape[0]):
        x_prompt, x_sample = _layer(
            x_prompt, x_sample, c_prompt, c_sample, l, w_ada[l], b_ada[l], g_pre_mix[l], g_post_mix[l],
            w_in[l], lambda_q1[l], lambda_k1[l], lambda_q2[l], lambda_k2[l], g_diff_sub[l], g_q_a[l],
            w_uq[l], g_kv_a[l], w_ukv[l], w_o[l], g_pre_ffn[l], g_post_ffn[l], w_router_group[l],
            b_router_group[l], w_router_expert[l], b_router_expert[l], w_gate[l], w_up[l], w_down[l])
    return x_prompt, x_sample
```

```python
import functools
import math

import jax
import jax.numpy as jnp
from jax import lax
from jax.experimental import pallas as pl
from jax.experimental.pallas import tpu as pltpu

F32 = jnp.float32
BF16 = jnp.bfloat16

D_MODEL = 2048
DIFF_HEADS = 8
DIFF_HEAD_DIM = 64
DIFF_WIDTH = DIFF_HEADS * 2 * DIFF_HEAD_DIM
MLA_HEADS = 8
MLA_NOPE_DIM = 128
MLA_ROPE_DIM = 64
MLA_V_DIM = 128
MLA_QK_PAD = 256
Q_LORA_RANK = 512
KV_LORA_RANK = 256
ROPE_BASE = 10000.0
D_MLA_OUT = MLA_HEADS * MLA_V_DIM
N_GROUPS = 4
EXPERTS_PER_GROUP = 4
N_EXPERTS = N_GROUPS * EXPERTS_PER_GROUP
D_EXPERT = 512
N_MOD = 6
NORM_EPS = 1e-6
LANES = 128
ROUTER_LANE0 = N_GROUPS
N_PAIRS = 6
N_BUCKETS = N_GROUPS * N_PAIRS
HX_WIDTH = D_MODEL + LANES
MOE_TM = 256
OPROJ_SPLIT = 1

C_DQ, C_DK, C_DV = 0, DIFF_WIDTH, 2 * DIFF_WIDTH
C_CQ = 3 * DIFF_WIDTH
C_CKV = C_CQ + Q_LORA_RANK
C_KR = C_CKV + KV_LORA_RANK
D_IN_PROJ = C_KR + MLA_ROPE_DIM

VMEM_LIMIT = 56 * 1024 * 1024
LOG2E = 1.4426950408889634
ATTN_TQ = 256
ATTN_TK = 512
ATTN_TQ_OUTER = 1024
ATTN_TKB = 2048
ATTN_UNROLL = 2


def _rms(x, g):
    return x * lax.rsqrt(jnp.mean(x * x, axis=-1, keepdims=True) + NORM_EPS) * g


def _dot(a, b):
    return jnp.dot(a, b, preferred_element_type=F32)


def _dot_nt(a, b):
    return lax.dot_general(a, b, (((1,), (1,)), ((), ())), preferred_element_type=F32)


class _Rows:
    def __init__(self, bp, sp, bs, ss, tm):
        assert sp % tm == 0 and ss % tm == 0
        self.bp, self.sp, self.bs, self.ss, self.tm = bp, sp, bs, ss, tm
        self.n_p = bp * sp // tm
        self.n_s = bs * ss // tm
        self.n = self.n_p + self.n_s
        self.tp = bp * sp
        self.ts = bs * ss
        self.t = self.tp + self.ts

    def p_idx(self, i):
        return jnp.minimum(i, self.n_p - 1)

    def s_idx(self, i):
        return jnp.maximum(i - self.n_p, 0)

    def seq(self, i):
        return jnp.where(i < self.n_p, i // (self.sp // self.tm),
                         self.bp + (i - self.n_p) // (self.ss // self.tm))

    def pos(self, i):
        return jnp.where(i < self.n_p, i % (self.sp // self.tm),
                         (i - self.n_p) % (self.ss // self.tm))


def _ada_kernel(c_ref, w_ref, b_ref, o_ref):
    c = c_ref[...]
    a = c / (1.0 + jnp.exp(-c))
    o_ref[...] = jnp.dot(a, w_ref[...], preferred_element_type=F32,
                         precision=lax.Precision.HIGHEST) + b_ref[...]


def _ada(c_all, w_ada, b_ada):
    nb, d = c_all.shape
    n = w_ada.shape[1]
    tn = 1024
    return pl.pallas_call(
        _ada_kernel,
        grid=(n // tn,),
        in_specs=[pl.BlockSpec((nb, d), lambda j: (0, 0)),
                  pl.BlockSpec((d, tn), lambda j: (0, j)),
                  pl.BlockSpec((1, tn), lambda j: (0, j))],
        out_specs=pl.BlockSpec((nb, tn), lambda j: (0, j)),
        out_shape=jax.ShapeDtypeStruct((nb, n), F32),
        compiler_params=pltpu.CompilerParams(dimension_semantics=("arbitrary",),
                                             vmem_limit_bytes=VMEM_LIMIT),
        name="ada",
    )(c_all, w_ada, b_ada)


def _rope_half(x, cos, sin):
    return x * cos + pltpu.roll(x, MLA_ROPE_DIM, 1) * sin


def _inproj_kernel(n_p, xp_ref, xs_ref, mod_ref, gpre_ref, win_ref, wkr_ref, gqa_ref, wq_ref, gkva_ref,
                   wkv_ref, cos_ref, sin_ref, dq_ref, dk_ref, dv_ref, qm_ref, km_ref, vm_ref):
    i = pl.program_id(0)
    x = jnp.where(i < n_p, xp_ref[...], xs_ref[...])
    shift = mod_ref[0, 0:1, :]
    scale = mod_ref[0, 1:2, :]
    hb = (_rms(x, gpre_ref[...]) * (1.0 + scale) + shift).astype(BF16)

    dq_ref[...] = (_dot(hb, win_ref[:, C_DQ:C_DQ + DIFF_WIDTH]) * (DIFF_HEAD_DIM ** -0.5 * LOG2E)).astype(BF16)
    dk_ref[...] = _dot(hb, win_ref[:, C_DK:C_DK + DIFF_WIDTH]).astype(BF16)
    dv_ref[...] = _dot(hb, win_ref[:, C_DV:C_DV + DIFF_WIDTH]).astype(BF16)

    cos = cos_ref[...]
    sin = sin_ref[...]
    cq = _dot(hb, win_ref[:, C_CQ:C_CQ + Q_LORA_RANK])
    ckv = _dot(hb, win_ref[:, C_CKV:C_CKV + KV_LORA_RANK])
    kr = _rope_half(_dot(hb, wkr_ref[...]), cos, sin).astype(BF16)

    q = _dot(_rms(cq, gqa_ref[...]).astype(BF16), wq_ref[...])
    qscale = (MLA_NOPE_DIM + MLA_ROPE_DIM) ** -0.5 * LOG2E
    for h in range(MLA_HEADS):
        c0 = h * MLA_QK_PAD
        qm_ref[:, c0:c0 + LANES] = (q[:, c0:c0 + LANES] * qscale).astype(BF16)
        qm_ref[:, c0 + LANES:c0 + 2 * LANES] = (
            _rope_half(q[:, c0 + LANES:c0 + 2 * LANES], cos, sin) * qscale).astype(BF16)

    kv = _dot(_rms(ckv, gkva_ref[...]).astype(BF16), wkv_ref[...])
    for h in range(MLA_HEADS):
        c0 = h * MLA_QK_PAD
        km_ref[:, c0:c0 + LANES] = kv[:, c0:c0 + LANES].astype(BF16)
        km_ref[:, c0 + LANES:c0 + 2 * LANES] = kr
        vm_ref[:, h * MLA_V_DIM:(h + 1) * MLA_V_DIM] = kv[:, c0 + LANES:c0 + 2 * LANES].astype(BF16)


def _inproj(rows, xp, xs, mod, g_pre, w_in, w_kr, g_q_a, w_q, g_kv_a, w_kv, cos_t, sin_t):
    tm, d = rows.tm, D_MODEL
    const = lambda i: (0, 0)
    row = lambda i: (i, 0)
    one = pl.Buffered(1)
    out_w = (DIFF_WIDTH, DIFF_WIDTH, DIFF_WIDTH, MLA_HEADS * MLA_QK_PAD, MLA_HEADS * MLA_QK_PAD, D_MLA_OUT)
    return pl.pallas_call(
        functools.partial(_inproj_kernel, rows.n_p),
        grid=(rows.n,),
        in_specs=[pl.BlockSpec((tm, d), lambda i: (rows.p_idx(i), 0)),
                  pl.BlockSpec((tm, d), lambda i: (rows.s_idx(i), 0)),
                  pl.BlockSpec((1, N_MOD, d), lambda i: (rows.seq(i), 0, 0)),
                  pl.BlockSpec((1, d), const),
                  pl.BlockSpec((d, D_IN_PROJ), const, pipeline_mode=one),
                  pl.BlockSpec((d, 2 * MLA_ROPE_DIM), const),
                  pl.BlockSpec((1, Q_LORA_RANK), const),
                  pl.BlockSpec((Q_LORA_RANK, MLA_HEADS * MLA_QK_PAD), const, pipeline_mode=one),
                  pl.BlockSpec((1, KV_LORA_RANK), const),
                  pl.BlockSpec((KV_LORA_RANK, MLA_HEADS * MLA_QK_PAD), const, pipeline_mode=one),
                  pl.BlockSpec((tm, LANES), lambda i: (rows.pos(i), 0)),
                  pl.BlockSpec((tm, LANES), lambda i: (rows.pos(i), 0))],
        out_specs=[pl.BlockSpec((tm, w), row) for w in out_w],
        out_shape=[jax.ShapeDtypeStruct((rows.t, w), BF16) for w in out_w],
        compiler_params=pltpu.CompilerParams(dimension_semantics=("arbitrary",),
                                             vmem_limit_bytes=VMEM_LIMIT),
        name="inproj",
    )(xp, xs, mod, g_pre, w_in, w_kr, g_q_a, w_q, g_kv_a, w_kv, cos_t, sin_t)


def _lane_groups(x, op):
    out = x[:, 0:LANES]
    for g in range(1, x.shape[1] // LANES):
        out = op(out, x[:, g * LANES:(g + 1) * LANES])
    return out


def _aligned(x, m):
    return x if isinstance(x, int) else pl.multiple_of(x, m)


def _chunk(ref, j, tk):
    return ref[pl.ds(_aligned(j * tk, tk), tk), :]


def _score_block(unit, s_view, jb, cpb, tk, mpart):
    score_fn, shift_fn = unit
    for c in range(cpb):
        j = jb * cpb + c
        s = score_fn(j)
        s_view[jb, :, c * tk:(c + 1) * tk] = s
        part = _lane_groups(s, jnp.maximum)
        if shift_fn is not None:
            part = part + shift_fn(j)
        mpart = jnp.maximum(mpart, part)
    return mpart


def _value_block(unit, m, v_ref, s_view, jb, cpb, tk, lpart, acc):
    _, shift_fn = unit
    for c in range(cpb):
        j = jb * cpb + c
        mj = m if shift_fn is None else m - shift_fn(j)
        p = jnp.exp2(s_view[jb, :, c * tk:(c + 1) * tk] - mj)
        lpart = lpart + _lane_groups(p, jnp.add)
        acc = acc + _dot(p.astype(BF16), _chunk(v_ref, j, tk))
    return lpart, acc


def _attn_phase(geom, v_ref, score=None, value=None):
    tq, tk, cpb, nkb = geom
    zeros = jnp.zeros((tq, LANES), F32)

    def body(jb, carry):
        mpart, lpart, acc = carry
        if score is not None:
            mpart = _score_block(score[0], score[1], jb, cpb, tk, mpart)
        if value is not None:
            lpart, acc = _value_block(value[0], value[1], v_ref, value[2], jb, cpb, tk, lpart, acc)
        return mpart, lpart, acc

    mpart, lpart, acc = lax.fori_loop(0, nkb, body, (jnp.full((tq, LANES), -jnp.inf, F32), zeros, zeros),
                                      unroll=min(ATTN_UNROLL, nkb))
    m = None if score is None else jnp.max(mpart, axis=-1, keepdims=True)
    out = None if value is None else acc / jnp.sum(lpart, axis=-1, keepdims=True)
    return m, out


def _diff_attn_kernel(lambda_init, geom, n_sub, slopes_ref, lam_ref, gsub_ref, q_ref, k_ref, v_ref,
                      o_ref, s_buf, bias_buf):
    tq, tk, _, _ = geom
    h = pl.program_id(1)
    qo = pl.program_id(2)
    slope = slopes_ref[h] * LOG2E
    n_cross = tk // tq

    r_minus_c = (lax.broadcasted_iota(jnp.int32, (tq, tk), 0)
                 - lax.broadcasted_iota(jnp.int32, (tq, tk), 1)).astype(F32)
    bias_buf[0] = r_minus_c * (-slope)
    bias_buf[1] = r_minus_c * slope
    for c in range(n_cross):
        bias_buf[2 + c] = jnp.abs(r_minus_c + float(c * tq)) * (-slope)

    lp = lam_ref[...]
    lam = (jnp.exp(jnp.sum(lp[0:1] * lp[1:2], axis=-1, keepdims=True))
           - jnp.exp(jnp.sum(lp[2:3] * lp[3:4], axis=-1, keepdims=True)) + lambda_init)
    gsub = gsub_ref[...]

    def unit(sub, half):
        qbase = (qo * n_sub + sub) * tq
        jc = qbase // tk
        cross = 2 + (qbase - jc * tk) // tq
        q = q_ref[pl.ds(_aligned(sub * tq, tq), tq), :]
        lane = lax.broadcasted_iota(jnp.int32, q.shape, 1)
        keep = (lane < DIFF_HEAD_DIM) if half == 0 else (lane >= DIFF_HEAD_DIM)
        qh = jnp.where(keep, q, jnp.zeros_like(q))
        shift = lambda j: jnp.where(j == jc, 0.0, -slope * jnp.abs(qbase - j * tk).astype(F32))
        which = lambda j: jnp.where(j < jc, 0, jnp.where(j > jc, 1, cross))
        return (lambda j: _dot_nt(qh, _chunk(k_ref, j, tk)) + bias_buf[which(j)]), shift

    buf0, buf1 = s_buf.at[0], s_buf.at[1]

    def step(sub, m0, has_next):
        u0, u1 = unit(sub, 0), unit(sub, 1)
        m1, o0 = _attn_phase(geom, v_ref, score=(u1, buf1), value=(u0, m0, buf0))
        nxt = (unit(sub + 1, 0), buf0) if has_next else None
        m0_next, o1 = _attn_phase(geom, v_ref, score=nxt, value=(u1, m1, buf1))
        o = o0 - lam * o1
        o_ref[pl.ds(_aligned(sub * tq, tq), tq), :] = (_rms(o, gsub) * (1.0 - lambda_init)).astype(BF16)
        return m0_next

    m0, _ = _attn_phase(geom, v_ref, score=(unit(0, 0), buf0))
    m0 = lax.fori_loop(0, n_sub - 1, lambda sub, m: step(sub, m, True), m0)
    step(n_sub - 1, m0, False)


def _attn_tiles(s, row0):
    tq = min(ATTN_TQ, s)
    tk = min(ATTN_TK, s)
    tkb = min(ATTN_TKB, s)
    tq_outer = min(ATTN_TQ_OUTER, s)
    assert s % tq_outer == 0 and tq_outer % tq == 0 and s % tkb == 0 and tkb % tk == 0 and tk % tq == 0
    assert row0 % s == 0
    return (tq, tk, tkb // tk, s // tkb), tq_outer, (2, s // tkb, tq, tkb)


def _diff_attn(dq, dk, dv, slopes, lam_p, g_sub, row0, b, s, lambda_init):
    geom, tq_outer, s_buf_shape = _attn_tiles(s, row0)
    tq, tk = geom[0], geom[1]
    n_sub, nqo = tq_outer // tq, s // tq_outer
    dh = 2 * DIFF_HEAD_DIM
    return pl.pallas_call(
        functools.partial(_diff_attn_kernel, lambda_init, geom, n_sub),
        grid=(b, DIFF_HEADS, nqo),
        in_specs=[pl.BlockSpec(memory_space=pltpu.SMEM),
                  pl.BlockSpec((8, LANES), lambda bi, h, qi: (0, 0)),
                  pl.BlockSpec((1, dh), lambda bi, h, qi: (0, 0)),
                  pl.BlockSpec((tq_outer, dh), lambda bi, h, qi: (row0 // tq_outer + bi * nqo + qi, h)),
                  pl.BlockSpec((s, dh), lambda bi, h, qi: (row0 // s + bi, h)),
                  pl.BlockSpec((s, dh), lambda bi, h, qi: (row0 // s + bi, h))],
        out_specs=pl.BlockSpec((tq_outer, dh), lambda bi, h, qi: (bi * nqo + qi, h)),
        out_shape=jax.ShapeDtypeStruct((b * s, DIFF_WIDTH), BF16),
        scratch_shapes=[pltpu.VMEM(s_buf_shape, F32), pltpu.VMEM((2 + tk // tq, tq, tk), F32)],
        compiler_params=pltpu.CompilerParams(
            dimension_semantics=("arbitrary", "arbitrary", "arbitrary"), vmem_limit_bytes=VMEM_LIMIT),
        name="diff_attn",
    )(slopes, lam_p, g_sub, dq, dk, dv)


def _mla_attn_kernel(geom, n_sub, q_ref, k_ref, v_ref, o_ref, s_buf):
    tq, tk, _, _ = geom

    def unit(sub):
        q = q_ref[pl.ds(_aligned(sub * tq, tq), tq), :]
        return (lambda j: _dot_nt(q, _chunk(k_ref, j, tk))), None

    def store(sub, o):
        o_ref[pl.ds(_aligned(sub * tq, tq), tq), :] = o.astype(BF16)

    def step(sub, m):
        slot = sub % 2
        m_next, o = _attn_phase(geom, v_ref, score=(unit(sub + 1), s_buf.at[1 - slot]),
                                value=(unit(sub), m, s_buf.at[slot]))
        store(sub, o)
        return m_next

    def unpipelined(sub, carry):
        m, _ = _attn_phase(geom, v_ref, score=(unit(sub), s_buf.at[0]))
        _, o = _attn_phase(geom, v_ref, value=(unit(sub), m, s_buf.at[0]))
        store(sub, o)
        return carry

    if geom[3] == 1:
        lax.fori_loop(0, n_sub, unpipelined, 0)
        return
    m, _ = _attn_phase(geom, v_ref, score=(unit(0), s_buf.at[0]))
    m = lax.fori_loop(0, n_sub - 1, step, m)
    last = n_sub - 1
    _, o = _attn_phase(geom, v_ref, value=(unit(last), m, s_buf.at[last % 2]))
    store(last, o)


def _mla_attn(qm, km, vm, row0, b, s):
    geom, tq_outer, s_buf_shape = _attn_tiles(s, row0)
    n_sub, nqo = tq_outer // geom[0], s // tq_outer
    return pl.pallas_call(
        functools.partial(_mla_attn_kernel, geom, n_sub),
        grid=(b, MLA_HEADS, nqo),
        in_specs=[pl.BlockSpec((tq_outer, MLA_QK_PAD), lambda bi, h, qi: (row0 // tq_outer + bi * nqo + qi, h)),
                  pl.BlockSpec((s, MLA_QK_PAD), lambda bi, h, qi: (row0 // s + bi, h)),
                  pl.BlockSpec((s, MLA_V_DIM), lambda bi, h, qi: (row0 // s + bi, h))],
        out_specs=pl.BlockSpec((tq_outer, MLA_V_DIM), lambda bi, h, qi: (bi * nqo + qi, h)),
        out_shape=jax.ShapeDtypeStruct((b * s, D_MLA_OUT), BF16),
        scratch_shapes=[pltpu.VMEM(s_buf_shape, F32)],
        compiler_params=pltpu.CompilerParams(
            dimension_semantics=("arbitrary", "arbitrary", "arbitrary"), vmem_limit_bytes=VMEM_LIMIT),
        name="mla_attn",
    )(qm, km, vm)


def _route(logits):
    lane = lax.broadcasted_iota(jnp.int32, logits.shape, 1)
    neg = jnp.full(logits.shape, -jnp.inf, F32)
    big = jnp.full(logits.shape, LANES, jnp.int32)
    first = lambda mask: jnp.min(jnp.where(mask, lane, big), axis=-1, keepdims=True)

    gl = jnp.where(lane < N_GROUPS, logits, neg)
    gmax = jnp.max(gl, axis=-1, keepdims=True)
    g_idx = first(gl == gmax)
    g_w = 1.0 / jnp.sum(jnp.exp(gl - gmax), axis=-1, keepdims=True)

    lo = ROUTER_LANE0 + EXPERTS_PER_GROUP * g_idx
    el = jnp.where(lane >= lo, jnp.where(lane < lo + EXPERTS_PER_GROUP, logits, neg), neg)
    v1 = jnp.max(el, axis=-1, keepdims=True)
    i1 = first(el == v1)
    el2 = jnp.where(lane == i1, neg, el)
    v2 = jnp.max(el2, axis=-1, keepdims=True)
    i2 = first(el2 == v2)
    t = jnp.exp(v2 - v1)
    w1 = g_w / (1.0 + t)
    w2 = w1 * t
    first_low = i1 < i2
    ea = jnp.minimum(i1, i2) - lo
    eb = jnp.maximum(i1, i2) - lo
    pair = jnp.where(ea == 0, 0, jnp.where(ea == 1, 3, 5)) + eb - ea - 1
    bucket = g_idx * N_PAIRS + pair
    return bucket, jnp.where(first_low, w1, w2), jnp.where(first_low, w2, w1)


def _oproj_kernel(n_p, oap_ref, oas_ref, obp_ref, obs_ref, xp_ref, xs_ref, mod_ref, wo_ref, gpost_ref,
                  gpre_ref, wr_ref, br_ref, xmid_ref, hx_ref, meta_ref, counts_ref, cnt):
    i = pl.program_id(0)

    @pl.when(i == 0)
    def _():
        cnt[...] = jnp.zeros(cnt.shape, F32)

    is_p = i < n_p
    w_r = wr_ref[...]
    w_hi = w_r.astype(BF16)
    w_hi_lo = jnp.concatenate([w_hi, (w_r - w_hi.astype(F32)).astype(BF16)], axis=1)
    tm = xmid_ref.shape[0]
    th = tm // OPROJ_SPLIT
    lane = lax.broadcasted_iota(jnp.int32, (th, LANES), 1)
    earlier = jnp.where(lax.broadcasted_iota(jnp.int32, (th, th), 0) > lax.broadcasted_iota(jnp.int32, (th, th), 1),
                        1.0, 0.0).astype(BF16)

    for part in range(OPROJ_SPLIT):
        r = slice(part * th, (part + 1) * th)
        oa = jnp.where(is_p, oap_ref[r, :], oas_ref[r, :])
        ob = jnp.where(is_p, obp_ref[r, :], obs_ref[r, :])
        x = jnp.where(is_p, xp_ref[r, :], xs_ref[r, :])
        mix = _dot(oa, wo_ref[0:DIFF_WIDTH, :]) + _dot(ob, wo_ref[DIFF_WIDTH:DIFF_WIDTH + D_MLA_OUT, :])
        x_mid = x + mod_ref[0, 2:3, :] * _rms(mix, gpost_ref[...])
        xmid_ref[r, :] = x_mid
        h2 = _rms(x_mid, gpre_ref[...]) * (1.0 + mod_ref[0, 4:5, :]) + mod_ref[0, 3:4, :]
        h_hi = h2.astype(BF16)
        h_lo = (h2 - h_hi.astype(F32)).astype(BF16)
        hi_terms = _dot(h_hi, w_hi_lo)
        logits = hi_terms[:, 0:LANES] + hi_terms[:, LANES:2 * LANES] + _dot(h_lo, w_hi) + br_ref[...]
        bucket, wa, wb = _route(logits)

        hx_ref[r, 0:D_MODEL] = h2
        hx_ref[r, D_MODEL:HX_WIDTH] = jnp.where(lane == 0, wa, jnp.where(lane == 1, wb, jnp.zeros_like(logits)))

        onehot = jnp.where(lane == bucket, 1.0, 0.0)
        before = _dot(earlier, onehot.astype(BF16))
        rank = jnp.sum(onehot * (before + cnt[0:1, :]), axis=-1, keepdims=True)
        cnt[...] = cnt[...] + jnp.sum(onehot, axis=0, keepdims=True)
        meta_ref[r, :] = jnp.where(lane == 0, bucket, jnp.where(lane == 1, rank.astype(jnp.int32), 0))
    counts_ref[...] = cnt[...]


def _oproj(rows, oa_p, oa_s, ob_p, ob_s, xp, xs, mod, w_o, g_post, g_pre, w_r, b_r):
    tm, d = rows.tm, D_MODEL
    const = lambda i: (0, 0)
    row = lambda i: (i, 0)
    prow = lambda i: (rows.p_idx(i), 0)
    srow = lambda i: (rows.s_idx(i), 0)
    return pl.pallas_call(
        functools.partial(_oproj_kernel, rows.n_p),
        grid=(rows.n,),
        in_specs=[pl.BlockSpec((tm, DIFF_WIDTH), prow), pl.BlockSpec((tm, DIFF_WIDTH), srow),
                  pl.BlockSpec((tm, D_MLA_OUT), prow), pl.BlockSpec((tm, D_MLA_OUT), srow),
                  pl.BlockSpec((tm, d), prow), pl.BlockSpec((tm, d), srow),
                  pl.BlockSpec((1, N_MOD, d), lambda i: (rows.seq(i), 0, 0)),
                  pl.BlockSpec((DIFF_WIDTH + D_MLA_OUT, d), const, pipeline_mode=pl.Buffered(1)),
                  pl.BlockSpec((1, d), const), pl.BlockSpec((1, d), const),
                  pl.BlockSpec((d, LANES), const), pl.BlockSpec((1, LANES), const)],
        out_specs=[pl.BlockSpec((tm, d), row), pl.BlockSpec((tm, HX_WIDTH), row),
                   pl.BlockSpec((tm, LANES), row), pl.BlockSpec((8, LANES), const)],
        out_shape=[jax.ShapeDtypeStruct((rows.t, d), F32), jax.ShapeDtypeStruct((rows.t, HX_WIDTH), F32),
                   jax.ShapeDtypeStruct((rows.t, LANES), jnp.int32), jax.ShapeDtypeStruct((8, LANES), F32)],
        scratch_shapes=[pltpu.VMEM((8, LANES), F32)],
        compiler_params=pltpu.CompilerParams(dimension_semantics=("arbitrary",),
                                             vmem_limit_bytes=VMEM_LIMIT),
        name="oproj",
    )(oa_p, oa_s, ob_p, ob_s, xp, xs, mod, w_o, g_post, g_pre, w_r, b_r)


def _gather_rows(idx_ref, src_hbm, dst, sem, inline=False):
    def body(r, carry):
        pltpu.make_async_copy(src_hbm.at[pl.ds(idx_ref[0, 0, r], 1), :], dst.at[pl.ds(r, 1), :], sem).start()
        return carry

    if inline:
        for r in range(dst.shape[0]):
            body(r, 0)
    else:
        lax.fori_loop(0, dst.shape[0], body, 0, unroll=8)


def _wait_rows(src_hbm, dst, sem):
    pltpu.make_async_copy(src_hbm.at[pl.ds(0, dst.shape[0]), :], dst, sem).wait()


def _experts_kernel(ea_ref, eb_ref, nv_ref, inv_cur, inv_nxt, hx_hbm, wga, wua, wda, wgb, wub, wdb,
                    f_ref, xbuf, sem):
    i = pl.program_id(0)
    n_valid = nv_ref[0]
    slot = i % 2

    @pl.when(i == 0)
    def _():
        _gather_rows(inv_cur, hx_hbm, xbuf.at[0], sem.at[0])

    def compute():
        _wait_rows(hx_hbm, xbuf.at[slot], sem.at[slot])
        x = xbuf[slot, :, 0:D_MODEL].astype(BF16)
        aux = xbuf[slot, :, D_MODEL:HX_WIDTH]

        def hidden(wg, wu, w):
            g = _dot(x, wg[0])
            return ((g / (1.0 + jnp.exp(-g))) * _dot(x, wu[0]) * w).astype(BF16)

        f_ref[...] = (_dot(hidden(wga, wua, aux[:, 0:1]), wda[0])
                      + _dot(hidden(wgb, wub, aux[:, 1:2]), wdb[0]))

    @pl.when(i + 1 < n_valid)
    def _():
        _gather_rows(inv_nxt, hx_hbm, xbuf.at[1 - slot], sem.at[1 - slot], inline=True)
        compute()

    @pl.when(i + 1 == n_valid)
    def _():
        compute()

    @pl.when(i >= n_valid)
    def _():
        f_ref[...] = jnp.zeros(f_ref.shape, F32)


def _experts(hx, inv3, tile_ea, tile_eb, n_valid, w_gate, w_up, w_down):
    n_tiles, _, tm = inv3.shape
    d = D_MODEL
    wspec = lambda shape, which: pl.BlockSpec(
        (1,) + shape, (lambda i, ea, eb, nv: (ea[i], 0, 0)) if which == 0 else (lambda i, ea, eb, nv: (eb[i], 0, 0)))
    smem = lambda imap: pl.BlockSpec((1, 1, tm), imap, memory_space=pltpu.SMEM)
    return pl.pallas_call(
        _experts_kernel,
        grid_spec=pltpu.PrefetchScalarGridSpec(
            num_scalar_prefetch=3,
            grid=(n_tiles,),
            in_specs=[smem(lambda i, ea, eb, nv: (i, 0, 0)),
                      smem(lambda i, ea, eb, nv: (jnp.minimum(i + 1, n_tiles - 1), 0, 0)),
                      pl.BlockSpec(memory_space=pl.ANY),
                      wspec((d, D_EXPERT), 0), wspec((d, D_EXPERT), 0), wspec((D_EXPERT, d), 0),
                      wspec((d, D_EXPERT), 1), wspec((d, D_EXPERT), 1), wspec((D_EXPERT, d), 1)],
            out_specs=pl.BlockSpec((tm, d), lambda i, ea, eb, nv: (i, 0)),
            scratch_shapes=[pltpu.VMEM((2, tm, HX_WIDTH), F32), pltpu.SemaphoreType.DMA((2,))]),
        out_shape=jax.ShapeDtypeStruct((n_tiles * tm, d), F32),
        compiler_params=pltpu.CompilerParams(dimension_semantics=("arbitrary",),
                                             vmem_limit_bytes=VMEM_LIMIT),
        name="experts",
    )(tile_ea, tile_eb, n_valid, inv3, inv3, hx, w_gate, w_up, w_down, w_gate, w_up, w_down)


def _final_kernel(n_p, pos_cur, pos_nxt, f_hbm, xmid_ref, mod_ref, gpost_ref, yp_ref, ys_ref, fbuf, sem):
    i = pl.program_id(0)
    slot = i % 2

    @pl.when(i == 0)
    def _():
        _gather_rows(pos_cur, f_hbm, fbuf.at[0], sem.at[0])

    def compute():
        _wait_rows(f_hbm, fbuf.at[slot], sem.at[slot])
        y = xmid_ref[...] + mod_ref[0, 5:6, :] * _rms(fbuf[slot], gpost_ref[...])

        @pl.when(i < n_p)
        def _():
            yp_ref[...] = y

        @pl.when(i >= n_p)
        def _():
            ys_ref[...] = y

    @pl.when(i + 1 < pl.num_programs(0))
    def _():
        _gather_rows(pos_nxt, f_hbm, fbuf.at[1 - slot], sem.at[1 - slot], inline=True)
        compute()

    @pl.when(i + 1 == pl.num_programs(0))
    def _():
        compute()


def _final(rows, pos3, f_sorted, x_mid, mod, g_post):
    tm, d = rows.tm, D_MODEL
    smem = lambda imap: pl.BlockSpec((1, 1, tm), imap, memory_space=pltpu.SMEM)
    return pl.pallas_call(
        functools.partial(_final_kernel, rows.n_p),
        grid=(rows.n,),
        in_specs=[smem(lambda i: (i, 0, 0)),
                  smem(lambda i: (jnp.minimum(i + 1, rows.n - 1), 0, 0)),
                  pl.BlockSpec(memory_space=pl.ANY),
                  pl.BlockSpec((tm, d), lambda i: (i, 0)),
                  pl.BlockSpec((1, N_MOD, d), lambda i: (rows.seq(i), 0, 0)),
                  pl.BlockSpec((1, d), lambda i: (0, 0))],
        out_specs=[pl.BlockSpec((tm, d), lambda i: (rows.p_idx(i), 0)),
                   pl.BlockSpec((tm, d), lambda i: (rows.s_idx(i), 0))],
        out_shape=[jax.ShapeDtypeStruct((rows.tp, d), F32), jax.ShapeDtypeStruct((rows.ts, d), F32)],
        scratch_shapes=[pltpu.VMEM((2, tm, d), F32), pltpu.SemaphoreType.DMA((2,))],
        compiler_params=pltpu.CompilerParams(dimension_semantics=("arbitrary",),
                                             vmem_limit_bytes=VMEM_LIMIT),
        name="final",
    )(pos3, pos3, f_sorted, x_mid, mod, g_post)


def _moe_plan(meta, counts, tm):
    t = meta.shape[0]
    n_tiles = t // tm + N_BUCKETS
    cnt = counts[0, :N_BUCKETS].astype(jnp.int32)
    padded = (cnt + tm - 1) // tm * tm
    ends = jnp.cumsum(padded)
    buckets = jnp.arange(N_BUCKETS, dtype=jnp.int32)
    start_of = jnp.sum(jnp.where(meta[:, 0:1] == buckets[None, :], (ends - padded)[None, :], 0), axis=1)
    pos = start_of + meta[:, 1]
    inv = jnp.zeros((n_tiles * tm,), jnp.int32).at[pos].set(jnp.arange(t, dtype=jnp.int32))
    n_valid = ends[-1] // tm
    tile = jnp.minimum(jnp.arange(n_tiles, dtype=jnp.int32), n_valid - 1)
    bucket = jnp.sum((tile[:, None] * tm >= ends[None, :]).astype(jnp.int32), axis=1)
    group, pair = bucket // N_PAIRS, bucket % N_PAIRS
    ea = jnp.array([0, 0, 0, 1, 1, 2], jnp.int32)[pair]
    eb = jnp.array([1, 2, 3, 2, 3, 3], jnp.int32)[pair]
    base = group * EXPERTS_PER_GROUP
    return pos, inv, base + ea, base + eb, n_valid.reshape(1).astype(jnp.int32)


def _rotate_half_cols(w):
    half = MLA_ROPE_DIM // 2
    return jnp.concatenate([-w[..., half:], w[..., :half]], axis=-1)


def _rope_tables(s_max):
    half = MLA_ROPE_DIM // 2
    inv = ROPE_BASE ** (-jnp.arange(half, dtype=F32) / half)
    ang = jnp.arange(s_max, dtype=F32)[:, None] * inv[None, :]
    pad = jnp.zeros((s_max, LANES - MLA_ROPE_DIM), F32)
    cos, sin = jnp.cos(ang), jnp.sin(ang)
    return (jnp.concatenate([cos, cos, pad], axis=-1), jnp.concatenate([sin, sin, pad], axis=-1))


def _layer(x_prompt, x_sample, c_prompt, c_sample, layer_idx, w_ada, b_ada, g_pre_mix, g_post_mix, w_in,
           lambda_q1, lambda_k1, lambda_q2, lambda_k2, g_diff_sub, g_q_a, w_uq, g_kv_a, w_ukv, w_o,
           g_pre_ffn, g_post_ffn, w_router_group, b_router_group, w_router_expert, b_router_expert,
           w_gate, w_up, w_down, tm=256):
    bp, sp, d = x_prompt.shape
    bs, ss, _ = x_sample.shape
    rows = _Rows(bp, sp, bs, ss, tm)
    xp = x_prompt.reshape(bp * sp, d)
    xs = x_sample.reshape(bs * ss, d)
    lambda_init = 0.8 - 0.6 * math.exp(-0.3 * layer_idx)

    nb = bp + bs
    nb_pad = -(-nb // 8) * 8
    c_all = jnp.concatenate([c_prompt, c_sample, jnp.zeros((nb_pad - nb, d), F32)], axis=0)
    mod = _ada(c_all, w_ada, b_ada.reshape(1, -1)).reshape(nb_pad, N_MOD, d)

    w_kr = w_in[:, C_KR:C_KR + MLA_ROPE_DIM]
    w_kr = jnp.concatenate([w_kr, _rotate_half_cols(w_kr)], axis=1).astype(BF16)
    wq3 = w_uq.reshape(Q_LORA_RANK, MLA_HEADS, MLA_NOPE_DIM + MLA_ROPE_DIM)
    wq_rope = wq3[..., MLA_NOPE_DIM:]
    w_q = jnp.concatenate([wq3[..., :MLA_NOPE_DIM], wq_rope, _rotate_half_cols(wq_rope)], axis=-1)
    w_q = w_q.reshape(Q_LORA_RANK, MLA_HEADS * MLA_QK_PAD).astype(BF16)
    w_kv = w_ukv.astype(BF16)
    cos_t, sin_t = _rope_tables(max(sp, ss))

    dq, dk, dv, qm, km, vm = _inproj(rows, xp, xs, mod, g_pre_mix.reshape(1, d), w_in.astype(BF16), w_kr,
                                     g_q_a.reshape(1, -1), w_q, g_kv_a.reshape(1, -1), w_kv, cos_t, sin_t)

    slopes = jnp.array([2.0 ** (-8.0 * (i + 1) / DIFF_HEADS) for i in range(DIFF_HEADS)], dtype=F32)
    lam_p = jnp.zeros((8, LANES), F32).at[0:4, 0:DIFF_HEAD_DIM].set(
        jnp.stack([lambda_q1, lambda_k1, lambda_q2, lambda_k2]))
    g_sub = g_diff_sub.reshape(1, -1)
    oa_p = _diff_attn(dq, dk, dv, slopes, lam_p, g_sub, 0, bp, sp, lambda_init)
    oa_s = _diff_attn(dq, dk, dv, slopes, lam_p, g_sub, rows.tp, bs, ss, lambda_init)
    ob_p = _mla_attn(qm, km, vm, 0, bp, sp)
    ob_s = _mla_attn(qm, km, vm, rows.tp, bs, ss)

    w_r = jnp.concatenate([w_router_group, w_router_expert.reshape(d, N_EXPERTS),
                           jnp.zeros((d, LANES - N_GROUPS - N_EXPERTS), F32)], axis=1)
    b_r = jnp.concatenate([b_router_group, b_router_expert.reshape(N_EXPERTS),
                           jnp.zeros((LANES - N_GROUPS - N_EXPERTS,), F32)]).reshape(1, LANES)
    x_mid, hx, meta, counts = _oproj(rows, oa_p, oa_s, ob_p, ob_s, xp, xs, mod, w_o.astype(BF16),
                                     g_post_mix.reshape(1, d), g_pre_ffn.reshape(1, d), w_r, b_r)

    pos, inv, tile_ea, tile_eb, n_valid = _moe_plan(meta, counts, MOE_TM)
    f_sorted = _experts(hx, inv.reshape(-1, 1, MOE_TM), tile_ea, tile_eb, n_valid,
                        w_gate.astype(BF16), w_up.astype(BF16), w_down.astype(BF16))
    yp, ys = _final(rows, pos.reshape(rows.n, 1, tm), f_sorted, x_mid, mod, g_post_ffn.reshape(1, d))
    return yp.reshape(bp, sp, d), ys.reshape(bs, ss, d)


def kernel(x_prompt, x_sample, c_prompt, c_sample, w_ada, b_ada, g_pre_mix, g_post_mix, w_in, lambda_q1,
           lambda_k1, lambda_q2, lambda_k2, g_diff_sub, g_q_a, w_uq, g_kv_a, w_ukv, w_o, g_pre_ffn,
           g_post_ffn, w_router_group, b_router_group, w_router_expert, b_router_expert, w_gate, w_up,
           w_down):
    for l in range(w_ada.shape[0]):
        x_prompt, x_sample = _layer(
            x_prompt, x_sample, c_prompt, c_sample, l, w_ada[l], b_ada[l], g_pre_mix[l], g_post_mix[l],
            w_in[l], lambda_q1[l], lambda_k1[l], lambda_q2[l], lambda_k2[l], g_diff_sub[l], g_q_a[l],
            w_uq[l], g_kv_a[l], w_ukv[l], w_o[l], g_pre_ffn[l], g_post_ffn[l], w_router_group[l],
            b_router_group[l], w_router_expert[l], b_router_expert[l], w_gate[l], w_up[l], w_down[l])
    return x_prompt, x_sample
```

```python
import functools
import math

import jax
import jax.numpy as jnp
from jax import lax
from jax.experimental import pallas as pl
from jax.experimental.pallas import tpu as pltpu

F32 = jnp.float32
BF16 = jnp.bfloat16

D_MODEL = 2048
DIFF_HEADS = 8
DIFF_HEAD_DIM = 64
DIFF_WIDTH = DIFF_HEADS * 2 * DIFF_HEAD_DIM
MLA_HEADS = 8
MLA_NOPE_DIM = 128
MLA_ROPE_DIM = 64
MLA_V_DIM = 128
MLA_QK_PAD = 256
V_AUG = 256
Q_LORA_RANK = 512
KV_LORA_RANK = 256
ROPE_BASE = 10000.0
D_MLA_OUT = MLA_HEADS * MLA_V_DIM
N_GROUPS = 4
EXPERTS_PER_GROUP = 4
N_EXPERTS = N_GROUPS * EXPERTS_PER_GROUP
D_EXPERT = 512
N_MOD = 6
NORM_EPS = 1e-6
LANES = 128
ROUTER_LANE0 = N_GROUPS
N_PAIRS = 6
N_BUCKETS = N_GROUPS * N_PAIRS
HX_WIDTH = D_MODEL + LANES
MOE_TM = 256
OPROJ_SPLIT = 1

C_DQ, C_DK, C_DV = 0, DIFF_WIDTH, 2 * DIFF_WIDTH
C_CQ = 3 * DIFF_WIDTH
C_CKV = C_CQ + Q_LORA_RANK
C_KR = C_CKV + KV_LORA_RANK
D_IN_PROJ = C_KR + MLA_ROPE_DIM

VMEM_LIMIT = 56 * 1024 * 1024
LOG2E = 1.4426950408889634
ATTN_TQ = 256
ATTN_TK = 512
ATTN_TQ_OUTER = 1024
ATTN_TKB = 2048
ATTN_UNROLL = 2
ATTN_GROUP = 4
ATTN_SCORE_BYTES = 16 * 1024 * 1024


def _rms(x, g):
    return x * lax.rsqrt(jnp.mean(x * x, axis=-1, keepdims=True) + NORM_EPS) * g


def _dot(a, b):
    return jnp.dot(a, b, preferred_element_type=F32)


def _dot_nt(a, b):
    return lax.dot_general(a, b, (((1,), (1,)), ((), ())), preferred_element_type=F32)


class _Rows:
    def __init__(self, bp, sp, bs, ss, tm):
        assert sp % tm == 0 and ss % tm == 0
        self.bp, self.sp, self.bs, self.ss, self.tm = bp, sp, bs, ss, tm
        self.n_p = bp * sp // tm
        self.n_s = bs * ss // tm
        self.n = self.n_p + self.n_s
        self.tp = bp * sp
        self.ts = bs * ss
        self.t = self.tp + self.ts

    def p_idx(self, i):
        return jnp.minimum(i, self.n_p - 1)

    def s_idx(self, i):
        return jnp.maximum(i - self.n_p, 0)

    def seq(self, i):
        return jnp.where(i < self.n_p, i // (self.sp // self.tm),
                         self.bp + (i - self.n_p) // (self.ss // self.tm))

    def pos(self, i):
        return jnp.where(i < self.n_p, i % (self.sp // self.tm),
                         (i - self.n_p) % (self.ss // self.tm))


def _ada_kernel(c_ref, w_ref, b_ref, o_ref):
    c = c_ref[...]
    a = c / (1.0 + jnp.exp(-c))
    o_ref[...] = jnp.dot(a, w_ref[...], preferred_element_type=F32,
                         precision=lax.Precision.HIGHEST) + b_ref[...]


def _ada(c_all, w_ada, b_ada):
    nb, d = c_all.shape
    n = w_ada.shape[1]
    tn = 1024
    return pl.pallas_call(
        _ada_kernel,
        grid=(n // tn,),
        in_specs=[pl.BlockSpec((nb, d), lambda j: (0, 0)),
                  pl.BlockSpec((d, tn), lambda j: (0, j)),
                  pl.BlockSpec((1, tn), lambda j: (0, j))],
        out_specs=pl.BlockSpec((nb, tn), lambda j: (0, j)),
        out_shape=jax.ShapeDtypeStruct((nb, n), F32),
        compiler_params=pltpu.CompilerParams(dimension_semantics=("arbitrary",),
                                             vmem_limit_bytes=VMEM_LIMIT),
        name="ada",
    )(c_all, w_ada, b_ada)


def _rope_half(x, cos, sin):
    return x * cos + pltpu.roll(x, MLA_ROPE_DIM, 1) * sin


def _inproj_kernel(n_p, xp_ref, xs_ref, mod_ref, gpre_ref, win_ref, wkr_ref, gqa_ref, wq_ref, gkva_ref,
                   wkv_ref, cos_ref, sin_ref, dq_ref, dk_ref, dv_ref, qm_ref, km_ref, vm_ref):
    i = pl.program_id(0)
    x = jnp.where(i < n_p, xp_ref[...], xs_ref[...])
    shift = mod_ref[0, 0:1, :]
    scale = mod_ref[0, 1:2, :]
    hb = (_rms(x, gpre_ref[...]) * (1.0 + scale) + shift).astype(BF16)

    dq_ref[...] = (_dot(hb, win_ref[:, C_DQ:C_DQ + DIFF_WIDTH]) * (DIFF_HEAD_DIM ** -0.5 * LOG2E)).astype(BF16)
    dk_ref[...] = _dot(hb, win_ref[:, C_DK:C_DK + DIFF_WIDTH]).astype(BF16)
    lane = lax.broadcasted_iota(jnp.int32, (x.shape[0], LANES), 1)
    ones_col = jnp.where(lane == 0, 1.0, 0.0).astype(BF16)
    dv = _dot(hb, win_ref[:, C_DV:C_DV + DIFF_WIDTH]).astype(BF16)
    for h in range(DIFF_HEADS):
        dv_ref[:, h * V_AUG:h * V_AUG + LANES] = dv[:, h * LANES:(h + 1) * LANES]
        dv_ref[:, h * V_AUG + LANES:(h + 1) * V_AUG] = ones_col

    cos = cos_ref[...]
    sin = sin_ref[...]
    cq = _dot(hb, win_ref[:, C_CQ:C_CQ + Q_LORA_RANK])
    ckv = _dot(hb, win_ref[:, C_CKV:C_CKV + KV_LORA_RANK])
    kr = _rope_half(_dot(hb, wkr_ref[...]), cos, sin).astype(BF16)

    q = _dot(_rms(cq, gqa_ref[...]).astype(BF16), wq_ref[...])
    qscale = (MLA_NOPE_DIM + MLA_ROPE_DIM) ** -0.5 * LOG2E
    for h in range(MLA_HEADS):
        c0 = h * MLA_QK_PAD
        qm_ref[:, c0:c0 + LANES] = (q[:, c0:c0 + LANES] * qscale).astype(BF16)
        qm_ref[:, c0 + LANES:c0 + 2 * LANES] = (
            _rope_half(q[:, c0 + LANES:c0 + 2 * LANES], cos, sin) * qscale).astype(BF16)

    kv = _dot(_rms(ckv, gkva_ref[...]).astype(BF16), wkv_ref[...])
    for h in range(MLA_HEADS):
        c0 = h * MLA_QK_PAD
        km_ref[:, c0:c0 + LANES] = kv[:, c0:c0 + LANES].astype(BF16)
        km_ref[:, c0 + LANES:c0 + 2 * LANES] = kr
        vm_ref[:, h * V_AUG:h * V_AUG + LANES] = kv[:, c0 + LANES:c0 + 2 * LANES].astype(BF16)
        vm_ref[:, h * V_AUG + LANES:(h + 1) * V_AUG] = ones_col


def _inproj(rows, xp, xs, mod, g_pre, w_in, w_kr, g_q_a, w_q, g_kv_a, w_kv, cos_t, sin_t):
    tm, d = rows.tm, D_MODEL
    const = lambda i: (0, 0)
    row = lambda i: (i, 0)
    one = pl.Buffered(1)
    out_w = (DIFF_WIDTH, DIFF_WIDTH, DIFF_HEADS * V_AUG, MLA_HEADS * MLA_QK_PAD, MLA_HEADS * MLA_QK_PAD,
             MLA_HEADS * V_AUG)
    return pl.pallas_call(
        functools.partial(_inproj_kernel, rows.n_p),
        grid=(rows.n,),
        in_specs=[pl.BlockSpec((tm, d), lambda i: (rows.p_idx(i), 0)),
                  pl.BlockSpec((tm, d), lambda i: (rows.s_idx(i), 0)),
                  pl.BlockSpec((1, N_MOD, d), lambda i: (rows.seq(i), 0, 0)),
                  pl.BlockSpec((1, d), const),
                  pl.BlockSpec((d, D_IN_PROJ), const, pipeline_mode=one),
                  pl.BlockSpec((d, 2 * MLA_ROPE_DIM), const),
                  pl.BlockSpec((1, Q_LORA_RANK), const),
                  pl.BlockSpec((Q_LORA_RANK, MLA_HEADS * MLA_QK_PAD), const, pipeline_mode=one),
                  pl.BlockSpec((1, KV_LORA_RANK), const),
                  pl.BlockSpec((KV_LORA_RANK, MLA_HEADS * MLA_QK_PAD), const, pipeline_mode=one),
                  pl.BlockSpec((tm, LANES), lambda i: (rows.pos(i), 0)),
                  pl.BlockSpec((tm, LANES), lambda i: (rows.pos(i), 0))],
        out_specs=[pl.BlockSpec((tm, w), row) for w in out_w],
        out_shape=[jax.ShapeDtypeStruct((rows.t, w), BF16) for w in out_w],
        compiler_params=pltpu.CompilerParams(dimension_semantics=("arbitrary",),
                                             vmem_limit_bytes=VMEM_LIMIT),
        name="inproj",
    )(xp, xs, mod, g_pre, w_in, w_kr, g_q_a, w_q, g_kv_a, w_kv, cos_t, sin_t)


def _lane_groups(x, op):
    out = x[:, 0:LANES]
    for g in range(1, x.shape[1] // LANES):
        out = op(out, x[:, g * LANES:(g + 1) * LANES])
    return out


def _aligned(x, m):
    return x if isinstance(x, int) else pl.multiple_of(x, m)


def _chunk(ref, j, tk):
    return ref[pl.ds(_aligned(j * tk, tk), tk), :]


def _score_block(unit, s_view, jb, cpb, tk, mpart):
    score_fn, shift_fn = unit
    for c in range(cpb):
        j = jb * cpb + c
        s = score_fn(j)
        s_view[jb, :, c * tk:(c + 1) * tk] = s
        part = _lane_groups(s, jnp.maximum)
        if shift_fn is not None:
            part = part + shift_fn(j)
        mpart = jnp.maximum(mpart, part)
    return mpart


def _value_block(unit, m, v_ref, s_view, jb, cpb, tk, acc):
    _, shift_fn = unit
    ps = []
    for c in range(cpb):
        j = jb * cpb + c
        mj = m if shift_fn is None else m - shift_fn(j)
        ps.append(jnp.exp2(s_view[jb, :, c * tk:(c + 1) * tk] - mj).astype(BF16))
    p_all = ps[0] if cpb == 1 else jnp.concatenate(ps, axis=1)
    return acc + _dot(p_all, _chunk(v_ref, jb, cpb * tk))


def _attn_phases(geom, v_ref, score=(), value=()):
    tq, tk, cpb, nkb = geom
    zeros = jnp.zeros((tq, 2 * LANES), F32)
    neg_inf = jnp.full((tq, LANES), -jnp.inf, F32)

    def body(jb, carry):
        mparts, accs = carry
        mparts = tuple(_score_block(u, view, jb, cpb, tk, mp) for (u, view), mp in zip(score, mparts))
        accs = tuple(_value_block(u, m, v_ref, view, jb, cpb, tk, acc) for (u, m, view), acc in zip(value, accs))
        return mparts, accs

    init = ((neg_inf,) * len(score), (zeros,) * len(value))
    mparts, accs = lax.fori_loop(0, nkb, body, init, unroll=min(ATTN_UNROLL, nkb))
    return ([jnp.max(mp, axis=-1, keepdims=True) for mp in mparts],
            [acc[:, 0:LANES] / acc[:, LANES:LANES + 1] for acc in accs])


def _attn_grouped(geom, v_ref, units, s_buf):
    views = [s_buf.at[k] for k in range(len(units))]
    ms, _ = _attn_phases(geom, v_ref, score=tuple(zip(units, views)))
    _, outs = _attn_phases(geom, v_ref, value=tuple(zip(units, ms, views)))
    return outs


def _diff_attn_kernel(lambda_init, geom, n_sub, slopes_ref, lam_ref, gsub_ref, q_ref, k_ref, v_ref,
                      o_ref, s_buf, bias_buf):
    tq, tk, _, _ = geom
    h = pl.program_id(1)
    qo = pl.program_id(2)
    slope = slopes_ref[h] * LOG2E
    n_cross = tk // tq

    r_minus_c = (lax.broadcasted_iota(jnp.int32, (tq, tk), 0)
                 - lax.broadcasted_iota(jnp.int32, (tq, tk), 1)).astype(F32)
    bias_buf[0] = r_minus_c * (-slope)
    bias_buf[1] = r_minus_c * slope
    for c in range(n_cross):
        bias_buf[2 + c] = jnp.abs(r_minus_c + float(c * tq)) * (-slope)

    lp = lam_ref[...]
    lam = (jnp.exp(jnp.sum(lp[0:1] * lp[1:2], axis=-1, keepdims=True))
           - jnp.exp(jnp.sum(lp[2:3] * lp[3:4], axis=-1, keepdims=True)) + lambda_init)
    gsub = gsub_ref[...]

    def unit(sub, half):
        qbase = (qo * n_sub + sub) * tq
        jc = qbase // tk
        cross = 2 + (qbase - jc * tk) // tq
        q = q_ref[pl.ds(_aligned(sub * tq, tq), tq), :]
        lane = lax.broadcasted_iota(jnp.int32, q.shape, 1)
        keep = (lane < DIFF_HEAD_DIM) if half == 0 else (lane >= DIFF_HEAD_DIM)
        qh = jnp.where(keep, q, jnp.zeros_like(q))
        shift = lambda j: jnp.where(j == jc, 0.0, -slope * jnp.abs(qbase - j * tk).astype(F32))
        which = lambda j: jnp.where(j < jc, 0, jnp.where(j > jc, 1, cross))
        return (lambda j: _dot_nt(qh, _chunk(k_ref, j, tk)) + bias_buf[which(j)]), shift

    group = s_buf.shape[0] // 2

    def grouped(g, carry):
        subs = [g * group + k for k in range(group)]
        outs = _attn_grouped(geom, v_ref, [unit(sub, half) for sub in subs for half in range(2)], s_buf)
        for k, sub in enumerate(subs):
            o = outs[2 * k] - lam * outs[2 * k + 1]
            o_ref[pl.ds(_aligned(sub * tq, tq), tq), :] = (_rms(o, gsub) * (1.0 - lambda_init)).astype(BF16)
        return carry

    lax.fori_loop(0, n_sub // group, grouped, 0)


def _attn_tiles(s, row0, units_per_sub):
    tq = min(ATTN_TQ, s)
    tk = min(ATTN_TK, s)
    tkb = min(ATTN_TKB, s)
    tq_outer = min(ATTN_TQ_OUTER, s)
    assert s % tq_outer == 0 and tq_outer % tq == 0 and s % tkb == 0 and tkb % tk == 0 and tk % tq == 0
    assert row0 % s == 0
    nkb, n_sub = s // tkb, tq_outer // tq
    units = max(units_per_sub, min(ATTN_GROUP, ATTN_SCORE_BYTES // (tq * s * 4)))
    group = min(units // units_per_sub, n_sub)
    assert n_sub % group == 0
    return (tq, tk, tkb // tk, nkb), tq_outer, (group * units_per_sub, nkb, tq, tkb)


def _diff_attn(dq, dk, dv, slopes, lam_p, g_sub, row0, b, s, lambda_init):
    geom, tq_outer, s_buf_shape = _attn_tiles(s, row0, 2)
    tq, tk = geom[0], geom[1]
    n_sub, nqo = tq_outer // tq, s // tq_outer
    dh = 2 * DIFF_HEAD_DIM
    return pl.pallas_call(
        functools.partial(_diff_attn_kernel, lambda_init, geom, n_sub),
        grid=(b, DIFF_HEADS, nqo),
        in_specs=[pl.BlockSpec(memory_space=pltpu.SMEM),
                  pl.BlockSpec((8, LANES), lambda bi, h, qi: (0, 0)),
                  pl.BlockSpec((1, dh), lambda bi, h, qi: (0, 0)),
                  pl.BlockSpec((tq_outer, dh), lambda bi, h, qi: (row0 // tq_outer + bi * nqo + qi, h)),
                  pl.BlockSpec((s, dh), lambda bi, h, qi: (row0 // s + bi, h)),
                  pl.BlockSpec((s, V_AUG), lambda bi, h, qi: (row0 // s + bi, h))],
        out_specs=pl.BlockSpec((tq_outer, dh), lambda bi, h, qi: (bi * nqo + qi, h)),
        out_shape=jax.ShapeDtypeStruct((b * s, DIFF_WIDTH), BF16),
        scratch_shapes=[pltpu.VMEM(s_buf_shape, F32), pltpu.VMEM((2 + tk // tq, tq, tk), F32)],
        compiler_params=pltpu.CompilerParams(
            dimension_semantics=("arbitrary", "arbitrary", "arbitrary"), vmem_limit_bytes=VMEM_LIMIT),
        name="diff_attn",
    )(slopes, lam_p, g_sub, dq, dk, dv)


def _mla_attn_kernel(geom, n_sub, q_ref, k_ref, v_ref, o_ref, s_buf):
    tq, tk, _, _ = geom

    def unit(sub):
        q = q_ref[pl.ds(_aligned(sub * tq, tq), tq), :]
        return (lambda j: _dot_nt(q, _chunk(k_ref, j, tk))), None

    group = s_buf.shape[0]

    def grouped(g, carry):
        subs = [g * group + k for k in range(group)]
        for sub, o in zip(subs, _attn_grouped(geom, v_ref, [unit(sub) for sub in subs], s_buf)):
            o_ref[pl.ds(_aligned(sub * tq, tq), tq), :] = o.astype(BF16)
        return carry

    lax.fori_loop(0, n_sub // group, grouped, 0)


def _mla_attn(qm, km, vm, row0, b, s):
    geom, tq_outer, s_buf_shape = _attn_tiles(s, row0, 1)
    n_sub, nqo = tq_outer // geom[0], s // tq_outer
    return pl.pallas_call(
        functools.partial(_mla_attn_kernel, geom, n_sub),
        grid=(b, MLA_HEADS, nqo),
        in_specs=[pl.BlockSpec((tq_outer, MLA_QK_PAD), lambda bi, h, qi: (row0 // tq_outer + bi * nqo + qi, h)),
                  pl.BlockSpec((s, MLA_QK_PAD), lambda bi, h, qi: (row0 // s + bi, h)),
                  pl.BlockSpec((s, V_AUG), lambda bi, h, qi: (row0 // s + bi, h))],
        out_specs=pl.BlockSpec((tq_outer, MLA_V_DIM), lambda bi, h, qi: (bi * nqo + qi, h)),
        out_shape=jax.ShapeDtypeStruct((b * s, D_MLA_OUT), BF16),
        scratch_shapes=[pltpu.VMEM(s_buf_shape, F32)],
        compiler_params=pltpu.CompilerParams(
            dimension_semantics=("arbitrary", "arbitrary", "arbitrary"), vmem_limit_bytes=VMEM_LIMIT),
        name="mla_attn",
    )(qm, km, vm)


def _route(logits):
    lane = lax.broadcasted_iota(jnp.int32, logits.shape, 1)
    neg = jnp.full(logits.shape, -jnp.inf, F32)
    big = jnp.full(logits.shape, LANES, jnp.int32)
    first = lambda mask: jnp.min(jnp.where(mask, lane, big), axis=-1, keepdims=True)

    gl = jnp.where(lane < N_GROUPS, logits, neg)
    gmax = jnp.max(gl, axis=-1, keepdims=True)
    g_idx = first(gl == gmax)
    g_w = 1.0 / jnp.sum(jnp.exp(gl - gmax), axis=-1, keepdims=True)

    lo = ROUTER_LANE0 + EXPERTS_PER_GROUP * g_idx
    el = jnp.where(lane >= lo, jnp.where(lane < lo + EXPERTS_PER_GROUP, logits, neg), neg)
    v1 = jnp.max(el, axis=-1, keepdims=True)
    i1 = first(el == v1)
    el2 = jnp.where(lane == i1, neg, el)
    v2 = jnp.max(el2, axis=-1, keepdims=True)
    i2 = first(el2 == v2)
    t = jnp.exp(v2 - v1)
    w1 = g_w / (1.0 + t)
    w2 = w1 * t
    first_low = i1 < i2
    ea = jnp.minimum(i1, i2) - lo
    eb = jnp.maximum(i1, i2) - lo
    pair = jnp.where(ea == 0, 0, jnp.where(ea == 1, 3, 5)) + eb - ea - 1
    bucket = g_idx * N_PAIRS + pair
    return bucket, jnp.where(first_low, w1, w2), jnp.where(first_low, w2, w1)


def _oproj_kernel(n_p, oap_ref, oas_ref, obp_ref, obs_ref, xp_ref, xs_ref, mod_ref, wo_ref, gpost_ref,
                  gpre_ref, wr_ref, br_ref, xmid_ref, hx_ref, meta_ref, counts_ref, cnt):
    i = pl.program_id(0)

    @pl.when(i == 0)
    def _():
        cnt[...] = jnp.zeros(cnt.shape, F32)

    is_p = i < n_p
    w_r = wr_ref[...]
    w_hi = w_r.astype(BF16)
    w_hi_lo = jnp.concatenate([w_hi, (w_r - w_hi.astype(F32)).astype(BF16)], axis=1)
    tm = xmid_ref.shape[0]
    th = tm // OPROJ_SPLIT
    lane = lax.broadcasted_iota(jnp.int32, (th, LANES), 1)
    earlier = jnp.where(lax.broadcasted_iota(jnp.int32, (th, th), 0) > lax.broadcasted_iota(jnp.int32, (th, th), 1),
                        1.0, 0.0).astype(BF16)

    for part in range(OPROJ_SPLIT):
        r = slice(part * th, (part + 1) * th)
        oa = jnp.where(is_p, oap_ref[r, :], oas_ref[r, :])
        ob = jnp.where(is_p, obp_ref[r, :], obs_ref[r, :])
        x = jnp.where(is_p, xp_ref[r, :], xs_ref[r, :])
        mix = _dot(oa, wo_ref[0:DIFF_WIDTH, :]) + _dot(ob, wo_ref[DIFF_WIDTH:DIFF_WIDTH + D_MLA_OUT, :])
        x_mid = x + mod_ref[0, 2:3, :] * _rms(mix, gpost_ref[...])
        xmid_ref[r, :] = x_mid
        h2 = _rms(x_mid, gpre_ref[...]) * (1.0 + mod_ref[0, 4:5, :]) + mod_ref[0, 3:4, :]
        h_hi = h2.astype(BF16)
        h_lo = (h2 - h_hi.astype(F32)).astype(BF16)
        hi_terms = _dot(h_hi, w_hi_lo)
        logits = hi_terms[:, 0:LANES] + hi_terms[:, LANES:2 * LANES] + _dot(h_lo, w_hi) + br_ref[...]
        bucket, wa, wb = _route(logits)

        hx_ref[r, 0:D_MODEL] = h2
        hx_ref[r, D_MODEL:HX_WIDTH] = jnp.where(lane == 0, wa, jnp.where(lane == 1, wb, jnp.zeros_like(logits)))

        onehot = jnp.where(lane == bucket, 1.0, 0.0)
        before = _dot(earlier, onehot.astype(BF16))
        rank = jnp.sum(onehot * (before + cnt[0:1, :]), axis=-1, keepdims=True)
        cnt[...] = cnt[...] + jnp.sum(onehot, axis=0, keepdims=True)
        meta_ref[r, :] = jnp.where(lane == 0, bucket, jnp.where(lane == 1, rank.astype(jnp.int32), 0))
    counts_ref[...] = cnt[...]


def _oproj(rows, oa_p, oa_s, ob_p, ob_s, xp, xs, mod, w_o, g_post, g_pre, w_r, b_r):
    tm, d = rows.tm, D_MODEL
    const = lambda i: (0, 0)
    row = lambda i: (i, 0)
    prow = lambda i: (rows.p_idx(i), 0)
    srow = lambda i: (rows.s_idx(i), 0)
    return pl.pallas_call(
        functools.partial(_oproj_kernel, rows.n_p),
        grid=(rows.n,),
        in_specs=[pl.BlockSpec((tm, DIFF_WIDTH), prow), pl.BlockSpec((tm, DIFF_WIDTH), srow),
                  pl.BlockSpec((tm, D_MLA_OUT), prow), pl.BlockSpec((tm, D_MLA_OUT), srow),
                  pl.BlockSpec((tm, d), prow), pl.BlockSpec((tm, d), srow),
                  pl.BlockSpec((1, N_MOD, d), lambda i: (rows.seq(i), 0, 0)),
                  pl.BlockSpec((DIFF_WIDTH + D_MLA_OUT, d), const, pipeline_mode=pl.Buffered(1)),
                  pl.BlockSpec((1, d), const), pl.BlockSpec((1, d), const),
                  pl.BlockSpec((d, LANES), const), pl.BlockSpec((1, LANES), const)],
        out_specs=[pl.BlockSpec((tm, d), row), pl.BlockSpec((tm, HX_WIDTH), row),
                   pl.BlockSpec((tm, LANES), row), pl.BlockSpec((8, LANES), const)],
        out_shape=[jax.ShapeDtypeStruct((rows.t, d), F32), jax.ShapeDtypeStruct((rows.t, HX_WIDTH), F32),
                   jax.ShapeDtypeStruct((rows.t, LANES), jnp.int32), jax.ShapeDtypeStruct((8, LANES), F32)],
        scratch_shapes=[pltpu.VMEM((8, LANES), F32)],
        compiler_params=pltpu.CompilerParams(dimension_semantics=("arbitrary",),
                                             vmem_limit_bytes=VMEM_LIMIT),
        name="oproj",
    )(oa_p, oa_s, ob_p, ob_s, xp, xs, mod, w_o, g_post, g_pre, w_r, b_r)


def _gather_rows(idx_ref, src_hbm, dst, sem, inline=False):
    def body(r, carry):
        pltpu.make_async_copy(src_hbm.at[pl.ds(idx_ref[0, 0, r], 1), :], dst.at[pl.ds(r, 1), :], sem).start()
        return carry

    if inline:
        for r in range(dst.shape[0]):
            body(r, 0)
    else:
        lax.fori_loop(0, dst.shape[0], body, 0, unroll=8)


def _wait_rows(src_hbm, dst, sem):
    pltpu.make_async_copy(src_hbm.at[pl.ds(0, dst.shape[0]), :], dst, sem).wait()


def _experts_kernel(ea_ref, eb_ref, nv_ref, inv_cur, inv_nxt, hx_hbm, wga, wua, wda, wgb, wub, wdb,
                    f_ref, xbuf, sem):
    i = pl.program_id(0)
    n_valid = nv_ref[0]
    slot = i % 2

    @pl.when(i == 0)
    def _():
        _gather_rows(inv_cur, hx_hbm, xbuf.at[0], sem.at[0])

    def compute():
        _wait_rows(hx_hbm, xbuf.at[slot], sem.at[slot])
        x = xbuf[slot, :, 0:D_MODEL].astype(BF16)
        aux = xbuf[slot, :, D_MODEL:HX_WIDTH]

        def hidden(wg, wu, w):
            g = _dot(x, wg[0])
            return ((g / (1.0 + jnp.exp(-g))) * _dot(x, wu[0]) * w).astype(BF16)

        f_ref[...] = (_dot(hidden(wga, wua, aux[:, 0:1]), wda[0])
                      + _dot(hidden(wgb, wub, aux[:, 1:2]), wdb[0]))

    @pl.when(i + 1 < n_valid)
    def _():
        _gather_rows(inv_nxt, hx_hbm, xbuf.at[1 - slot], sem.at[1 - slot], inline=True)
        compute()

    @pl.when(i + 1 == n_valid)
    def _():
        compute()

    @pl.when(i >= n_valid)
    def _():
        f_ref[...] = jnp.zeros(f_ref.shape, F32)


def _experts(hx, inv3, tile_ea, tile_eb, n_valid, w_gate, w_up, w_down):
    n_tiles, _, tm = inv3.shape
    d = D_MODEL
    wspec = lambda shape, which: pl.BlockSpec(
        (1,) + shape, (lambda i, ea, eb, nv: (ea[i], 0, 0)) if which == 0 else (lambda i, ea, eb, nv: (eb[i], 0, 0)))
    smem = lambda imap: pl.BlockSpec((1, 1, tm), imap, memory_space=pltpu.SMEM)
    return pl.pallas_call(
        _experts_kernel,
        grid_spec=pltpu.PrefetchScalarGridSpec(
            num_scalar_prefetch=3,
            grid=(n_tiles,),
            in_specs=[smem(lambda i, ea, eb, nv: (i, 0, 0)),
                      smem(lambda i, ea, eb, nv: (jnp.minimum(i + 1, n_tiles - 1), 0, 0)),
                      pl.BlockSpec(memory_space=pl.ANY),
                      wspec((d, D_EXPERT), 0), wspec((d, D_EXPERT), 0), wspec((D_EXPERT, d), 0),
                      wspec((d, D_EXPERT), 1), wspec((d, D_EXPERT), 1), wspec((D_EXPERT, d), 1)],
            out_specs=pl.BlockSpec((tm, d), lambda i, ea, eb, nv: (i, 0)),
            scratch_shapes=[pltpu.VMEM((2, tm, HX_WIDTH), F32), pltpu.SemaphoreType.DMA((2,))]),
        out_shape=jax.ShapeDtypeStruct((n_tiles * tm, d), F32),
        compiler_params=pltpu.CompilerParams(dimension_semantics=("arbitrary",),
                                             vmem_limit_bytes=VMEM_LIMIT),
        name="experts",
    )(tile_ea, tile_eb, n_valid, inv3, inv3, hx, w_gate, w_up, w_down, w_gate, w_up, w_down)


def _final_kernel(n_p, pos_cur, pos_nxt, f_hbm, xmid_ref, mod_ref, gpost_ref, yp_ref, ys_ref, fbuf, sem):
    i = pl.program_id(0)
    slot = i % 2

    @pl.when(i == 0)
    def _():
        _gather_rows(pos_cur, f_hbm, fbuf.at[0], sem.at[0])

    def compute():
        _wait_rows(f_hbm, fbuf.at[slot], sem.at[slot])
        y = xmid_ref[...] + mod_ref[0, 5:6, :] * _rms(fbuf[slot], gpost_ref[...])

        @pl.when(i < n_p)
        def _():
            yp_ref[...] = y

        @pl.when(i >= n_p)
        def _():
            ys_ref[...] = y

    @pl.when(i + 1 < pl.num_programs(0))
    def _():
        _gather_rows(pos_nxt, f_hbm, fbuf.at[1 - slot], sem.at[1 - slot], inline=True)
        compute()

    @pl.when(i + 1 == pl.num_programs(0))
    def _():
        compute()


def _final(rows, pos3, f_sorted, x_mid, mod, g_post):
    tm, d = rows.tm, D_MODEL
    smem = lambda imap: pl.BlockSpec((1, 1, tm), imap, memory_space=pltpu.SMEM)
    return pl.pallas_call(
        functools.partial(_final_kernel, rows.n_p),
        grid=(rows.n,),
        in_specs=[smem(lambda i: (i, 0, 0)),
                  smem(lambda i: (jnp.minimum(i + 1, rows.n - 1), 0, 0)),
                  pl.BlockSpec(memory_space=pl.ANY),
                  pl.BlockSpec((tm, d), lambda i: (i, 0)),
                  pl.BlockSpec((1, N_MOD, d), lambda i: (rows.seq(i), 0, 0)),
                  pl.BlockSpec((1, d), lambda i: (0, 0))],
        out_specs=[pl.BlockSpec((tm, d), lambda i: (rows.p_idx(i), 0)),
                   pl.BlockSpec((tm, d), lambda i: (rows.s_idx(i), 0))],
        out_shape=[jax.ShapeDtypeStruct((rows.tp, d), F32), jax.ShapeDtypeStruct((rows.ts, d), F32)],
        scratch_shapes=[pltpu.VMEM((2, tm, d), F32), pltpu.SemaphoreType.DMA((2,))],
        compiler_params=pltpu.CompilerParams(dimension_semantics=("arbitrary",),
                                             vmem_limit_bytes=VMEM_LIMIT),
        name="final",
    )(pos3, pos3, f_sorted, x_mid, mod, g_post)


def _moe_plan(meta, counts, tm):
    t = meta.shape[0]
    n_tiles = t // tm + N_BUCKETS
    cnt = counts[0, :N_BUCKETS].astype(jnp.int32)
    padded = (cnt + tm - 1) // tm * tm
    ends = jnp.cumsum(padded)
    buckets = jnp.arange(N_BUCKETS, dtype=jnp.int32)
    start_of = jnp.sum(jnp.where(meta[:, 0:1] == buckets[None, :], (ends - padded)[None, :], 0), axis=1)
    pos = start_of + meta[:, 1]
    inv = jnp.zeros((n_tiles * tm,), jnp.int32).at[pos].set(jnp.arange(t, dtype=jnp.int32))
    n_valid = ends[-1] // tm
    tile = jnp.minimum(jnp.arange(n_tiles, dtype=jnp.int32), n_valid - 1)
    bucket = jnp.sum((tile[:, None] * tm >= ends[None, :]).astype(jnp.int32), axis=1)
    group, pair = bucket // N_PAIRS, bucket % N_PAIRS
    ea = jnp.array([0, 0, 0, 1, 1, 2], jnp.int32)[pair]
    eb = jnp.array([1, 2, 3, 2, 3, 3], jnp.int32)[pair]
    base = group * EXPERTS_PER_GROUP
    return pos, inv, base + ea, base + eb, n_valid.reshape(1).astype(jnp.int32)


def _rotate_half_cols(w):
    half = MLA_ROPE_DIM // 2
    return jnp.concatenate([-w[..., half:], w[..., :half]], axis=-1)


def _rope_tables(s_max):
    half = MLA_ROPE_DIM // 2
    inv = ROPE_BASE ** (-jnp.arange(half, dtype=F32) / half)
    ang = jnp.arange(s_max, dtype=F32)[:, None] * inv[None, :]
    pad = jnp.zeros((s_max, LANES - MLA_ROPE_DIM), F32)
    cos, sin = jnp.cos(ang), jnp.sin(ang)
    return (jnp.concatenate([cos, cos, pad], axis=-1), jnp.concatenate([sin, sin, pad], axis=-1))


def _layer(x_prompt, x_sample, c_prompt, c_sample, layer_idx, w_ada, b_ada, g_pre_mix, g_post_mix, w_in,
           lambda_q1, lambda_k1, lambda_q2, lambda_k2, g_diff_sub, g_q_a, w_uq, g_kv_a, w_ukv, w_o,
           g_pre_ffn, g_post_ffn, w_router_group, b_router_group, w_router_expert, b_router_expert,
           w_gate, w_up, w_down, tm=256):
    bp, sp, d = x_prompt.shape
    bs, ss, _ = x_sample.shape
    rows = _Rows(bp, sp, bs, ss, tm)
    xp = x_prompt.reshape(bp * sp, d)
    xs = x_sample.reshape(bs * ss, d)
    lambda_init = 0.8 - 0.6 * math.exp(-0.3 * layer_idx)

    nb = bp + bs
    nb_pad = -(-nb // 8) * 8
    c_all = jnp.concatenate([c_prompt, c_sample, jnp.zeros((nb_pad - nb, d), F32)], axis=0)
    mod = _ada(c_all, w_ada, b_ada.reshape(1, -1)).reshape(nb_pad, N_MOD, d)

    w_kr = w_in[:, C_KR:C_KR + MLA_ROPE_DIM]
    w_kr = jnp.concatenate([w_kr, _rotate_half_cols(w_kr)], axis=1).astype(BF16)
    wq3 = w_uq.reshape(Q_LORA_RANK, MLA_HEADS, MLA_NOPE_DIM + MLA_ROPE_DIM)
    wq_rope = wq3[..., MLA_NOPE_DIM:]
    w_q = jnp.concatenate([wq3[..., :MLA_NOPE_DIM], wq_rope, _rotate_half_cols(wq_rope)], axis=-1)
    w_q = w_q.reshape(Q_LORA_RANK, MLA_HEADS * MLA_QK_PAD).astype(BF16)
    w_kv = w_ukv.astype(BF16)
    cos_t, sin_t = _rope_tables(max(sp, ss))

    dq, dk, dv, qm, km, vm = _inproj(rows, xp, xs, mod, g_pre_mix.reshape(1, d), w_in.astype(BF16), w_kr,
                                     g_q_a.reshape(1, -1), w_q, g_kv_a.reshape(1, -1), w_kv, cos_t, sin_t)

    slopes = jnp.array([2.0 ** (-8.0 * (i + 1) / DIFF_HEADS) for i in range(DIFF_HEADS)], dtype=F32)
    lam_p = jnp.zeros((8, LANES), F32).at[0:4, 0:DIFF_HEAD_DIM].set(
        jnp.stack([lambda_q1, lambda_k1, lambda_q2, lambda_k2]))
    g_sub = g_diff_sub.reshape(1, -1)
    oa_p = _diff_attn(dq, dk, dv, slopes, lam_p, g_sub, 0, bp, sp, lambda_init)
    oa_s = _diff_attn(dq, dk, dv, slopes, lam_p, g_sub, rows.tp, bs, ss, lambda_init)
    ob_p = _mla_attn(qm, km, vm, 0, bp, sp)
    ob_s = _mla_attn(qm, km, vm, rows.tp, bs, ss)

    w_r = jnp.concatenate([w_router_group, w_router_expert.reshape(d, N_EXPERTS),
                           jnp.zeros((d, LANES - N_GROUPS - N_EXPERTS), F32)], axis=1)
    b_r = jnp.concatenate([b_router_group, b_router_expert.reshape(N_EXPERTS),
                           jnp.zeros((LANES - N_GROUPS - N_EXPERTS,), F32)]).reshape(1, LANES)
    x_mid, hx, meta, counts = _oproj(rows, oa_p, oa_s, ob_p, ob_s, xp, xs, mod, w_o.astype(BF16),
                                     g_post_mix.reshape(1, d), g_pre_ffn.reshape(1, d), w_r, b_r)

    pos, inv, tile_ea, tile_eb, n_valid = _moe_plan(meta, counts, MOE_TM)
    f_sorted = _experts(hx, inv.reshape(-1, 1, MOE_TM), tile_ea, tile_eb, n_valid,
                        w_gate.astype(BF16), w_up.astype(BF16), w_down.astype(BF16))
    yp, ys = _final(rows, pos.reshape(rows.n, 1, tm), f_sorted, x_mid, mod, g_post_ffn.reshape(1, d))
    return yp.reshape(bp, sp, d), ys.reshape(bs, ss, d)


def kernel(x_prompt, x_sample, c_prompt, c_sample, w_ada, b_ada, g_pre_mix, g_post_mix, w_in, lambda_q1,
           lambda_k1, lambda_q2, lambda_k2, g_diff_sub, g_q_a, w_uq, g_kv_a, w_ukv, w_o, g_pre_ffn,
           g_post_ffn, w_router_group, b_router_group, w_router_expert, b_router_expert, w_gate, w_up,
           w_down):
    for l in range(w_ada.shape[0]):
        x_prompt, x_sample = _layer(
            x_prompt, x_sample, c_prompt, c_sample, l, w_ada[l], b_ada[l], g_pre_mix[l], g_post_mix[l],
            w_in[l], lambda_q1[l], lambda_k1[l], lambda_q2[l], lambda_k2[l], g_diff_sub[l], g_q_a[l],
            w_uq[l], g_kv_a[l], w_ukv[l], w_o[l], g_pre_ffn[l], g_post_ffn[l], w_router_group[l],
            b_router_group[l], w_router_expert[l], b_router_expert[l], w_gate[l], w_up[l], w_down[l])
    return x_prompt, x_sample
```

```python
import functools
import math

import jax
import jax.numpy as jnp
from jax import lax
from jax.experimental import pallas as pl
from jax.experimental.pallas import tpu as pltpu

F32 = jnp.float32
BF16 = jnp.bfloat16

D_MODEL = 2048
DIFF_HEADS = 8
DIFF_HEAD_DIM = 64
DIFF_WIDTH = DIFF_HEADS * 2 * DIFF_HEAD_DIM
MLA_HEADS = 8
MLA_NOPE_DIM = 128
MLA_ROPE_DIM = 64
MLA_V_DIM = 128
MLA_QK_PAD = 256
V_AUG = 256
Q_LORA_RANK = 512
KV_LORA_RANK = 256
ROPE_BASE = 10000.0
D_MLA_OUT = MLA_HEADS * MLA_V_DIM
N_GROUPS = 4
EXPERTS_PER_GROUP = 4
N_EXPERTS = N_GROUPS * EXPERTS_PER_GROUP
D_EXPERT = 512
N_MOD = 6
NORM_EPS = 1e-6
LANES = 128
ROUTER_LANE0 = N_GROUPS
N_PAIRS = 6
N_BUCKETS = N_GROUPS * N_PAIRS
HX_WIDTH = D_MODEL + LANES
MOE_TM = 256

C_DQ, C_DK, C_DV = 0, DIFF_WIDTH, 2 * DIFF_WIDTH
C_CQ = 3 * DIFF_WIDTH
C_CKV = C_CQ + Q_LORA_RANK
C_KR = C_CKV + KV_LORA_RANK
D_IN_PROJ = C_KR + MLA_ROPE_DIM

VMEM_LIMIT = 56 * 1024 * 1024
LOG2E = 1.4426950408889634
ATTN_TQ = 256
ATTN_TK = 512
ATTN_TQ_OUTER = 1024
ATTN_TKB = 2048
ATTN_UNROLL = 2
ATTN_GROUP = 4
ATTN_SCORE_BYTES = 16 * 1024 * 1024


def _rms(x, g):
    return x * lax.rsqrt(jnp.mean(x * x, axis=-1, keepdims=True) + NORM_EPS) * g


def _dot(a, b):
    return jnp.dot(a, b, preferred_element_type=F32)


def _dot_nt(a, b):
    return lax.dot_general(a, b, (((1,), (1,)), ((), ())), preferred_element_type=F32)


class _Rows:
    def __init__(self, bp, sp, bs, ss, tm):
        assert sp % tm == 0 and ss % tm == 0
        self.bp, self.sp, self.bs, self.ss, self.tm = bp, sp, bs, ss, tm
        self.n_p = bp * sp // tm
        self.n_s = bs * ss // tm
        self.n = self.n_p + self.n_s
        self.tp = bp * sp
        self.ts = bs * ss
        self.t = self.tp + self.ts

    def p_idx(self, i):
        return jnp.minimum(i, self.n_p - 1)

    def s_idx(self, i):
        return jnp.maximum(i - self.n_p, 0)

    def seq(self, i):
        return jnp.where(i < self.n_p, i // (self.sp // self.tm),
                         self.bp + (i - self.n_p) // (self.ss // self.tm))

    def pos(self, i):
        return jnp.where(i < self.n_p, i % (self.sp // self.tm),
                         (i - self.n_p) % (self.ss // self.tm))


def _ada_kernel(c_ref, w_ref, b_ref, o_ref):
    c = c_ref[...]
    a = c / (1.0 + jnp.exp(-c))
    o_ref[...] = jnp.dot(a, w_ref[...], preferred_element_type=F32,
                         precision=lax.Precision.HIGHEST) + b_ref[...]


def _ada(c_all, w_ada, b_ada):
    nb, d = c_all.shape
    n = w_ada.shape[1]
    tn = 1024
    return pl.pallas_call(
        _ada_kernel,
        grid=(n // tn,),
        in_specs=[pl.BlockSpec((nb, d), lambda j: (0, 0)),
                  pl.BlockSpec((d, tn), lambda j: (0, j)),
                  pl.BlockSpec((1, tn), lambda j: (0, j))],
        out_specs=pl.BlockSpec((nb, tn), lambda j: (0, j)),
        out_shape=jax.ShapeDtypeStruct((nb, n), F32),
        compiler_params=pltpu.CompilerParams(dimension_semantics=("arbitrary",),
                                             vmem_limit_bytes=VMEM_LIMIT),
        name="ada",
    )(c_all, w_ada, b_ada)


def _rope_half(x, cos, sin):
    return x * cos + pltpu.roll(x, MLA_ROPE_DIM, 1) * sin


def _inproj_kernel(n_p, xp_ref, xs_ref, mod_ref, gpre_ref, win_ref, wkr_ref, gqa_ref, wq_ref, gkva_ref,
                   wkv_ref, cos_ref, sin_ref, dq_ref, dk_ref, dv_ref, qm_ref, km_ref, vm_ref):
    i = pl.program_id(0)
    x = jnp.where(i < n_p, xp_ref[...], xs_ref[...])
    shift = mod_ref[0, 0:1, :]
    scale = mod_ref[0, 1:2, :]
    hb = (_rms(x, gpre_ref[...]) * (1.0 + scale) + shift).astype(BF16)

    dq_ref[...] = (_dot(hb, win_ref[:, C_DQ:C_DQ + DIFF_WIDTH]) * (DIFF_HEAD_DIM ** -0.5 * LOG2E)).astype(BF16)
    dk_ref[...] = _dot(hb, win_ref[:, C_DK:C_DK + DIFF_WIDTH]).astype(BF16)
    dv_ref[...] = _dot(hb, win_ref[:, C_DV:C_DV + DIFF_WIDTH]).astype(BF16)

    cos = cos_ref[...]
    sin = sin_ref[...]
    cq = _dot(hb, win_ref[:, C_CQ:C_CQ + Q_LORA_RANK])
    ckv = _dot(hb, win_ref[:, C_CKV:C_CKV + KV_LORA_RANK])
    kr = _rope_half(_dot(hb, wkr_ref[...]), cos, sin).astype(BF16)

    q = _dot(_rms(cq, gqa_ref[...]).astype(BF16), wq_ref[...])
    qscale = (MLA_NOPE_DIM + MLA_ROPE_DIM) ** -0.5 * LOG2E
    for h in range(MLA_HEADS):
        c0 = h * MLA_QK_PAD
        qm_ref[:, c0:c0 + LANES] = (q[:, c0:c0 + LANES] * qscale).astype(BF16)
        qm_ref[:, c0 + LANES:c0 + 2 * LANES] = (
            _rope_half(q[:, c0 + LANES:c0 + 2 * LANES], cos, sin) * qscale).astype(BF16)

    kv = _dot(_rms(ckv, gkva_ref[...]).astype(BF16), wkv_ref[...])
    for h in range(MLA_HEADS):
        c0 = h * MLA_QK_PAD
        km_ref[:, c0:c0 + LANES] = kv[:, c0:c0 + LANES].astype(BF16)
        km_ref[:, c0 + LANES:c0 + 2 * LANES] = kr
        vm_ref[:, h * MLA_V_DIM:(h + 1) * MLA_V_DIM] = kv[:, c0 + LANES:c0 + 2 * LANES].astype(BF16)


def _inproj(rows, xp, xs, mod, g_pre, w_in, w_kr, g_q_a, w_q, g_kv_a, w_kv, cos_t, sin_t):
    tm, d = rows.tm, D_MODEL
    const = lambda i: (0, 0)
    row = lambda i: (i, 0)
    one = pl.Buffered(1)
    out_w = (DIFF_WIDTH, DIFF_WIDTH, DIFF_WIDTH, MLA_HEADS * MLA_QK_PAD, MLA_HEADS * MLA_QK_PAD, D_MLA_OUT)
    return pl.pallas_call(
        functools.partial(_inproj_kernel, rows.n_p),
        grid=(rows.n,),
        in_specs=[pl.BlockSpec((tm, d), lambda i: (rows.p_idx(i), 0)),
                  pl.BlockSpec((tm, d), lambda i: (rows.s_idx(i), 0)),
                  pl.BlockSpec((1, N_MOD, d), lambda i: (rows.seq(i), 0, 0)),
                  pl.BlockSpec((1, d), const),
                  pl.BlockSpec((d, D_IN_PROJ), const, pipeline_mode=one),
                  pl.BlockSpec((d, 2 * MLA_ROPE_DIM), const),
                  pl.BlockSpec((1, Q_LORA_RANK), const),
                  pl.BlockSpec((Q_LORA_RANK, MLA_HEADS * MLA_QK_PAD), const, pipeline_mode=one),
                  pl.BlockSpec((1, KV_LORA_RANK), const),
                  pl.BlockSpec((KV_LORA_RANK, MLA_HEADS * MLA_QK_PAD), const, pipeline_mode=one),
                  pl.BlockSpec((tm, LANES), lambda i: (rows.pos(i), 0)),
                  pl.BlockSpec((tm, LANES), lambda i: (rows.pos(i), 0))],
        out_specs=[pl.BlockSpec((tm, w), row) for w in out_w],
        out_shape=[jax.ShapeDtypeStruct((rows.t, w), BF16) for w in out_w],
        compiler_params=pltpu.CompilerParams(dimension_semantics=("arbitrary",),
                                             vmem_limit_bytes=VMEM_LIMIT),
        name="inproj",
    )(xp, xs, mod, g_pre, w_in, w_kr, g_q_a, w_q, g_kv_a, w_kv, cos_t, sin_t)


def _lane_groups(x, op):
    out = x[:, 0:LANES]
    for g in range(1, x.shape[1] // LANES):
        out = op(out, x[:, g * LANES:(g + 1) * LANES])
    return out


def _aligned(x, m):
    return x if isinstance(x, int) else pl.multiple_of(x, m)


def _chunk(ref, j, tk):
    return ref[pl.ds(_aligned(j * tk, tk), tk), :]


def _score_block(unit, s_view, jb, cpb, tk, mpart):
    score_fn, shift_fn = unit
    for c in range(cpb):
        j = jb * cpb + c
        s = score_fn(j)
        s_view[jb, :, c * tk:(c + 1) * tk] = s
        part = _lane_groups(s, jnp.maximum)
        if shift_fn is not None:
            part = part + shift_fn(j)
        mpart = jnp.maximum(mpart, part)
    return mpart


def _value_block(unit, m, v_ref, s_view, jb, cpb, tk, acc):
    _, shift_fn = unit
    ps = []
    for c in range(cpb):
        j = jb * cpb + c
        mj = m if shift_fn is None else m - shift_fn(j)
        ps.append(jnp.exp2(s_view[jb, :, c * tk:(c + 1) * tk] - mj).astype(BF16))
    p_all = ps[0] if cpb == 1 else jnp.concatenate(ps, axis=1)
    return acc + _dot(p_all, _chunk(v_ref, jb, cpb * tk))


def _attn_phases(geom, v_ref, score=(), value=()):
    tq, tk, cpb, nkb = geom
    zeros = jnp.zeros((tq, 2 * LANES), F32)
    neg_inf = jnp.full((tq, LANES), -jnp.inf, F32)

    def body(jb, carry):
        mparts, accs = carry
        mparts = tuple(_score_block(u, view, jb, cpb, tk, mp) for (u, view), mp in zip(score, mparts))
        accs = tuple(_value_block(u, m, v_ref, view, jb, cpb, tk, acc) for (u, m, view), acc in zip(value, accs))
        return mparts, accs

    init = ((neg_inf,) * len(score), (zeros,) * len(value))
    mparts, accs = lax.fori_loop(0, nkb, body, init, unroll=min(ATTN_UNROLL, nkb))
    return ([jnp.max(mp, axis=-1, keepdims=True) for mp in mparts],
            [acc[:, 0:LANES] / acc[:, LANES:LANES + 1] for acc in accs])


def _fill_value_rows(v_ref, vaug):
    lane = lax.broadcasted_iota(jnp.int32, v_ref.shape, 1)
    vaug[:, 0:LANES] = v_ref[...]
    vaug[:, LANES:2 * LANES] = jnp.where(lane == 0, 1.0, 0.0).astype(BF16)


def _attn_grouped(geom, v_ref, units, s_buf):
    views = [s_buf.at[k] for k in range(len(units))]
    ms, _ = _attn_phases(geom, v_ref, score=tuple(zip(units, views)))
    _, outs = _attn_phases(geom, v_ref, value=tuple(zip(units, ms, views)))
    return outs


def _diff_attn_kernel(lambda_init, geom, n_sub, slopes_ref, lam_ref, gsub_ref, q_ref, k_ref, v_ref,
                      o_ref, s_buf, bias_buf, vaug):
    tq, tk, _, _ = geom
    h = pl.program_id(1)
    qo = pl.program_id(2)
    slope = slopes_ref[h] * LOG2E
    n_cross = tk // tq

    @pl.when(qo == 0)
    def _():
        _fill_value_rows(v_ref, vaug)


    r_minus_c = (lax.broadcasted_iota(jnp.int32, (tq, tk), 0)
                 - lax.broadcasted_iota(jnp.int32, (tq, tk), 1)).astype(F32)
    bias_buf[0] = r_minus_c * (-slope)
    bias_buf[1] = r_minus_c * slope
    for c in range(n_cross):
        bias_buf[2 + c] = jnp.abs(r_minus_c + float(c * tq)) * (-slope)

    lp = lam_ref[...]
    lam = (jnp.exp(jnp.sum(lp[0:1] * lp[1:2], axis=-1, keepdims=True))
           - jnp.exp(jnp.sum(lp[2:3] * lp[3:4], axis=-1, keepdims=True)) + lambda_init)
    gsub = gsub_ref[...]

    def unit(sub, half):
        qbase = (qo * n_sub + sub) * tq
        jc = qbase // tk
        cross = 2 + (qbase - jc * tk) // tq
        q = q_ref[pl.ds(_aligned(sub * tq, tq), tq), :]
        lane = lax.broadcasted_iota(jnp.int32, q.shape, 1)
        keep = (lane < DIFF_HEAD_DIM) if half == 0 else (lane >= DIFF_HEAD_DIM)
        qh = jnp.where(keep, q, jnp.zeros_like(q))
        shift = lambda j: jnp.where(j == jc, 0.0, -slope * jnp.abs(qbase - j * tk).astype(F32))
        which = lambda j: jnp.where(j < jc, 0, jnp.where(j > jc, 1, cross))
        return (lambda j: _dot_nt(qh, _chunk(k_ref, j, tk)) + bias_buf[which(j)]), shift

    group = s_buf.shape[0] // 2

    def grouped(g, carry):
        subs = [g * group + k for k in range(group)]
        outs = _attn_grouped(geom, vaug, [unit(sub, half) for sub in subs for half in range(2)], s_buf)
        for k, sub in enumerate(subs):
            o = outs[2 * k] - lam * outs[2 * k + 1]
            o_ref[pl.ds(_aligned(sub * tq, tq), tq), :] = (_rms(o, gsub) * (1.0 - lambda_init)).astype(BF16)
        return carry

    lax.fori_loop(0, n_sub // group, grouped, 0)


def _attn_tiles(s, row0, units_per_sub):
    tq = min(ATTN_TQ, s)
    tk = min(ATTN_TK, s)
    tkb = min(ATTN_TKB, s)
    tq_outer = min(ATTN_TQ_OUTER, s)
    assert s % tq_outer == 0 and tq_outer % tq == 0 and s % tkb == 0 and tkb % tk == 0 and tk % tq == 0
    assert row0 % s == 0
    nkb, n_sub = s // tkb, tq_outer // tq
    units = max(units_per_sub, min(ATTN_GROUP, ATTN_SCORE_BYTES // (tq * s * 4)))
    group = min(units // units_per_sub, n_sub)
    assert n_sub % group == 0
    return (tq, tk, tkb // tk, nkb), tq_outer, (group * units_per_sub, nkb, tq, tkb)


def _diff_attn(dq, dk, dv, slopes, lam_p, g_sub, row0, b, s, lambda_init):
    geom, tq_outer, s_buf_shape = _attn_tiles(s, row0, 2)
    tq, tk = geom[0], geom[1]
    n_sub, nqo = tq_outer // tq, s // tq_outer
    dh = 2 * DIFF_HEAD_DIM
    return pl.pallas_call(
        functools.partial(_diff_attn_kernel, lambda_init, geom, n_sub),
        grid=(b, DIFF_HEADS, nqo),
        in_specs=[pl.BlockSpec(memory_space=pltpu.SMEM),
                  pl.BlockSpec((8, LANES), lambda bi, h, qi: (0, 0)),
                  pl.BlockSpec((1, dh), lambda bi, h, qi: (0, 0)),
                  pl.BlockSpec((tq_outer, dh), lambda bi, h, qi: (row0 // tq_outer + bi * nqo + qi, h)),
                  pl.BlockSpec((s, dh), lambda bi, h, qi: (row0 // s + bi, h)),
                  pl.BlockSpec((s, dh), lambda bi, h, qi: (row0 // s + bi, h))],
        out_specs=pl.BlockSpec((tq_outer, dh), lambda bi, h, qi: (bi * nqo + qi, h)),
        out_shape=jax.ShapeDtypeStruct((b * s, DIFF_WIDTH), BF16),
        scratch_shapes=[pltpu.VMEM(s_buf_shape, F32), pltpu.VMEM((2 + tk // tq, tq, tk), F32),
                        pltpu.VMEM((s, V_AUG), BF16)],
        compiler_params=pltpu.CompilerParams(
            dimension_semantics=("arbitrary", "arbitrary", "arbitrary"), vmem_limit_bytes=VMEM_LIMIT),
        name="diff_attn",
    )(slopes, lam_p, g_sub, dq, dk, dv)


def _mla_attn_kernel(geom, n_sub, q_ref, k_ref, v_ref, o_ref, s_buf, vaug):
    tq, tk, _, _ = geom

    @pl.when(pl.program_id(2) == 0)
    def _():
        _fill_value_rows(v_ref, vaug)

    def unit(sub):
        q = q_ref[pl.ds(_aligned(sub * tq, tq), tq), :]
        return (lambda j: _dot_nt(q, _chunk(k_ref, j, tk))), None

    group = s_buf.shape[0]

    def grouped(g, carry):
        subs = [g * group + k for k in range(group)]
        for sub, o in zip(subs, _attn_grouped(geom, vaug, [unit(sub) for sub in subs], s_buf)):
            o_ref[pl.ds(_aligned(sub * tq, tq), tq), :] = o.astype(BF16)
        return carry

    lax.fori_loop(0, n_sub // group, grouped, 0)


def _mla_attn(qm, km, vm, row0, b, s):
    geom, tq_outer, s_buf_shape = _attn_tiles(s, row0, 1)
    n_sub, nqo = tq_outer // geom[0], s // tq_outer
    return pl.pallas_call(
        functools.partial(_mla_attn_kernel, geom, n_sub),
        grid=(b, MLA_HEADS, nqo),
        in_specs=[pl.BlockSpec((tq_outer, MLA_QK_PAD), lambda bi, h, qi: (row0 // tq_outer + bi * nqo + qi, h)),
                  pl.BlockSpec((s, MLA_QK_PAD), lambda bi, h, qi: (row0 // s + bi, h)),
                  pl.BlockSpec((s, MLA_V_DIM), lambda bi, h, qi: (row0 // s + bi, h))],
        out_specs=pl.BlockSpec((tq_outer, MLA_V_DIM), lambda bi, h, qi: (bi * nqo + qi, h)),
        out_shape=jax.ShapeDtypeStruct((b * s, D_MLA_OUT), BF16),
        scratch_shapes=[pltpu.VMEM(s_buf_shape, F32), pltpu.VMEM((s, V_AUG), BF16)],
        compiler_params=pltpu.CompilerParams(
            dimension_semantics=("arbitrary", "arbitrary", "arbitrary"), vmem_limit_bytes=VMEM_LIMIT),
        name="mla_attn",
    )(qm, km, vm)


def _route(logits):
    lane = lax.broadcasted_iota(jnp.int32, logits.shape, 1)
    neg = jnp.full(logits.shape, -jnp.inf, F32)
    big = jnp.full(logits.shape, LANES, jnp.int32)
    first = lambda mask: jnp.min(jnp.where(mask, lane, big), axis=-1, keepdims=True)

    gl = jnp.where(lane < N_GROUPS, logits, neg)
    gmax = jnp.max(gl, axis=-1, keepdims=True)
    g_idx = first(gl == gmax)
    g_w = 1.0 / jnp.sum(jnp.exp(gl - gmax), axis=-1, keepdims=True)

    lo = ROUTER_LANE0 + EXPERTS_PER_GROUP * g_idx
    el = jnp.where(lane >= lo, jnp.where(lane < lo + EXPERTS_PER_GROUP, logits, neg), neg)
    v1 = jnp.max(el, axis=-1, keepdims=True)
    i1 = first(el == v1)
    el2 = jnp.where(lane == i1, neg, el)
    v2 = jnp.max(el2, axis=-1, keepdims=True)
    i2 = first(el2 == v2)
    t = jnp.exp(v2 - v1)
    w1 = g_w / (1.0 + t)
    w2 = w1 * t
    first_low = i1 < i2
    ea = jnp.minimum(i1, i2) - lo
    eb = jnp.maximum(i1, i2) - lo
    pair = jnp.where(ea == 0, 0, jnp.where(ea == 1, 3, 5)) + eb - ea - 1
    bucket = g_idx * N_PAIRS + pair
    return bucket, jnp.where(first_low, w1, w2), jnp.where(first_low, w2, w1)


def _oproj_kernel(n_p, oap_ref, oas_ref, obp_ref, obs_ref, xp_ref, xs_ref, mod_ref, wo_ref, gpost_ref,
                  gpre_ref, wr_ref, br_ref, xmid_ref, hx_ref, meta_ref, counts_ref, cnt):
    i = pl.program_id(0)

    @pl.when(i == 0)
    def _():
        cnt[...] = jnp.zeros(cnt.shape, F32)

    is_p = i < n_p
    oa = jnp.where(is_p, oap_ref[...], oas_ref[...])
    ob = jnp.where(is_p, obp_ref[...], obs_ref[...])
    x = jnp.where(is_p, xp_ref[...], xs_ref[...])
    mix = _dot(oa, wo_ref[0:DIFF_WIDTH, :]) + _dot(ob, wo_ref[DIFF_WIDTH:DIFF_WIDTH + D_MLA_OUT, :])
    x_mid = x + mod_ref[0, 2:3, :] * _rms(mix, gpost_ref[...])
    xmid_ref[...] = x_mid
    h2 = _rms(x_mid, gpre_ref[...]) * (1.0 + mod_ref[0, 4:5, :]) + mod_ref[0, 3:4, :]
    w_r = wr_ref[...]
    w_hi = w_r.astype(BF16)
    w_hi_lo = jnp.concatenate([w_hi, (w_r - w_hi.astype(F32)).astype(BF16)], axis=1)
    h_hi = h2.astype(BF16)
    h_lo = (h2 - h_hi.astype(F32)).astype(BF16)
    hi_terms = _dot(h_hi, w_hi_lo)
    logits = hi_terms[:, 0:LANES] + hi_terms[:, LANES:2 * LANES] + _dot(h_lo, w_hi) + br_ref[...]
    bucket, wa, wb = _route(logits)

    tm = logits.shape[0]
    lane = lax.broadcasted_iota(jnp.int32, logits.shape, 1)
    hx_ref[:, 0:D_MODEL] = h2
    hx_ref[:, D_MODEL:HX_WIDTH] = jnp.where(lane == 0, wa, jnp.where(lane == 1, wb, jnp.zeros_like(logits)))

    onehot = jnp.where(lane == bucket, 1.0, 0.0)
    earlier = (lax.broadcasted_iota(jnp.int32, (tm, tm), 0) > lax.broadcasted_iota(jnp.int32, (tm, tm), 1))
    before = _dot(jnp.where(earlier, 1.0, 0.0).astype(BF16), onehot.astype(BF16))
    rank = jnp.sum(onehot * (before + cnt[0:1, :]), axis=-1, keepdims=True)
    cnt[...] = cnt[...] + jnp.sum(onehot, axis=0, keepdims=True)
    meta_ref[...] = jnp.where(lane == 0, bucket, jnp.where(lane == 1, rank.astype(jnp.int32), 0))
    counts_ref[...] = cnt[...]


def _oproj(rows, oa_p, oa_s, ob_p, ob_s, xp, xs, mod, w_o, g_post, g_pre, w_r, b_r):
    tm, d = rows.tm, D_MODEL
    const = lambda i: (0, 0)
    row = lambda i: (i, 0)
    prow = lambda i: (rows.p_idx(i), 0)
    srow = lambda i: (rows.s_idx(i), 0)
    return pl.pallas_call(
        functools.partial(_oproj_kernel, rows.n_p),
        grid=(rows.n,),
        in_specs=[pl.BlockSpec((tm, DIFF_WIDTH), prow), pl.BlockSpec((tm, DIFF_WIDTH), srow),
                  pl.BlockSpec((tm, D_MLA_OUT), prow), pl.BlockSpec((tm, D_MLA_OUT), srow),
                  pl.BlockSpec((tm, d), prow), pl.BlockSpec((tm, d), srow),
                  pl.BlockSpec((1, N_MOD, d), lambda i: (rows.seq(i), 0, 0)),
                  pl.BlockSpec((DIFF_WIDTH + D_MLA_OUT, d), const, pipeline_mode=pl.Buffered(1)),
                  pl.BlockSpec((1, d), const), pl.BlockSpec((1, d), const),
                  pl.BlockSpec((d, LANES), const), pl.BlockSpec((1, LANES), const)],
        out_specs=[pl.BlockSpec((tm, d), row), pl.BlockSpec((tm, HX_WIDTH), row),
                   pl.BlockSpec((tm, LANES), row), pl.BlockSpec((8, LANES), const)],
        out_shape=[jax.ShapeDtypeStruct((rows.t, d), F32), jax.ShapeDtypeStruct((rows.t, HX_WIDTH), F32),
                   jax.ShapeDtypeStruct((rows.t, LANES), jnp.int32), jax.ShapeDtypeStruct((8, LANES), F32)],
        scratch_shapes=[pltpu.VMEM((8, LANES), F32)],
        compiler_params=pltpu.CompilerParams(dimension_semantics=("arbitrary",),
                                             vmem_limit_bytes=VMEM_LIMIT),
        name="oproj",
    )(oa_p, oa_s, ob_p, ob_s, xp, xs, mod, w_o, g_post, g_pre, w_r, b_r)


def _gather_rows(idx_ref, src_hbm, dst, sem, inline=False):
    def body(r, carry):
        pltpu.make_async_copy(src_hbm.at[pl.ds(idx_ref[0, 0, r], 1), :], dst.at[pl.ds(r, 1), :], sem).start()
        return carry

    if inline:
        for r in range(dst.shape[0]):
            body(r, 0)
    else:
        lax.fori_loop(0, dst.shape[0], body, 0, unroll=8)


def _wait_rows(src_hbm, dst, sem):
    pltpu.make_async_copy(src_hbm.at[pl.ds(0, dst.shape[0]), :], dst, sem).wait()


def _experts_kernel(ea_ref, eb_ref, nv_ref, inv_cur, inv_nxt, hx_hbm, wga, wua, wda, wgb, wub, wdb,
                    f_ref, xbuf, sem):
    i = pl.program_id(0)
    n_valid = nv_ref[0]
    slot = i % 2

    @pl.when(i == 0)
    def _():
        _gather_rows(inv_cur, hx_hbm, xbuf.at[0], sem.at[0])

    def compute():
        _wait_rows(hx_hbm, xbuf.at[slot], sem.at[slot])
        x = xbuf[slot, :, 0:D_MODEL].astype(BF16)
        aux = xbuf[slot, :, D_MODEL:HX_WIDTH]

        def hidden(wg, wu, w):
            g = _dot(x, wg[0])
            return ((g / (1.0 + jnp.exp(-g))) * _dot(x, wu[0]) * w).astype(BF16)

        f_ref[...] = (_dot(hidden(wga, wua, aux[:, 0:1]), wda[0])
                      + _dot(hidden(wgb, wub, aux[:, 1:2]), wdb[0]))

    @pl.when(i + 1 < n_valid)
    def _():
        _gather_rows(inv_nxt, hx_hbm, xbuf.at[1 - slot], sem.at[1 - slot], inline=True)
        compute()

    @pl.when(i + 1 == n_valid)
    def _():
        compute()

    @pl.when(i >= n_valid)
    def _():
        f_ref[...] = jnp.zeros(f_ref.shape, F32)


def _experts(hx, inv3, tile_ea, tile_eb, n_valid, w_gate, w_up, w_down):
    n_tiles, _, tm = inv3.shape
    d = D_MODEL
    wspec = lambda shape, which: pl.BlockSpec(
        (1,) + shape, (lambda i, ea, eb, nv: (ea[i], 0, 0)) if which == 0 else (lambda i, ea, eb, nv: (eb[i], 0, 0)))
    smem = lambda imap: pl.BlockSpec((1, 1, tm), imap, memory_space=pltpu.SMEM)
    return pl.pallas_call(
        _experts_kernel,
        grid_spec=pltpu.PrefetchScalarGridSpec(
            num_scalar_prefetch=3,
            grid=(n_tiles,),
            in_specs=[smem(lambda i, ea, eb, nv: (i, 0, 0)),
                      smem(lambda i, ea, eb, nv: (jnp.minimum(i + 1, n_tiles - 1), 0, 0)),
                      pl.BlockSpec(memory_space=pl.ANY),
                      wspec((d, D_EXPERT), 0), wspec((d, D_EXPERT), 0), wspec((D_EXPERT, d), 0),
                      wspec((d, D_EXPERT), 1), wspec((d, D_EXPERT), 1), wspec((D_EXPERT, d), 1)],
            out_specs=pl.BlockSpec((tm, d), lambda i, ea, eb, nv: (i, 0)),
            scratch_shapes=[pltpu.VMEM((2, tm, HX_WIDTH), F32), pltpu.SemaphoreType.DMA((2,))]),
        out_shape=jax.ShapeDtypeStruct((n_tiles * tm, d), F32),
        compiler_params=pltpu.CompilerParams(dimension_semantics=("arbitrary",),
                                             vmem_limit_bytes=VMEM_LIMIT),
        name="experts",
    )(tile_ea, tile_eb, n_valid, inv3, inv3, hx, w_gate, w_up, w_down, w_gate, w_up, w_down)


def _final_kernel(n_p, pos_cur, pos_nxt, f_hbm, xmid_ref, mod_ref, gpost_ref, yp_ref, ys_ref, fbuf, sem):
    i = pl.program_id(0)
    slot = i % 2

    @pl.when(i == 0)
    def _():
        _gather_rows(pos_cur, f_hbm, fbuf.at[0], sem.at[0])

    def compute():
        _wait_rows(f_hbm, fbuf.at[slot], sem.at[slot])
        y = xmid_ref[...] + mod_ref[0, 5:6, :] * _rms(fbuf[slot], gpost_ref[...])

        @pl.when(i < n_p)
        def _():
            yp_ref[...] = y

        @pl.when(i >= n_p)
        def _():
            ys_ref[...] = y

    @pl.when(i + 1 < pl.num_programs(0))
    def _():
        _gather_rows(pos_nxt, f_hbm, fbuf.at[1 - slot], sem.at[1 - slot], inline=True)
        compute()

    @pl.when(i + 1 == pl.num_programs(0))
    def _():
        compute()


def _final(rows, pos3, f_sorted, x_mid, mod, g_post):
    tm, d = rows.tm, D_MODEL
    smem = lambda imap: pl.BlockSpec((1, 1, tm), imap, memory_space=pltpu.SMEM)
    return pl.pallas_call(
        functools.partial(_final_kernel, rows.n_p),
        grid=(rows.n,),
        in_specs=[smem(lambda i: (i, 0, 0)),
                  smem(lambda i: (jnp.minimum(i + 1, rows.n - 1), 0, 0)),
                  pl.BlockSpec(memory_space=pl.ANY),
                  pl.BlockSpec((tm, d), lambda i: (i, 0)),
                  pl.BlockSpec((1, N_MOD, d), lambda i: (rows.seq(i), 0, 0)),
                  pl.BlockSpec((1, d), lambda i: (0, 0))],
        out_specs=[pl.BlockSpec((tm, d), lambda i: (rows.p_idx(i), 0)),
                   pl.BlockSpec((tm, d), lambda i: (rows.s_idx(i), 0))],
        out_shape=[jax.ShapeDtypeStruct((rows.tp, d), F32), jax.ShapeDtypeStruct((rows.ts, d), F32)],
        scratch_shapes=[pltpu.VMEM((2, tm, d), F32), pltpu.SemaphoreType.DMA((2,))],
        compiler_params=pltpu.CompilerParams(dimension_semantics=("arbitrary",),
                                             vmem_limit_bytes=VMEM_LIMIT),
        name="final",
    )(pos3, pos3, f_sorted, x_mid, mod, g_post)


def _moe_plan(meta, counts, tm):
    t = meta.shape[0]
    n_tiles = t // tm + N_BUCKETS
    cnt = counts[0, :N_BUCKETS].astype(jnp.int32)
    padded = (cnt + tm - 1) // tm * tm
    ends = jnp.cumsum(padded)
    buckets = jnp.arange(N_BUCKETS, dtype=jnp.int32)
    start_of = jnp.sum(jnp.where(meta[:, 0:1] == buckets[None, :], (ends - padded)[None, :], 0), axis=1)
    pos = start_of + meta[:, 1]
    inv = jnp.zeros((n_tiles * tm,), jnp.int32).at[pos].set(jnp.arange(t, dtype=jnp.int32))
    n_valid = ends[-1] // tm
    tile = jnp.minimum(jnp.arange(n_tiles, dtype=jnp.int32), n_valid - 1)
    bucket = jnp.sum((tile[:, None] * tm >= ends[None, :]).astype(jnp.int32), axis=1)
    group, pair = bucket // N_PAIRS, bucket % N_PAIRS
    ea = jnp.array([0, 0, 0, 1, 1, 2], jnp.int32)[pair]
    eb = jnp.array([1, 2, 3, 2, 3, 3], jnp.int32)[pair]
    base = group * EXPERTS_PER_GROUP
    return pos, inv, base + ea, base + eb, n_valid.reshape(1).astype(jnp.int32)


def _rotate_half_cols(w):
    half = MLA_ROPE_DIM // 2
    return jnp.concatenate([-w[..., half:], w[..., :half]], axis=-1)


def _rope_tables(s_max):
    half = MLA_ROPE_DIM // 2
    inv = ROPE_BASE ** (-jnp.arange(half, dtype=F32) / half)
    ang = jnp.arange(s_max, dtype=F32)[:, None] * inv[None, :]
    pad = jnp.zeros((s_max, LANES - MLA_ROPE_DIM), F32)
    cos, sin = jnp.cos(ang), jnp.sin(ang)
    return (jnp.concatenate([cos, cos, pad], axis=-1), jnp.concatenate([sin, sin, pad], axis=-1))


def _layer(x_prompt, x_sample, c_prompt, c_sample, layer_idx, w_ada, b_ada, g_pre_mix, g_post_mix, w_in,
           lambda_q1, lambda_k1, lambda_q2, lambda_k2, g_diff_sub, g_q_a, w_uq, g_kv_a, w_ukv, w_o,
           g_pre_ffn, g_post_ffn, w_router_group, b_router_group, w_router_expert, b_router_expert,
           w_gate, w_up, w_down, tm=256):
    bp, sp, d = x_prompt.shape
    bs, ss, _ = x_sample.shape
    rows = _Rows(bp, sp, bs, ss, tm)
    xp = x_prompt.reshape(bp * sp, d)
    xs = x_sample.reshape(bs * ss, d)
    lambda_init = 0.8 - 0.6 * math.exp(-0.3 * layer_idx)

    nb = bp + bs
    nb_pad = -(-nb // 8) * 8
    c_all = jnp.concatenate([c_prompt, c_sample, jnp.zeros((nb_pad - nb, d), F32)], axis=0)
    mod = _ada(c_all, w_ada, b_ada.reshape(1, -1)).reshape(nb_pad, N_MOD, d)

    w_kr = w_in[:, C_KR:C_KR + MLA_ROPE_DIM]
    w_kr = jnp.concatenate([w_kr, _rotate_half_cols(w_kr)], axis=1).astype(BF16)
    wq3 = w_uq.reshape(Q_LORA_RANK, MLA_HEADS, MLA_NOPE_DIM + MLA_ROPE_DIM)
    wq_rope = wq3[..., MLA_NOPE_DIM:]
    w_q = jnp.concatenate([wq3[..., :MLA_NOPE_DIM], wq_rope, _rotate_half_cols(wq_rope)], axis=-1)
    w_q = w_q.reshape(Q_LORA_RANK, MLA_HEADS * MLA_QK_PAD).astype(BF16)
    w_kv = w_ukv.astype(BF16)
    cos_t, sin_t = _rope_tables(max(sp, ss))

    dq, dk, dv, qm, km, vm = _inproj(rows, xp, xs, mod, g_pre_mix.reshape(1, d), w_in.astype(BF16), w_kr,
                                     g_q_a.reshape(1, -1), w_q, g_kv_a.reshape(1, -1), w_kv, cos_t, sin_t)

    slopes = jnp.array([2.0 ** (-8.0 * (i + 1) / DIFF_HEADS) for i in range(DIFF_HEADS)], dtype=F32)
    lam_p = jnp.zeros((8, LANES), F32).at[0:4, 0:DIFF_HEAD_DIM].set(
        jnp.stack([lambda_q1, lambda_k1, lambda_q2, lambda_k2]))
    g_sub = g_diff_sub.reshape(1, -1)
    oa_p = _diff_attn(dq, dk, dv, slopes, lam_p, g_sub, 0, bp, sp, lambda_init)
    oa_s = _diff_attn(dq, dk, dv, slopes, lam_p, g_sub, rows.tp, bs, ss, lambda_init)
    ob_p = _mla_attn(qm, km, vm, 0, bp, sp)
    ob_s = _mla_attn(qm, km, vm, rows.tp, bs, ss)

    w_r = jnp.concatenate([w_router_group, w_router_expert.reshape(d, N_EXPERTS),
                           jnp.zeros((d, LANES - N_GROUPS - N_EXPERTS), F32)], axis=1)
    b_r = jnp.concatenate([b_router_group, b_router_expert.reshape(N_EXPERTS),
                           jnp.zeros((LANES - N_GROUPS - N_EXPERTS,), F32)]).reshape(1, LANES)
    x_mid, hx, meta, counts = _oproj(rows, oa_p, oa_s, ob_p, ob_s, xp, xs, mod, w_o.astype(BF16),
                                     g_post_mix.reshape(1, d), g_pre_ffn.reshape(1, d), w_r, b_r)

    pos, inv, tile_ea, tile_eb, n_valid = _moe_plan(meta, counts, MOE_TM)
    f_sorted = _experts(hx, inv.reshape(-1, 1, MOE_TM), tile_ea, tile_eb, n_valid,
                        w_gate.astype(BF16), w_up.astype(BF16), w_down.astype(BF16))
    yp, ys = _final(rows, pos.reshape(rows.n, 1, tm), f_sorted, x_mid, mod, g_post_ffn.reshape(1, d))
    return yp.reshape(bp, sp, d), ys.reshape(bs, ss, d)


def kernel(x_prompt, x_sample, c_prompt, c_sample, w_ada, b_ada, g_pre_mix, g_post_mix, w_in, lambda_q1,
           lambda_k1, lambda_q2, lambda_k2, g_diff_sub, g_q_a, w_uq, g_kv_a, w_ukv, w_o, g_pre_ffn,
           g_post_ffn, w_router_group, b_router_group, w_router_expert, b_router_expert, w_gate, w_up,
           w_down):
    for l in range(w_ada.shape[0]):
        x_prompt, x_sample = _layer(
            x_prompt, x_sample, c_prompt, c_sample, l, w_ada[l], b_ada[l], g_pre_mix[l], g_post_mix[l],
            w_in[l], lambda_q1[l], lambda_k1[l], lambda_q2[l], lambda_k2[l], g_diff_sub[l], g_q_a[l],
            w_uq[l], g_kv_a[l], w_ukv[l], w_o[l], g_pre_ffn[l], g_post_ffn[l], w_router_group[l],
            b_router_group[l], w_router_expert[l], b_router_expert[l], w_gate[l], w_up[l], w_down[l])
    return x_prompt, x_sample
```

```python
import functools
import math

import jax
import jax.numpy as jnp
from jax import lax
from jax.experimental import pallas as pl
from jax.experimental.pallas import tpu as pltpu

F32 = jnp.float32
BF16 = jnp.bfloat16

D_MODEL = 2048
DIFF_HEADS = 8
DIFF_HEAD_DIM = 64
DIFF_WIDTH = DIFF_HEADS * 2 * DIFF_HEAD_DIM
MLA_HEADS = 8
MLA_NOPE_DIM = 128
MLA_ROPE_DIM = 64
MLA_V_DIM = 128
MLA_QK_PAD = 256
V_AUG = 256
Q_LORA_RANK = 512
KV_LORA_RANK = 256
ROPE_BASE = 10000.0
D_MLA_OUT = MLA_HEADS * MLA_V_DIM
N_GROUPS = 4
EXPERTS_PER_GROUP = 4
N_EXPERTS = N_GROUPS * EXPERTS_PER_GROUP
D_EXPERT = 512
N_MOD = 6
NORM_EPS = 1e-6
LANES = 128
ROUTER_LANE0 = N_GROUPS
N_PAIRS = 6
N_BUCKETS = N_GROUPS * N_PAIRS
HX_WIDTH = D_MODEL + LANES
MOE_TM = 256
GATHER_SLICES = 8

C_DQ, C_DK, C_DV = 0, DIFF_WIDTH, 2 * DIFF_WIDTH
C_CQ = 3 * DIFF_WIDTH
C_CKV = C_CQ + Q_LORA_RANK
C_KR = C_CKV + KV_LORA_RANK
D_IN_PROJ = C_KR + MLA_ROPE_DIM

VMEM_LIMIT = 56 * 1024 * 1024
LOG2E = 1.4426950408889634
ATTN_TQ = 256
ATTN_TK = 512
ATTN_TQ_OUTER = 1024
ATTN_TKB = 2048
ATTN_UNROLL = 2
ATTN_GROUP = 4
ATTN_SCORE_BYTES = 16 * 1024 * 1024


def _rms(x, g):
    return x * lax.rsqrt(jnp.mean(x * x, axis=-1, keepdims=True) + NORM_EPS) * g


def _dot(a, b):
    return jnp.dot(a, b, preferred_element_type=F32)


def _dot_nt(a, b):
    return lax.dot_general(a, b, (((1,), (1,)), ((), ())), preferred_element_type=F32)


class _Rows:
    def __init__(self, bp, sp, bs, ss, tm):
        assert sp % tm == 0 and ss % tm == 0
        self.bp, self.sp, self.bs, self.ss, self.tm = bp, sp, bs, ss, tm
        self.n_p = bp * sp // tm
        self.n_s = bs * ss // tm
        self.n = self.n_p + self.n_s
        self.tp = bp * sp
        self.ts = bs * ss
        self.t = self.tp + self.ts

    def p_idx(self, i):
        return jnp.minimum(i, self.n_p - 1)

    def s_idx(self, i):
        return jnp.maximum(i - self.n_p, 0)

    def seq(self, i):
        return jnp.where(i < self.n_p, i // (self.sp // self.tm),
                         self.bp + (i - self.n_p) // (self.ss // self.tm))

    def pos(self, i):
        return jnp.where(i < self.n_p, i % (self.sp // self.tm),
                         (i - self.n_p) % (self.ss // self.tm))


def _ada_kernel(c_ref, w_ref, b_ref, o_ref):
    c = c_ref[...]
    a = c / (1.0 + jnp.exp(-c))
    o_ref[...] = jnp.dot(a, w_ref[...], preferred_element_type=F32,
                         precision=lax.Precision.HIGHEST) + b_ref[...]


def _ada(c_all, w_ada, b_ada):
    nb, d = c_all.shape
    n = w_ada.shape[1]
    tn = 1024
    return pl.pallas_call(
        _ada_kernel,
        grid=(n // tn,),
        in_specs=[pl.BlockSpec((nb, d), lambda j: (0, 0)),
                  pl.BlockSpec((d, tn), lambda j: (0, j)),
                  pl.BlockSpec((1, tn), lambda j: (0, j))],
        out_specs=pl.BlockSpec((nb, tn), lambda j: (0, j)),
        out_shape=jax.ShapeDtypeStruct((nb, n), F32),
        compiler_params=pltpu.CompilerParams(dimension_semantics=("arbitrary",),
                                             vmem_limit_bytes=VMEM_LIMIT),
        name="ada",
    )(c_all, w_ada, b_ada)


def _rope_half(x, cos, sin):
    return x * cos + pltpu.roll(x, MLA_ROPE_DIM, 1) * sin


def _inproj_kernel(n_p, xp_ref, xs_ref, mod_ref, gpre_ref, win_ref, wkr_ref, gqa_ref, wq_ref, gkva_ref,
                   wkv_ref, cos_ref, sin_ref, dq_ref, dk_ref, dv_ref, qm_ref, km_ref, vm_ref):
    i = pl.program_id(0)
    x = jnp.where(i < n_p, xp_ref[...], xs_ref[...])
    shift = mod_ref[0, 0:1, :]
    scale = mod_ref[0, 1:2, :]
    hb = (_rms(x, gpre_ref[...]) * (1.0 + scale) + shift).astype(BF16)

    dq_ref[...] = (_dot(hb, win_ref[:, C_DQ:C_DQ + DIFF_WIDTH]) * (DIFF_HEAD_DIM ** -0.5 * LOG2E)).astype(BF16)
    dk_ref[...] = _dot(hb, win_ref[:, C_DK:C_DK + DIFF_WIDTH]).astype(BF16)
    dv_ref[...] = _dot(hb, win_ref[:, C_DV:C_DV + DIFF_WIDTH]).astype(BF16)

    cos = cos_ref[...]
    sin = sin_ref[...]
    cq = _dot(hb, win_ref[:, C_CQ:C_CQ + Q_LORA_RANK])
    ckv = _dot(hb, win_ref[:, C_CKV:C_CKV + KV_LORA_RANK])
    kr = _rope_half(_dot(hb, wkr_ref[...]), cos, sin).astype(BF16)

    q = _dot(_rms(cq, gqa_ref[...]).astype(BF16), wq_ref[...])
    qscale = (MLA_NOPE_DIM + MLA_ROPE_DIM) ** -0.5 * LOG2E
    for h in range(MLA_HEADS):
        c0 = h * MLA_QK_PAD
        qm_ref[:, c0:c0 + LANES] = (q[:, c0:c0 + LANES] * qscale).astype(BF16)
        qm_ref[:, c0 + LANES:c0 + 2 * LANES] = (
            _rope_half(q[:, c0 + LANES:c0 + 2 * LANES], cos, sin) * qscale).astype(BF16)

    kv = _dot(_rms(ckv, gkva_ref[...]).astype(BF16), wkv_ref[...])
    for h in range(MLA_HEADS):
        c0 = h * MLA_QK_PAD
        km_ref[:, c0:c0 + LANES] = kv[:, c0:c0 + LANES].astype(BF16)
        km_ref[:, c0 + LANES:c0 + 2 * LANES] = kr
        vm_ref[:, h * MLA_V_DIM:(h + 1) * MLA_V_DIM] = kv[:, c0 + LANES:c0 + 2 * LANES].astype(BF16)


def _inproj(rows, xp, xs, mod, g_pre, w_in, w_kr, g_q_a, w_q, g_kv_a, w_kv, cos_t, sin_t):
    tm, d = rows.tm, D_MODEL
    const = lambda i: (0, 0)
    row = lambda i: (i, 0)
    one = pl.Buffered(1)
    out_w = (DIFF_WIDTH, DIFF_WIDTH, DIFF_WIDTH, MLA_HEADS * MLA_QK_PAD, MLA_HEADS * MLA_QK_PAD, D_MLA_OUT)
    return pl.pallas_call(
        functools.partial(_inproj_kernel, rows.n_p),
        grid=(rows.n,),
        in_specs=[pl.BlockSpec((tm, d), lambda i: (rows.p_idx(i), 0)),
                  pl.BlockSpec((tm, d), lambda i: (rows.s_idx(i), 0)),
                  pl.BlockSpec((1, N_MOD, d), lambda i: (rows.seq(i), 0, 0)),
                  pl.BlockSpec((1, d), const),
                  pl.BlockSpec((d, D_IN_PROJ), const, pipeline_mode=one),
                  pl.BlockSpec((d, 2 * MLA_ROPE_DIM), const),
                  pl.BlockSpec((1, Q_LORA_RANK), const),
                  pl.BlockSpec((Q_LORA_RANK, MLA_HEADS * MLA_QK_PAD), const, pipeline_mode=one),
                  pl.BlockSpec((1, KV_LORA_RANK), const),
                  pl.BlockSpec((KV_LORA_RANK, MLA_HEADS * MLA_QK_PAD), const, pipeline_mode=one),
                  pl.BlockSpec((tm, LANES), lambda i: (rows.pos(i), 0)),
                  pl.BlockSpec((tm, LANES), lambda i: (rows.pos(i), 0))],
        out_specs=[pl.BlockSpec((tm, w), row) for w in out_w],
        out_shape=[jax.ShapeDtypeStruct((rows.t, w), BF16) for w in out_w],
        compiler_params=pltpu.CompilerParams(dimension_semantics=("arbitrary",),
                                             vmem_limit_bytes=VMEM_LIMIT),
        name="inproj",
    )(xp, xs, mod, g_pre, w_in, w_kr, g_q_a, w_q, g_kv_a, w_kv, cos_t, sin_t)


def _lane_groups(x, op):
    out = x[:, 0:LANES]
    for g in range(1, x.shape[1] // LANES):
        out = op(out, x[:, g * LANES:(g + 1) * LANES])
    return out


def _aligned(x, m):
    return x if isinstance(x, int) else pl.multiple_of(x, m)


def _chunk(ref, j, tk):
    return ref[pl.ds(_aligned(j * tk, tk), tk), :]


def _score_block(unit, s_view, jb, cpb, tk, mpart):
    score_fn, shift_fn = unit
    for c in range(cpb):
        j = jb * cpb + c
        s = score_fn(j)
        s_view[jb, :, c * tk:(c + 1) * tk] = s
        part = _lane_groups(s, jnp.maximum)
        if shift_fn is not None:
            part = part + shift_fn(j)
        mpart = jnp.maximum(mpart, part)
    return mpart


def _value_block(unit, m, v_ref, s_view, jb, cpb, tk, acc):
    _, shift_fn = unit
    ps = []
    for c in range(cpb):
        j = jb * cpb + c
        mj = m if shift_fn is None else m - shift_fn(j)
        ps.append(jnp.exp2(s_view[jb, :, c * tk:(c + 1) * tk] - mj).astype(BF16))
    p_all = ps[0] if cpb == 1 else jnp.concatenate(ps, axis=1)
    return acc + _dot(p_all, _chunk(v_ref, jb, cpb * tk))


def _attn_phases(geom, v_ref, score=(), value=()):
    tq, tk, cpb, nkb = geom
    zeros = jnp.zeros((tq, 2 * LANES), F32)
    neg_inf = jnp.full((tq, LANES), -jnp.inf, F32)

    def body(jb, carry):
        mparts, accs = carry
        mparts = tuple(_score_block(u, view, jb, cpb, tk, mp) for (u, view), mp in zip(score, mparts))
        accs = tuple(_value_block(u, m, v_ref, view, jb, cpb, tk, acc) for (u, m, view), acc in zip(value, accs))
        return mparts, accs

    init = ((neg_inf,) * len(score), (zeros,) * len(value))
    mparts, accs = lax.fori_loop(0, nkb, body, init, unroll=min(ATTN_UNROLL, nkb))
    return ([jnp.max(mp, axis=-1, keepdims=True) for mp in mparts],
            [acc[:, 0:LANES] / acc[:, LANES:LANES + 1] for acc in accs])


def _fill_value_rows(v_ref, vaug):
    lane = lax.broadcasted_iota(jnp.int32, v_ref.shape, 1)
    vaug[:, 0:LANES] = v_ref[...]
    vaug[:, LANES:2 * LANES] = jnp.where(lane == 0, 1.0, 0.0).astype(BF16)


def _attn_grouped(geom, v_ref, units, s_buf):
    views = [s_buf.at[k] for k in range(len(units))]
    ms, _ = _attn_phases(geom, v_ref, score=tuple(zip(units, views)))
    _, outs = _attn_phases(geom, v_ref, value=tuple(zip(units, ms, views)))
    return outs


def _diff_attn_kernel(lambda_init, geom, n_sub, slopes_ref, lam_ref, gsub_ref, q_ref, k_ref, v_ref,
                      o_ref, s_buf, bias_buf, vaug):
    tq, tk, _, _ = geom
    h = pl.program_id(1)
    qo = pl.program_id(2)
    slope = slopes_ref[h] * LOG2E
    n_cross = tk // tq

    @pl.when(qo == 0)
    def _():
        _fill_value_rows(v_ref, vaug)
        r_minus_c = (lax.broadcasted_iota(jnp.int32, (tq, tk), 0)
                     - lax.broadcasted_iota(jnp.int32, (tq, tk), 1)).astype(F32)
        bias_buf[0] = r_minus_c * (-slope)
        bias_buf[1] = r_minus_c * slope
        for c in range(n_cross):
            bias_buf[2 + c] = jnp.abs(r_minus_c + float(c * tq)) * (-slope)

    lp = lam_ref[...]
    lam = (jnp.exp(jnp.sum(lp[0:1] * lp[1:2], axis=-1, keepdims=True))
           - jnp.exp(jnp.sum(lp[2:3] * lp[3:4], axis=-1, keepdims=True)) + lambda_init)
    gsub = gsub_ref[...]

    def unit(sub, half):
        qbase = (qo * n_sub + sub) * tq
        jc = qbase // tk
        cross = 2 + (qbase - jc * tk) // tq
        q = q_ref[pl.ds(_aligned(sub * tq, tq), tq), :]
        lane = lax.broadcasted_iota(jnp.int32, q.shape, 1)
        keep = (lane < DIFF_HEAD_DIM) if half == 0 else (lane >= DIFF_HEAD_DIM)
        qh = jnp.where(keep, q, jnp.zeros_like(q))
        shift = lambda j: jnp.where(j == jc, 0.0, -slope * jnp.abs(qbase - j * tk).astype(F32))
        which = lambda j: jnp.where(j < jc, 0, jnp.where(j > jc, 1, cross))
        return (lambda j: _dot_nt(qh, _chunk(k_ref, j, tk)) + bias_buf[which(j)]), shift

    group = s_buf.shape[0] // 2

    def grouped(g, carry):
        subs = [g * group + k for k in range(group)]
        outs = _attn_grouped(geom, vaug, [unit(sub, half) for sub in subs for half in range(2)], s_buf)
        for k, sub in enumerate(subs):
            o = outs[2 * k] - lam * outs[2 * k + 1]
            o_ref[pl.ds(_aligned(sub * tq, tq), tq), :] = (_rms(o, gsub) * (1.0 - lambda_init)).astype(BF16)
        return carry

    lax.fori_loop(0, n_sub // group, grouped, 0)


def _attn_tiles(s, row0, units_per_sub):
    tq = min(ATTN_TQ, s)
    tk = min(ATTN_TK, s)
    tkb = min(ATTN_TKB, s)
    tq_outer = min(ATTN_TQ_OUTER, s)
    assert s % tq_outer == 0 and tq_outer % tq == 0 and s % tkb == 0 and tkb % tk == 0 and tk % tq == 0
    assert row0 % s == 0
    nkb, n_sub = s // tkb, tq_outer // tq
    units = max(units_per_sub, min(ATTN_GROUP, ATTN_SCORE_BYTES // (tq * s * 4)))
    group = min(units // units_per_sub, n_sub)
    assert n_sub % group == 0
    return (tq, tk, tkb // tk, nkb), tq_outer, (group * units_per_sub, nkb, tq, tkb)


def _diff_attn(dq, dk, dv, slopes, lam_p, g_sub, row0, b, s, lambda_init):
    geom, tq_outer, s_buf_shape = _attn_tiles(s, row0, 2)
    tq, tk = geom[0], geom[1]
    n_sub, nqo = tq_outer // tq, s // tq_outer
    dh = 2 * DIFF_HEAD_DIM
    return pl.pallas_call(
        functools.partial(_diff_attn_kernel, lambda_init, geom, n_sub),
        grid=(b, DIFF_HEADS, nqo),
        in_specs=[pl.BlockSpec(memory_space=pltpu.SMEM),
                  pl.BlockSpec((8, LANES), lambda bi, h, qi: (0, 0)),
                  pl.BlockSpec((1, dh), lambda bi, h, qi: (0, 0)),
                  pl.BlockSpec((tq_outer, dh), lambda bi, h, qi: (row0 // tq_outer + bi * nqo + qi, h)),
                  pl.BlockSpec((s, dh), lambda bi, h, qi: (row0 // s + bi, h)),
                  pl.BlockSpec((s, dh), lambda bi, h, qi: (row0 // s + bi, h))],
        out_specs=pl.BlockSpec((tq_outer, dh), lambda bi, h, qi: (bi * nqo + qi, h)),
        out_shape=jax.ShapeDtypeStruct((b * s, DIFF_WIDTH), BF16),
        scratch_shapes=[pltpu.VMEM(s_buf_shape, F32), pltpu.VMEM((2 + tk // tq, tq, tk), F32),
                        pltpu.VMEM((s, V_AUG), BF16)],
        compiler_params=pltpu.CompilerParams(
            dimension_semantics=("arbitrary", "arbitrary", "arbitrary"), vmem_limit_bytes=VMEM_LIMIT),
        name="diff_attn",
    )(slopes, lam_p, g_sub, dq, dk, dv)


def _mla_attn_kernel(geom, n_sub, q_ref, k_ref, v_ref, o_ref, s_buf, vaug):
    tq, tk, _, _ = geom

    @pl.when(pl.program_id(2) == 0)
    def _():
        _fill_value_rows(v_ref, vaug)

    def unit(sub):
        q = q_ref[pl.ds(_aligned(sub * tq, tq), tq), :]
        return (lambda j: _dot_nt(q, _chunk(k_ref, j, tk))), None

    group = s_buf.shape[0]

    def grouped(g, carry):
        subs = [g * group + k for k in range(group)]
        for sub, o in zip(subs, _attn_grouped(geom, vaug, [unit(sub) for sub in subs], s_buf)):
            o_ref[pl.ds(_aligned(sub * tq, tq), tq), :] = o.astype(BF16)
        return carry

    lax.fori_loop(0, n_sub // group, grouped, 0)


def _mla_attn(qm, km, vm, row0, b, s):
    geom, tq_outer, s_buf_shape = _attn_tiles(s, row0, 1)
    n_sub, nqo = tq_outer // geom[0], s // tq_outer
    return pl.pallas_call(
        functools.partial(_mla_attn_kernel, geom, n_sub),
        grid=(b, MLA_HEADS, nqo),
        in_specs=[pl.BlockSpec((tq_outer, MLA_QK_PAD), lambda bi, h, qi: (row0 // tq_outer + bi * nqo + qi, h)),
                  pl.BlockSpec((s, MLA_QK_PAD), lambda bi, h, qi: (row0 // s + bi, h)),
                  pl.BlockSpec((s, MLA_V_DIM), lambda bi, h, qi: (row0 // s + bi, h))],
        out_specs=pl.BlockSpec((tq_outer, MLA_V_DIM), lambda bi, h, qi: (bi * nqo + qi, h)),
        out_shape=jax.ShapeDtypeStruct((b * s, D_MLA_OUT), BF16),
        scratch_shapes=[pltpu.VMEM(s_buf_shape, F32), pltpu.VMEM((s, V_AUG), BF16)],
        compiler_params=pltpu.CompilerParams(
            dimension_semantics=("arbitrary", "arbitrary", "arbitrary"), vmem_limit_bytes=VMEM_LIMIT),
        name="mla_attn",
    )(qm, km, vm)


def _route(logits):
    lane = lax.broadcasted_iota(jnp.int32, logits.shape, 1)
    neg = jnp.full(logits.shape, -jnp.inf, F32)
    big = jnp.full(logits.shape, LANES, jnp.int32)
    first = lambda mask: jnp.min(jnp.where(mask, lane, big), axis=-1, keepdims=True)

    gl = jnp.where(lane < N_GROUPS, logits, neg)
    gmax = jnp.max(gl, axis=-1, keepdims=True)
    g_idx = first(gl == gmax)
    g_w = 1.0 / jnp.sum(jnp.exp(gl - gmax), axis=-1, keepdims=True)

    lo = ROUTER_LANE0 + EXPERTS_PER_GROUP * g_idx
    el = jnp.where(lane >= lo, jnp.where(lane < lo + EXPERTS_PER_GROUP, logits, neg), neg)
    v1 = jnp.max(el, axis=-1, keepdims=True)
    i1 = first(el == v1)
    el2 = jnp.where(lane == i1, neg, el)
    v2 = jnp.max(el2, axis=-1, keepdims=True)
    i2 = first(el2 == v2)
    t = jnp.exp(v2 - v1)
    w1 = g_w / (1.0 + t)
    w2 = w1 * t
    first_low = i1 < i2
    ea = jnp.minimum(i1, i2) - lo
    eb = jnp.maximum(i1, i2) - lo
    pair = jnp.where(ea == 0, 0, jnp.where(ea == 1, 3, 5)) + eb - ea - 1
    bucket = g_idx * N_PAIRS + pair
    return bucket, jnp.where(first_low, w1, w2), jnp.where(first_low, w2, w1)


def _oproj_kernel(n_p, oap_ref, oas_ref, obp_ref, obs_ref, xp_ref, xs_ref, mod_ref, wo_ref, gpost_ref,
                  gpre_ref, wr_ref, br_ref, xmid_ref, hx_ref, meta_ref, counts_ref, cnt):
    i = pl.program_id(0)

    @pl.when(i == 0)
    def _():
        cnt[...] = jnp.zeros(cnt.shape, F32)

    is_p = i < n_p
    oa = jnp.where(is_p, oap_ref[...], oas_ref[...])
    ob = jnp.where(is_p, obp_ref[...], obs_ref[...])
    x = jnp.where(is_p, xp_ref[...], xs_ref[...])
    mix = _dot(oa, wo_ref[0:DIFF_WIDTH, :]) + _dot(ob, wo_ref[DIFF_WIDTH:DIFF_WIDTH + D_MLA_OUT, :])
    x_mid = x + mod_ref[0, 2:3, :] * _rms(mix, gpost_ref[...])
    xmid_ref[...] = x_mid
    h2 = _rms(x_mid, gpre_ref[...]) * (1.0 + mod_ref[0, 4:5, :]) + mod_ref[0, 3:4, :]
    w_r = wr_ref[...]
    w_hi = w_r.astype(BF16)
    w_hi_lo = jnp.concatenate([w_hi, (w_r - w_hi.astype(F32)).astype(BF16)], axis=1)
    h_hi = h2.astype(BF16)
    h_lo = (h2 - h_hi.astype(F32)).astype(BF16)
    hi_terms = _dot(h_hi, w_hi_lo)
    logits = hi_terms[:, 0:LANES] + hi_terms[:, LANES:2 * LANES] + _dot(h_lo, w_hi) + br_ref[...]
    bucket, wa, wb = _route(logits)

    tm = logits.shape[0]
    lane = lax.broadcasted_iota(jnp.int32, logits.shape, 1)
    hx_ref[:, 0:D_MODEL] = h2
    hx_ref[:, D_MODEL:HX_WIDTH] = jnp.where(lane == 0, wa, jnp.where(lane == 1, wb, jnp.zeros_like(logits)))

    onehot = jnp.where(lane == bucket, 1.0, 0.0)
    earlier = (lax.broadcasted_iota(jnp.int32, (tm, tm), 0) > lax.broadcasted_iota(jnp.int32, (tm, tm), 1))
    before = _dot(jnp.where(earlier, 1.0, 0.0).astype(BF16), onehot.astype(BF16))
    rank = jnp.sum(onehot * (before + cnt[0:1, :]), axis=-1, keepdims=True)
    cnt[...] = cnt[...] + jnp.sum(onehot, axis=0, keepdims=True)
    meta_ref[...] = jnp.where(lane == 0, bucket, jnp.where(lane == 1, rank.astype(jnp.int32), 0))
    counts_ref[...] = cnt[...]


def _oproj(rows, oa_p, oa_s, ob_p, ob_s, xp, xs, mod, w_o, g_post, g_pre, w_r, b_r):
    tm, d = rows.tm, D_MODEL
    const = lambda i: (0, 0)
    row = lambda i: (i, 0)
    prow = lambda i: (rows.p_idx(i), 0)
    srow = lambda i: (rows.s_idx(i), 0)
    return pl.pallas_call(
        functools.partial(_oproj_kernel, rows.n_p),
        grid=(rows.n,),
        in_specs=[pl.BlockSpec((tm, DIFF_WIDTH), prow), pl.BlockSpec((tm, DIFF_WIDTH), srow),
                  pl.BlockSpec((tm, D_MLA_OUT), prow), pl.BlockSpec((tm, D_MLA_OUT), srow),
                  pl.BlockSpec((tm, d), prow), pl.BlockSpec((tm, d), srow),
                  pl.BlockSpec((1, N_MOD, d), lambda i: (rows.seq(i), 0, 0)),
                  pl.BlockSpec((DIFF_WIDTH + D_MLA_OUT, d), const, pipeline_mode=pl.Buffered(1)),
                  pl.BlockSpec((1, d), const), pl.BlockSpec((1, d), const),
                  pl.BlockSpec((d, LANES), const), pl.BlockSpec((1, LANES), const)],
        out_specs=[pl.BlockSpec((tm, d), row), pl.BlockSpec((tm, HX_WIDTH), row),
                   pl.BlockSpec((tm, LANES), row), pl.BlockSpec((8, LANES), const)],
        out_shape=[jax.ShapeDtypeStruct((rows.t, d), F32), jax.ShapeDtypeStruct((rows.t, HX_WIDTH), F32),
                   jax.ShapeDtypeStruct((rows.t, LANES), jnp.int32), jax.ShapeDtypeStruct((8, LANES), F32)],
        scratch_shapes=[pltpu.VMEM((8, LANES), F32)],
        compiler_params=pltpu.CompilerParams(dimension_semantics=("arbitrary",),
                                             vmem_limit_bytes=VMEM_LIMIT),
        name="oproj",
    )(oa_p, oa_s, ob_p, ob_s, xp, xs, mod, w_o, g_post, g_pre, w_r, b_r)


def _gather_rows(idx_ref, src_hbm, dst, sem, inline_rows=None):
    def body(r, carry):
        pltpu.make_async_copy(src_hbm.at[pl.ds(idx_ref[0, 0, r], 1), :], dst.at[pl.ds(r, 1), :], sem).start()
        return carry

    if inline_rows is not None:
        for r in range(*inline_rows):
            body(r, 0)
    else:
        lax.fori_loop(0, dst.shape[0], body, 0, unroll=8)


def _wait_rows(src_hbm, dst, sem):
    pltpu.make_async_copy(src_hbm.at[pl.ds(0, dst.shape[0]), :], dst, sem).wait()


def _experts_kernel(ea_ref, eb_ref, nv_ref, inv_cur, inv_nxt, hx_hbm, wga, wua, wda, wgb, wub, wdb,
                    f_ref, xbuf, sem):
    i = pl.program_id(0)
    n_valid = nv_ref[0]
    slot = i % 2

    @pl.when(i == 0)
    def _():
        _gather_rows(inv_cur, hx_hbm, xbuf.at[0], sem.at[0])

    tm = f_ref.shape[0]

    def compute(prefetch):
        _wait_rows(hx_hbm, xbuf.at[slot], sem.at[slot])
        x = xbuf[slot, :, 0:D_MODEL].astype(BF16)
        aux = xbuf[slot, :, D_MODEL:HX_WIDTH]
        batches = iter(range(4))

        def dot_then_prefetch(a, w):
            out = _dot(a, w)
            if prefetch:
                k = next(batches)
                _gather_rows(inv_nxt, hx_hbm, xbuf.at[1 - slot], sem.at[1 - slot],
                             inline_rows=(k * tm // 4, (k + 1) * tm // 4))
            return out

        def hidden(wg, wu, w):
            g = dot_then_prefetch(x, wg[0])
            return ((g / (1.0 + jnp.exp(-g))) * dot_then_prefetch(x, wu[0]) * w).astype(BF16)

        f_ref[...] = (_dot(hidden(wga, wua, aux[:, 0:1]), wda[0])
                      + _dot(hidden(wgb, wub, aux[:, 1:2]), wdb[0]))

    @pl.when(i + 1 < n_valid)
    def _():
        compute(True)

    @pl.when(i + 1 == n_valid)
    def _():
        compute(False)

    @pl.when(i >= n_valid)
    def _():
        f_ref[...] = jnp.zeros(f_ref.shape, F32)


def _experts(hx, inv3, tile_ea, tile_eb, n_valid, w_gate, w_up, w_down):
    n_tiles, _, tm = inv3.shape
    d = D_MODEL
    wspec = lambda shape, which: pl.BlockSpec(
        (1,) + shape, (lambda i, ea, eb, nv: (ea[i], 0, 0)) if which == 0 else (lambda i, ea, eb, nv: (eb[i], 0, 0)))
    smem = lambda imap: pl.BlockSpec((1, 1, tm), imap, memory_space=pltpu.SMEM)
    return pl.pallas_call(
        _experts_kernel,
        grid_spec=pltpu.PrefetchScalarGridSpec(
            num_scalar_prefetch=3,
            grid=(n_tiles,),
            in_specs=[smem(lambda i, ea, eb, nv: (i, 0, 0)),
                      smem(lambda i, ea, eb, nv: (jnp.minimum(i + 1, n_tiles - 1), 0, 0)),
                      pl.BlockSpec(memory_space=pl.ANY),
                      wspec((d, D_EXPERT), 0), wspec((d, D_EXPERT), 0), wspec((D_EXPERT, d), 0),
                      wspec((d, D_EXPERT), 1), wspec((d, D_EXPERT), 1), wspec((D_EXPERT, d), 1)],
            out_specs=pl.BlockSpec((tm, d), lambda i, ea, eb, nv: (i, 0)),
            scratch_shapes=[pltpu.VMEM((2, tm, HX_WIDTH), F32), pltpu.SemaphoreType.DMA((2,))]),
        out_shape=jax.ShapeDtypeStruct((n_tiles * tm, d), F32),
        compiler_params=pltpu.CompilerParams(dimension_semantics=("arbitrary",),
                                             vmem_limit_bytes=VMEM_LIMIT),
        name="experts",
    )(tile_ea, tile_eb, n_valid, inv3, inv3, hx, w_gate, w_up, w_down, w_gate, w_up, w_down)


def _final_kernel(n_p, pos_cur, pos_nxt, f_hbm, xmid_ref, mod_ref, gpost_ref, yp_ref, ys_ref, fbuf, sem):
    i = pl.program_id(0)
    slot = i % 2

    @pl.when(i == 0)
    def _():
        _gather_rows(pos_cur, f_hbm, fbuf.at[0], sem.at[0])

    tm = xmid_ref.shape[0]
    rows_per_slice = tm // GATHER_SLICES

    def compute(y_ref, prefetch):
        _wait_rows(f_hbm, fbuf.at[slot], sem.at[slot])
        for s in range(GATHER_SLICES):
            lo, hi = s * rows_per_slice, (s + 1) * rows_per_slice
            y_ref[lo:hi, :] = (xmid_ref[lo:hi, :]
                               + mod_ref[0, 5:6, :] * _rms(fbuf[slot, lo:hi, :], gpost_ref[...]))
            if prefetch:
                _gather_rows(pos_nxt, f_hbm, fbuf.at[1 - slot], sem.at[1 - slot], inline_rows=(lo, hi))

    last = i + 1 == pl.num_programs(0)
    for y_ref, mine in ((yp_ref, i < n_p), (ys_ref, i >= n_p)):
        @pl.when(jnp.logical_and(mine, jnp.logical_not(last)))
        def _():
            compute(y_ref, True)

        @pl.when(jnp.logical_and(mine, last))
        def _():
            compute(y_ref, False)


def _final(rows, pos3, f_sorted, x_mid, mod, g_post):
    tm, d = rows.tm, D_MODEL
    smem = lambda imap: pl.BlockSpec((1, 1, tm), imap, memory_space=pltpu.SMEM)
    return pl.pallas_call(
        functools.partial(_final_kernel, rows.n_p),
        grid=(rows.n,),
        in_specs=[smem(lambda i: (i, 0, 0)),
                  smem(lambda i: (jnp.minimum(i + 1, rows.n - 1), 0, 0)),
                  pl.BlockSpec(memory_space=pl.ANY),
                  pl.BlockSpec((tm, d), lambda i: (i, 0)),
                  pl.BlockSpec((1, N_MOD, d), lambda i: (rows.seq(i), 0, 0)),
                  pl.BlockSpec((1, d), lambda i: (0, 0))],
        out_specs=[pl.BlockSpec((tm, d), lambda i: (rows.p_idx(i), 0)),
                   pl.BlockSpec((tm, d), lambda i: (rows.s_idx(i), 0))],
        out_shape=[jax.ShapeDtypeStruct((rows.tp, d), F32), jax.ShapeDtypeStruct((rows.ts, d), F32)],
        scratch_shapes=[pltpu.VMEM((2, tm, d), F32), pltpu.SemaphoreType.DMA((2,))],
        compiler_params=pltpu.CompilerParams(dimension_semantics=("arbitrary",),
                                             vmem_limit_bytes=VMEM_LIMIT),
        name="final",
    )(pos3, pos3, f_sorted, x_mid, mod, g_post)


def _moe_plan(meta, counts, tm):
    t = meta.shape[0]
    n_tiles = t // tm + N_BUCKETS
    cnt = counts[0, :N_BUCKETS].astype(jnp.int32)
    padded = (cnt + tm - 1) // tm * tm
    ends = jnp.cumsum(padded)
    buckets = jnp.arange(N_BUCKETS, dtype=jnp.int32)
    start_of = jnp.sum(jnp.where(meta[:, 0:1] == buckets[None, :], (ends - padded)[None, :], 0), axis=1)
    pos = start_of + meta[:, 1]
    inv = jnp.zeros((n_tiles * tm,), jnp.int32).at[pos].set(jnp.arange(t, dtype=jnp.int32))
    n_valid = ends[-1] // tm
    tile = jnp.minimum(jnp.arange(n_tiles, dtype=jnp.int32), n_valid - 1)
    bucket = jnp.sum((tile[:, None] * tm >= ends[None, :]).astype(jnp.int32), axis=1)
    group, pair = bucket // N_PAIRS, bucket % N_PAIRS
    ea = jnp.array([0, 0, 0, 1, 1, 2], jnp.int32)[pair]
    eb = jnp.array([1, 2, 3, 2, 3, 3], jnp.int32)[pair]
    base = group * EXPERTS_PER_GROUP
    return pos, inv, base + ea, base + eb, n_valid.reshape(1).astype(jnp.int32)


def _rotate_half_cols(w):
    half = MLA_ROPE_DIM // 2
    return jnp.concatenate([-w[..., half:], w[..., :half]], axis=-1)


def _rope_tables(s_max):
    half = MLA_ROPE_DIM // 2
    inv = ROPE_BASE ** (-jnp.arange(half, dtype=F32) / half)
    ang = jnp.arange(s_max, dtype=F32)[:, None] * inv[None, :]
    pad = jnp.zeros((s_max, LANES - MLA_ROPE_DIM), F32)
    cos, sin = jnp.cos(ang), jnp.sin(ang)
    return (jnp.concatenate([cos, cos, pad], axis=-1), jnp.concatenate([sin, sin, pad], axis=-1))


def _layer(x_prompt, x_sample, c_prompt, c_sample, layer_idx, w_ada, b_ada, g_pre_mix, g_post_mix, w_in,
           lambda_q1, lambda_k1, lambda_q2, lambda_k2, g_diff_sub, g_q_a, w_uq, g_kv_a, w_ukv, w_o,
           g_pre_ffn, g_post_ffn, w_router_group, b_router_group, w_router_expert, b_router_expert,
           w_gate, w_up, w_down, tm=256):
    bp, sp, d = x_prompt.shape
    bs, ss, _ = x_sample.shape
    rows = _Rows(bp, sp, bs, ss, tm)
    xp = x_prompt.reshape(bp * sp, d)
    xs = x_sample.reshape(bs * ss, d)
    lambda_init = 0.8 - 0.6 * math.exp(-0.3 * layer_idx)

    nb = bp + bs
    nb_pad = -(-nb // 8) * 8
    c_all = jnp.concatenate([c_prompt, c_sample, jnp.zeros((nb_pad - nb, d), F32)], axis=0)
    mod = _ada(c_all, w_ada, b_ada.reshape(1, -1)).reshape(nb_pad, N_MOD, d)

    w_kr = w_in[:, C_KR:C_KR + MLA_ROPE_DIM]
    w_kr = jnp.concatenate([w_kr, _rotate_half_cols(w_kr)], axis=1).astype(BF16)
    wq3 = w_uq.reshape(Q_LORA_RANK, MLA_HEADS, MLA_NOPE_DIM + MLA_ROPE_DIM)
    wq_rope = wq3[..., MLA_NOPE_DIM:]
    w_q = jnp.concatenate([wq3[..., :MLA_NOPE_DIM], wq_rope, _rotate_half_cols(wq_rope)], axis=-1)
    w_q = w_q.reshape(Q_LORA_RANK, MLA_HEADS * MLA_QK_PAD).astype(BF16)
    w_kv = w_ukv.astype(BF16)
    cos_t, sin_t = _rope_tables(max(sp, ss))

    dq, dk, dv, qm, km, vm = _inproj(rows, xp, xs, mod, g_pre_mix.reshape(1, d), w_in.astype(BF16), w_kr,
                                     g_q_a.reshape(1, -1), w_q, g_kv_a.reshape(1, -1), w_kv, cos_t, sin_t)

    slopes = jnp.array([2.0 ** (-8.0 * (i + 1) / DIFF_HEADS) for i in range(DIFF_HEADS)], dtype=F32)
    lam_p = jnp.zeros((8, LANES), F32).at[0:4, 0:DIFF_HEAD_DIM].set(
        jnp.stack([lambda_q1, lambda_k1, lambda_q2, lambda_k2]))
    g_sub = g_diff_sub.reshape(1, -1)
    oa_p = _diff_attn(dq, dk, dv, slopes, lam_p, g_sub, 0, bp, sp, lambda_init)
    oa_s = _diff_attn(dq, dk, dv, slopes, lam_p, g_sub, rows.tp, bs, ss, lambda_init)
    ob_p = _mla_attn(qm, km, vm, 0, bp, sp)
    ob_s = _mla_attn(qm, km, vm, rows.tp, bs, ss)

    w_r = jnp.concatenate([w_router_group, w_router_expert.reshape(d, N_EXPERTS),
                           jnp.zeros((d, LANES - N_GROUPS - N_EXPERTS), F32)], axis=1)
    b_r = jnp.concatenate([b_router_group, b_router_expert.reshape(N_EXPERTS),
                           jnp.zeros((LANES - N_GROUPS - N_EXPERTS,), F32)]).reshape(1, LANES)
    x_mid, hx, meta, counts = _oproj(rows, oa_p, oa_s, ob_p, ob_s, xp, xs, mod, w_o.astype(BF16),
                                     g_post_mix.reshape(1, d), g_pre_ffn.reshape(1, d), w_r, b_r)

    pos, inv, tile_ea, tile_eb, n_valid = _moe_plan(meta, counts, MOE_TM)
    f_sorted = _experts(hx, inv.reshape(-1, 1, MOE_TM), tile_ea, tile_eb, n_valid,
                        w_gate.astype(BF16), w_up.astype(BF16), w_down.astype(BF16))
    yp, ys = _final(rows, pos.reshape(rows.n, 1, tm), f_sorted, x_mid, mod, g_post_ffn.reshape(1, d))
    return yp.reshape(bp, sp, d), ys.reshape(bs, ss, d)


def kernel(x_prompt, x_sample, c_prompt, c_sample, w_ada, b_ada, g_pre_mix, g_post_mix, w_in, lambda_q1,
           lambda_k1, lambda_q2, lambda_k2, g_diff_sub, g_q_a, w_uq, g_kv_a, w_ukv, w_o, g_pre_ffn,
           g_post_ffn, w_router_group, b_router_group, w_router_expert, b_router_expert, w_gate, w_up,
           w_down):
    for l in range(w_ada.shape[0]):
        x_prompt, x_sample = _layer(
            x_prompt, x_sample, c_prompt, c_sample, l, w_ada[l], b_ada[l], g_pre_mix[l], g_post_mix[l],
            w_in[l], lambda_q1[l], lambda_k1[l], lambda_q2[l], lambda_k2[l], g_diff_sub[l], g_q_a[l],
            w_uq[l], g_kv_a[l], w_ukv[l], w_o[l], g_pre_ffn[l], g_post_ffn[l], w_router_group[l],
            b_router_group[l], w_router_expert[l], b_router_expert[l], w_gate[l], w_up[l], w_down[l])
    return x_prompt, x_sample
```

```python
import functools
import math

import jax
import jax.numpy as jnp
from jax import lax
from jax.experimental import pallas as pl
from jax.experimental.pallas import tpu as pltpu

F32 = jnp.float32
BF16 = jnp.bfloat16

D_MODEL = 2048
DIFF_HEADS = 8
DIFF_HEAD_DIM = 64
DIFF_WIDTH = DIFF_HEADS * 2 * DIFF_HEAD_DIM
MLA_HEADS = 8
MLA_NOPE_DIM = 128
MLA_ROPE_DIM = 64
MLA_V_DIM = 128
MLA_QK_PAD = 256
V_AUG = 256
Q_LORA_RANK = 512
KV_LORA_RANK = 256
ROPE_BASE = 10000.0
D_MLA_OUT = MLA_HEADS * MLA_V_DIM
N_GROUPS = 4
EXPERTS_PER_GROUP = 4
N_EXPERTS = N_GROUPS * EXPERTS_PER_GROUP
D_EXPERT = 512
N_MOD = 6
NORM_EPS = 1e-6
LANES = 128
ROUTER_LANE0 = N_GROUPS
N_PAIRS = 6
N_BUCKETS = N_GROUPS * N_PAIRS
HX_WIDTH = D_MODEL + LANES
MOE_TM = 256

C_DQ, C_DK, C_DV = 0, DIFF_WIDTH, 2 * DIFF_WIDTH
C_CQ = 3 * DIFF_WIDTH
C_CKV = C_CQ + Q_LORA_RANK
C_KR = C_CKV + KV_LORA_RANK
D_IN_PROJ = C_KR + MLA_ROPE_DIM

VMEM_LIMIT = 56 * 1024 * 1024
LOG2E = 1.4426950408889634
ATTN_TQ = 256
ATTN_TK = 512
ATTN_TQ_OUTER = 1024
ATTN_TKB = 2048
ATTN_UNROLL = 2
ATTN_GROUP = 4
ATTN_SCORE_BYTES = 16 * 1024 * 1024


def _rms(x, g):
    return x * lax.rsqrt(jnp.mean(x * x, axis=-1, keepdims=True) + NORM_EPS) * g


def _dot(a, b):
    return jnp.dot(a, b, preferred_element_type=F32)


def _dot_nt(a, b):
    return lax.dot_general(a, b, (((1,), (1,)), ((), ())), preferred_element_type=F32)


class _Rows:
    def __init__(self, bp, sp, bs, ss, tm):
        assert sp % tm == 0 and ss % tm == 0
        self.bp, self.sp, self.bs, self.ss, self.tm = bp, sp, bs, ss, tm
        self.n_p = bp * sp // tm
        self.n_s = bs * ss // tm
        self.n = self.n_p + self.n_s
        self.tp = bp * sp
        self.ts = bs * ss
        self.t = self.tp + self.ts

    def p_idx(self, i):
        return jnp.minimum(i, self.n_p - 1)

    def s_idx(self, i):
        return jnp.maximum(i - self.n_p, 0)

    def seq(self, i):
        return jnp.where(i < self.n_p, i // (self.sp // self.tm),
                         self.bp + (i - self.n_p) // (self.ss // self.tm))

    def pos(self, i):
        return jnp.where(i < self.n_p, i % (self.sp // self.tm),
                         (i - self.n_p) % (self.ss // self.tm))


def _ada_kernel(c_ref, w_ref, b_ref, o_ref):
    c = c_ref[...]
    a = c / (1.0 + jnp.exp(-c))
    o_ref[...] = jnp.dot(a, w_ref[...], preferred_element_type=F32,
                         precision=lax.Precision.HIGHEST) + b_ref[...]


def _ada(c_all, w_ada, b_ada):
    nb, d = c_all.shape
    n = w_ada.shape[1]
    tn = 1024
    return pl.pallas_call(
        _ada_kernel,
        grid=(n // tn,),
        in_specs=[pl.BlockSpec((nb, d), lambda j: (0, 0)),
                  pl.BlockSpec((d, tn), lambda j: (0, j)),
                  pl.BlockSpec((1, tn), lambda j: (0, j))],
        out_specs=pl.BlockSpec((nb, tn), lambda j: (0, j)),
        out_shape=jax.ShapeDtypeStruct((nb, n), F32),
        compiler_params=pltpu.CompilerParams(dimension_semantics=("arbitrary",),
                                             vmem_limit_bytes=VMEM_LIMIT),
        name="ada",
    )(c_all, w_ada, b_ada)


def _rope_half(x, cos, sin):
    return x * cos + pltpu.roll(x, MLA_ROPE_DIM, 1) * sin


def _inproj_kernel(n_p, xp_ref, xs_ref, mod_ref, gpre_ref, win_ref, wkr_ref, gqa_ref, wq_ref, gkva_ref,
                   wkv_ref, cos_ref, sin_ref, dq_ref, dk_ref, dv_ref, qm_ref, km_ref, vm_ref):
    i = pl.program_id(0)
    x = jnp.where(i < n_p, xp_ref[...], xs_ref[...])
    shift = mod_ref[0, 0:1, :]
    scale = mod_ref[0, 1:2, :]
    hb = (_rms(x, gpre_ref[...]) * (1.0 + scale) + shift).astype(BF16)

    dq_ref[...] = (_dot(hb, win_ref[:, C_DQ:C_DQ + DIFF_WIDTH]) * (DIFF_HEAD_DIM ** -0.5 * LOG2E)).astype(BF16)
    dk_ref[...] = _dot(hb, win_ref[:, C_DK:C_DK + DIFF_WIDTH]).astype(BF16)
    dv_ref[...] = _dot(hb, win_ref[:, C_DV:C_DV + DIFF_WIDTH]).astype(BF16)

    cos = cos_ref[...]
    sin = sin_ref[...]
    cq = _dot(hb, win_ref[:, C_CQ:C_CQ + Q_LORA_RANK])
    ckv = _dot(hb, win_ref[:, C_CKV:C_CKV + KV_LORA_RANK])
    kr = _rope_half(_dot(hb, wkr_ref[...]), cos, sin).astype(BF16)

    q = _dot(_rms(cq, gqa_ref[...]).astype(BF16), wq_ref[...])
    qscale = (MLA_NOPE_DIM + MLA_ROPE_DIM) ** -0.5 * LOG2E
    for h in range(MLA_HEADS):
        c0 = h * MLA_QK_PAD
        qm_ref[:, c0:c0 + LANES] = (q[:, c0:c0 + LANES] * qscale).astype(BF16)
        qm_ref[:, c0 + LANES:c0 + 2 * LANES] = (
            _rope_half(q[:, c0 + LANES:c0 + 2 * LANES], cos, sin) * qscale).astype(BF16)

    kv = _dot(_rms(ckv, gkva_ref[...]).astype(BF16), wkv_ref[...])
    for h in range(MLA_HEADS):
        c0 = h * MLA_QK_PAD
        km_ref[:, c0:c0 + LANES] = kv[:, c0:c0 + LANES].astype(BF16)
        km_ref[:, c0 + LANES:c0 + 2 * LANES] = kr
        vm_ref[:, h * MLA_V_DIM:(h + 1) * MLA_V_DIM] = kv[:, c0 + LANES:c0 + 2 * LANES].astype(BF16)


def _inproj(rows, xp, xs, mod, g_pre, w_in, w_kr, g_q_a, w_q, g_kv_a, w_kv, cos_t, sin_t):
    tm, d = rows.tm, D_MODEL
    const = lambda i: (0, 0)
    row = lambda i: (i, 0)
    one = pl.Buffered(1)
    out_w = (DIFF_WIDTH, DIFF_WIDTH, DIFF_WIDTH, MLA_HEADS * MLA_QK_PAD, MLA_HEADS * MLA_QK_PAD, D_MLA_OUT)
    return pl.pallas_call(
        functools.partial(_inproj_kernel, rows.n_p),
        grid=(rows.n,),
        in_specs=[pl.BlockSpec((tm, d), lambda i: (rows.p_idx(i), 0)),
                  pl.BlockSpec((tm, d), lambda i: (rows.s_idx(i), 0)),
                  pl.BlockSpec((1, N_MOD, d), lambda i: (rows.seq(i), 0, 0)),
                  pl.BlockSpec((1, d), const),
                  pl.BlockSpec((d, D_IN_PROJ), const, pipeline_mode=one),
                  pl.BlockSpec((d, 2 * MLA_ROPE_DIM), const),
                  pl.BlockSpec((1, Q_LORA_RANK), const),
                  pl.BlockSpec((Q_LORA_RANK, MLA_HEADS * MLA_QK_PAD), const, pipeline_mode=one),
                  pl.BlockSpec((1, KV_LORA_RANK), const),
                  pl.BlockSpec((KV_LORA_RANK, MLA_HEADS * MLA_QK_PAD), const, pipeline_mode=one),
                  pl.BlockSpec((tm, LANES), lambda i: (rows.pos(i), 0)),
                  pl.BlockSpec((tm, LANES), lambda i: (rows.pos(i), 0))],
        out_specs=[pl.BlockSpec((tm, w), row) for w in out_w],
        out_shape=[jax.ShapeDtypeStruct((rows.t, w), BF16) for w in out_w],
        compiler_params=pltpu.CompilerParams(dimension_semantics=("arbitrary",),
                                             vmem_limit_bytes=VMEM_LIMIT),
        name="inproj",
    )(xp, xs, mod, g_pre, w_in, w_kr, g_q_a, w_q, g_kv_a, w_kv, cos_t, sin_t)


def _lane_groups(x, op):
    out = x[:, 0:LANES]
    for g in range(1, x.shape[1] // LANES):
        out = op(out, x[:, g * LANES:(g + 1) * LANES])
    return out


def _aligned(x, m):
    return x if isinstance(x, int) else pl.multiple_of(x, m)


def _chunk(ref, j, tk):
    return ref[pl.ds(_aligned(j * tk, tk), tk), :]


def _score_block(unit, s_view, jb, cpb, tk, mpart):
    score_fn, shift_fn = unit
    for c in range(cpb):
        j = jb * cpb + c
        s = score_fn(j)
        s_view[jb, :, c * tk:(c + 1) * tk] = s
        part = _lane_groups(s, jnp.maximum)
        if shift_fn is not None:
            part = part + shift_fn(j)
        mpart = jnp.maximum(mpart, part)
    return mpart


def _value_block(unit, m, v_ref, s_view, jb, cpb, tk, acc):
    _, shift_fn = unit
    ps = []
    for c in range(cpb):
        j = jb * cpb + c
        mj = m if shift_fn is None else m - shift_fn(j)
        ps.append(jnp.exp2(s_view[jb, :, c * tk:(c + 1) * tk] - mj).astype(BF16))
    p_all = ps[0] if cpb == 1 else jnp.concatenate(ps, axis=1)
    return acc + _dot(p_all, _chunk(v_ref, jb, cpb * tk))


def _attn_phases(geom, v_ref, score=(), value=()):
    tq, tk, cpb, nkb = geom
    zeros = jnp.zeros((tq, 2 * LANES), F32)
    neg_inf = jnp.full((tq, LANES), -jnp.inf, F32)

    def body(jb, carry):
        mparts, accs = carry
        mparts = tuple(_score_block(u, view, jb, cpb, tk, mp) for (u, view), mp in zip(score, mparts))
        accs = tuple(_value_block(u, m, v_ref, view, jb, cpb, tk, acc) for (u, m, view), acc in zip(value, accs))
        return mparts, accs

    init = ((neg_inf,) * len(score), (zeros,) * len(value))
    mparts, accs = lax.fori_loop(0, nkb, body, init, unroll=min(ATTN_UNROLL, nkb))
    return ([jnp.max(mp, axis=-1, keepdims=True) for mp in mparts],
            [acc[:, 0:LANES] / acc[:, LANES:LANES + 1] for acc in accs])


def _fill_value_rows(v_ref, vaug):
    lane = lax.broadcasted_iota(jnp.int32, v_ref.shape, 1)
    vaug[:, 0:LANES] = v_ref[...]
    vaug[:, LANES:2 * LANES] = jnp.where(lane == 0, 1.0, 0.0).astype(BF16)


def _attn_grouped(geom, v_ref, units, s_buf):
    views = [s_buf.at[k] for k in range(len(units))]
    ms, _ = _attn_phases(geom, v_ref, score=tuple(zip(units, views)))
    _, outs = _attn_phases(geom, v_ref, value=tuple(zip(units, ms, views)))
    return outs


def _diff_attn_kernel(lambda_init, geom, n_sub, slopes_ref, lam_ref, gsub_ref, q_ref, k_ref, v_ref,
                      o_ref, s_buf, bias_buf, vaug):
    tq, tk, _, _ = geom
    h = pl.program_id(1)
    qo = pl.program_id(2)
    slope = slopes_ref[h] * LOG2E
    n_cross = tk // tq

    @pl.when(qo == 0)
    def _():
        _fill_value_rows(v_ref, vaug)
        r_minus_c = (lax.broadcasted_iota(jnp.int32, (tq, tk), 0)
                     - lax.broadcasted_iota(jnp.int32, (tq, tk), 1)).astype(F32)
        bias_buf[0] = r_minus_c * (-slope)
        bias_buf[1] = r_minus_c * slope
        for c in range(n_cross):
            bias_buf[2 + c] = jnp.abs(r_minus_c + float(c * tq)) * (-slope)

    lp = lam_ref[...]
    lam = (jnp.exp(jnp.sum(lp[0:1] * lp[1:2], axis=-1, keepdims=True))
           - jnp.exp(jnp.sum(lp[2:3] * lp[3:4], axis=-1, keepdims=True)) + lambda_init)
    gsub = gsub_ref[...]

    def unit(sub, half):
        qbase = (qo * n_sub + sub) * tq
        jc = qbase // tk
        cross = 2 + (qbase - jc * tk) // tq
        q = q_ref[pl.ds(_aligned(sub * tq, tq), tq), :]
        lane = lax.broadcasted_iota(jnp.int32, q.shape, 1)
        keep = (lane < DIFF_HEAD_DIM) if half == 0 else (lane >= DIFF_HEAD_DIM)
        qh = jnp.where(keep, q, jnp.zeros_like(q))
        shift = lambda j: jnp.where(j == jc, 0.0, -slope * jnp.abs(qbase - j * tk).astype(F32))
        which = lambda j: jnp.where(j < jc, 0, jnp.where(j > jc, 1, cross))
        return (lambda j: _dot_nt(qh, _chunk(k_ref, j, tk)) + bias_buf[which(j)]), shift

    group = s_buf.shape[0] // 2

    def grouped(g, carry):
        subs = [g * group + k for k in range(group)]
        outs = _attn_grouped(geom, vaug, [unit(sub, half) for sub in subs for half in range(2)], s_buf)
        for k, sub in enumerate(subs):
            o = outs[2 * k] - lam * outs[2 * k + 1]
            o_ref[pl.ds(_aligned(sub * tq, tq), tq), :] = (_rms(o, gsub) * (1.0 - lambda_init)).astype(BF16)
        return carry

    lax.fori_loop(0, n_sub // group, grouped, 0)


def _attn_tiles(s, row0, units_per_sub):
    tq = min(ATTN_TQ, s)
    tk = min(ATTN_TK, s)
    tkb = min(ATTN_TKB, s)
    tq_outer = min(ATTN_TQ_OUTER, s)
    assert s % tq_outer == 0 and tq_outer % tq == 0 and s % tkb == 0 and tkb % tk == 0 and tk % tq == 0
    assert row0 % s == 0
    nkb, n_sub = s // tkb, tq_outer // tq
    units = max(units_per_sub, min(ATTN_GROUP, ATTN_SCORE_BYTES // (tq * s * 4)))
    group = min(units // units_per_sub, n_sub)
    assert n_sub % group == 0
    return (tq, tk, tkb // tk, nkb), tq_outer, (group * units_per_sub, nkb, tq, tkb)


def _diff_attn(dq, dk, dv, slopes, lam_p, g_sub, row0, b, s, lambda_init):
    geom, tq_outer, s_buf_shape = _attn_tiles(s, row0, 2)
    tq, tk = geom[0], geom[1]
    n_sub, nqo = tq_outer // tq, s // tq_outer
    dh = 2 * DIFF_HEAD_DIM
    return pl.pallas_call(
        functools.partial(_diff_attn_kernel, lambda_init, geom, n_sub),
        grid=(b, DIFF_HEADS, nqo),
        in_specs=[pl.BlockSpec(memory_space=pltpu.SMEM),
                  pl.BlockSpec((8, LANES), lambda bi, h, qi: (0, 0)),
                  pl.BlockSpec((1, dh), lambda bi, h, qi: (0, 0)),
                  pl.BlockSpec((tq_outer, dh), lambda bi, h, qi: (row0 // tq_outer + bi * nqo + qi, h)),
                  pl.BlockSpec((s, dh), lambda bi, h, qi: (row0 // s + bi, h)),
                  pl.BlockSpec((s, dh), lambda bi, h, qi: (row0 // s + bi, h))],
        out_specs=pl.BlockSpec((tq_outer, dh), lambda bi, h, qi: (bi * nqo + qi, h)),
        out_shape=jax.ShapeDtypeStruct((b * s, DIFF_WIDTH), BF16),
        scratch_shapes=[pltpu.VMEM(s_buf_shape, F32), pltpu.VMEM((2 + tk // tq, tq, tk), F32),
                        pltpu.VMEM((s, V_AUG), BF16)],
        compiler_params=pltpu.CompilerParams(
            dimension_semantics=("arbitrary", "arbitrary", "arbitrary"), vmem_limit_bytes=VMEM_LIMIT),
        name="diff_attn",
    )(slopes, lam_p, g_sub, dq, dk, dv)


def _mla_attn_kernel(geom, n_sub, q_ref, k_ref, v_ref, o_ref, s_buf, vaug):
    tq, tk, _, _ = geom

    @pl.when(pl.program_id(2) == 0)
    def _():
        _fill_value_rows(v_ref, vaug)

    def unit(sub):
        q = q_ref[pl.ds(_aligned(sub * tq, tq), tq), :]
        return (lambda j: _dot_nt(q, _chunk(k_ref, j, tk))), None

    group = s_buf.shape[0]

    def grouped(g, carry):
        subs = [g * group + k for k in range(group)]
        for sub, o in zip(subs, _attn_grouped(geom, vaug, [unit(sub) for sub in subs], s_buf)):
            o_ref[pl.ds(_aligned(sub * tq, tq), tq), :] = o.astype(BF16)
        return carry

    lax.fori_loop(0, n_sub // group, grouped, 0)


def _mla_attn(qm, km, vm, row0, b, s):
    geom, tq_outer, s_buf_shape = _attn_tiles(s, row0, 1)
    n_sub, nqo = tq_outer // geom[0], s // tq_outer
    return pl.pallas_call(
        functools.partial(_mla_attn_kernel, geom, n_sub),
        grid=(b, MLA_HEADS, nqo),
        in_specs=[pl.BlockSpec((tq_outer, MLA_QK_PAD), lambda bi, h, qi: (row0 // tq_outer + bi * nqo + qi, h)),
                  pl.BlockSpec((s, MLA_QK_PAD), lambda bi, h, qi: (row0 // s + bi, h)),
                  pl.BlockSpec((s, MLA_V_DIM), lambda bi, h, qi: (row0 // s + bi, h))],
        out_specs=pl.BlockSpec((tq_outer, MLA_V_DIM), lambda bi, h, qi: (bi * nqo + qi, h)),
        out_shape=jax.ShapeDtypeStruct((b * s, D_MLA_OUT), BF16),
        scratch_shapes=[pltpu.VMEM(s_buf_shape, F32), pltpu.VMEM((s, V_AUG), BF16)],
        compiler_params=pltpu.CompilerParams(
            dimension_semantics=("arbitrary", "arbitrary", "arbitrary"), vmem_limit_bytes=VMEM_LIMIT),
        name="mla_attn",
    )(qm, km, vm)


def _route(logits):
    lane = lax.broadcasted_iota(jnp.int32, logits.shape, 1)
    neg = jnp.full(logits.shape, -jnp.inf, F32)
    big = jnp.full(logits.shape, LANES, jnp.int32)
    first = lambda mask: jnp.min(jnp.where(mask, lane, big), axis=-1, keepdims=True)

    gl = jnp.where(lane < N_GROUPS, logits, neg)
    gmax = jnp.max(gl, axis=-1, keepdims=True)
    g_idx = first(gl == gmax)
    g_w = 1.0 / jnp.sum(jnp.exp(gl - gmax), axis=-1, keepdims=True)

    lo = ROUTER_LANE0 + EXPERTS_PER_GROUP * g_idx
    el = jnp.where(lane >= lo, jnp.where(lane < lo + EXPERTS_PER_GROUP, logits, neg), neg)
    v1 = jnp.max(el, axis=-1, keepdims=True)
    i1 = first(el == v1)
    el2 = jnp.where(lane == i1, neg, el)
    v2 = jnp.max(el2, axis=-1, keepdims=True)
    i2 = first(el2 == v2)
    t = jnp.exp(v2 - v1)
    w1 = g_w / (1.0 + t)
    w2 = w1 * t
    first_low = i1 < i2
    ea = jnp.minimum(i1, i2) - lo
    eb = jnp.maximum(i1, i2) - lo
    pair = jnp.where(ea == 0, 0, jnp.where(ea == 1, 3, 5)) + eb - ea - 1
    bucket = g_idx * N_PAIRS + pair
    return bucket, jnp.where(first_low, w1, w2), jnp.where(first_low, w2, w1)


def _oproj_kernel(n_p, oap_ref, oas_ref, obp_ref, obs_ref, xp_ref, xs_ref, mod_ref, wo_ref, gpost_ref,
                  gpre_ref, wr_ref, br_ref, xmid_ref, hx_ref, meta_ref, counts_ref, cnt):
    i = pl.program_id(0)

    @pl.when(i == 0)
    def _():
        cnt[...] = jnp.zeros(cnt.shape, F32)

    is_p = i < n_p
    oa = jnp.where(is_p, oap_ref[...], oas_ref[...])
    ob = jnp.where(is_p, obp_ref[...], obs_ref[...])
    x = jnp.where(is_p, xp_ref[...], xs_ref[...])
    mix = _dot(oa, wo_ref[0:DIFF_WIDTH, :]) + _dot(ob, wo_ref[DIFF_WIDTH:DIFF_WIDTH + D_MLA_OUT, :])
    x_mid = x + mod_ref[0, 2:3, :] * _rms(mix, gpost_ref[...])
    xmid_ref[...] = x_mid
    h2 = _rms(x_mid, gpre_ref[...]) * (1.0 + mod_ref[0, 4:5, :]) + mod_ref[0, 3:4, :]
    w_r = wr_ref[...]
    w_hi = w_r.astype(BF16)
    w_hi_lo = jnp.concatenate([w_hi, (w_r - w_hi.astype(F32)).astype(BF16)], axis=1)
    h_hi = h2.astype(BF16)
    h_lo = (h2 - h_hi.astype(F32)).astype(BF16)
    hi_terms = _dot(h_hi, w_hi_lo)
    logits = hi_terms[:, 0:LANES] + hi_terms[:, LANES:2 * LANES] + _dot(h_lo, w_hi) + br_ref[...]
    bucket, wa, wb = _route(logits)

    tm = logits.shape[0]
    lane = lax.broadcasted_iota(jnp.int32, logits.shape, 1)
    hx_ref[:, 0:D_MODEL] = h2
    hx_ref[:, D_MODEL:HX_WIDTH] = jnp.where(lane == 0, wa, jnp.where(lane == 1, wb, jnp.zeros_like(logits)))

    onehot = jnp.where(lane == bucket, 1.0, 0.0)
    earlier = (lax.broadcasted_iota(jnp.int32, (tm, tm), 0) > lax.broadcasted_iota(jnp.int32, (tm, tm), 1))
    before = _dot(jnp.where(earlier, 1.0, 0.0).astype(BF16), onehot.astype(BF16))
    rank = jnp.sum(onehot * (before + cnt[0:1, :]), axis=-1, keepdims=True)
    cnt[...] = cnt[...] + jnp.sum(onehot, axis=0, keepdims=True)
    meta_ref[...] = jnp.where(lane == 0, bucket, jnp.where(lane == 1, rank.astype(jnp.int32), 0))
    counts_ref[...] = cnt[...]


def _oproj(rows, oa_p, oa_s, ob_p, ob_s, xp, xs, mod, w_o, g_post, g_pre, w_r, b_r):
    tm, d = rows.tm, D_MODEL
    const = lambda i: (0, 0)
    row = lambda i: (i, 0)
    prow = lambda i: (rows.p_idx(i), 0)
    srow = lambda i: (rows.s_idx(i), 0)
    return pl.pallas_call(
        functools.partial(_oproj_kernel, rows.n_p),
        grid=(rows.n,),
        in_specs=[pl.BlockSpec((tm, DIFF_WIDTH), prow), pl.BlockSpec((tm, DIFF_WIDTH), srow),
                  pl.BlockSpec((tm, D_MLA_OUT), prow), pl.BlockSpec((tm, D_MLA_OUT), srow),
                  pl.BlockSpec((tm, d), prow), pl.BlockSpec((tm, d), srow),
                  pl.BlockSpec((1, N_MOD, d), lambda i: (rows.seq(i), 0, 0)),
                  pl.BlockSpec((DIFF_WIDTH + D_MLA_OUT, d), const, pipeline_mode=pl.Buffered(1)),
                  pl.BlockSpec((1, d), const), pl.BlockSpec((1, d), const),
                  pl.BlockSpec((d, LANES), const), pl.BlockSpec((1, LANES), const)],
        out_specs=[pl.BlockSpec((tm, d), row), pl.BlockSpec((tm, HX_WIDTH), row),
                   pl.BlockSpec((tm, LANES), row), pl.BlockSpec((8, LANES), const)],
        out_shape=[jax.ShapeDtypeStruct((rows.t, d), F32), jax.ShapeDtypeStruct((rows.t, HX_WIDTH), F32),
                   jax.ShapeDtypeStruct((rows.t, LANES), jnp.int32), jax.ShapeDtypeStruct((8, LANES), F32)],
        scratch_shapes=[pltpu.VMEM((8, LANES), F32)],
        compiler_params=pltpu.CompilerParams(dimension_semantics=("arbitrary",),
                                             vmem_limit_bytes=VMEM_LIMIT),
        name="oproj",
    )(oa_p, oa_s, ob_p, ob_s, xp, xs, mod, w_o, g_post, g_pre, w_r, b_r)


def _gather_rows(idx_ref, src_hbm, dst, sem, inline=False):
    def copy(r):
        return pltpu.make_async_copy(src_hbm.at[pl.ds(idx_ref[0, 0, r], 1), :], dst.at[pl.ds(r, 1), :], sem)

    if inline:
        for r in range(dst.shape[0]):
            copy(r).start(priority=r % 2)
    else:
        def body(r, carry):
            copy(r).start()
            return carry

        lax.fori_loop(0, dst.shape[0], body, 0, unroll=8)


def _wait_rows(src_hbm, dst, sem):
    pltpu.make_async_copy(src_hbm.at[pl.ds(0, dst.shape[0]), :], dst, sem).wait()


def _experts_kernel(ea_ref, eb_ref, nv_ref, inv_cur, inv_nxt, hx_hbm, wga, wua, wda, wgb, wub, wdb,
                    f_ref, xbuf, sem):
    i = pl.program_id(0)
    n_valid = nv_ref[0]
    slot = i % 2

    @pl.when(i == 0)
    def _():
        _gather_rows(inv_cur, hx_hbm, xbuf.at[0], sem.at[0])

    def compute():
        _wait_rows(hx_hbm, xbuf.at[slot], sem.at[slot])
        x = xbuf[slot, :, 0:D_MODEL].astype(BF16)
        aux = xbuf[slot, :, D_MODEL:HX_WIDTH]

        def hidden(wg, wu, w):
            g = _dot(x, wg[0])
            return ((g / (1.0 + jnp.exp(-g))) * _dot(x, wu[0]) * w).astype(BF16)

        f_ref[...] = (_dot(hidden(wga, wua, aux[:, 0:1]), wda[0])
                      + _dot(hidden(wgb, wub, aux[:, 1:2]), wdb[0]))

    @pl.when(i + 1 < n_valid)
    def _():
        _gather_rows(inv_nxt, hx_hbm, xbuf.at[1 - slot], sem.at[1 - slot], inline=True)
        compute()

    @pl.when(i + 1 == n_valid)
    def _():
        compute()

    @pl.when(i >= n_valid)
    def _():
        f_ref[...] = jnp.zeros(f_ref.shape, F32)


def _experts(hx, inv3, tile_ea, tile_eb, n_valid, w_gate, w_up, w_down):
    n_tiles, _, tm = inv3.shape
    d = D_MODEL
    wspec = lambda shape, which: pl.BlockSpec(
        (1,) + shape, (lambda i, ea, eb, nv: (ea[i], 0, 0)) if which == 0 else (lambda i, ea, eb, nv: (eb[i], 0, 0)))
    smem = lambda imap: pl.BlockSpec((1, 1, tm), imap, memory_space=pltpu.SMEM)
    return pl.pallas_call(
        _experts_kernel,
        grid_spec=pltpu.PrefetchScalarGridSpec(
            num_scalar_prefetch=3,
            grid=(n_tiles,),
            in_specs=[smem(lambda i, ea, eb, nv: (i, 0, 0)),
                      smem(lambda i, ea, eb, nv: (jnp.minimum(i + 1, n_tiles - 1), 0, 0)),
                      pl.BlockSpec(memory_space=pl.ANY),
                      wspec((d, D_EXPERT), 0), wspec((d, D_EXPERT), 0), wspec((D_EXPERT, d), 0),
                      wspec((d, D_EXPERT), 1), wspec((d, D_EXPERT), 1), wspec((D_EXPERT, d), 1)],
            out_specs=pl.BlockSpec((tm, d), lambda i, ea, eb, nv: (i, 0)),
            scratch_shapes=[pltpu.VMEM((2, tm, HX_WIDTH), F32), pltpu.SemaphoreType.DMA((2,))]),
        out_shape=jax.ShapeDtypeStruct((n_tiles * tm, d), F32),
        compiler_params=pltpu.CompilerParams(dimension_semantics=("arbitrary",),
                                             vmem_limit_bytes=VMEM_LIMIT),
        name="experts",
    )(tile_ea, tile_eb, n_valid, inv3, inv3, hx, w_gate, w_up, w_down, w_gate, w_up, w_down)


def _final_kernel(n_p, pos_cur, pos_nxt, f_hbm, xmid_ref, mod_ref, gpost_ref, yp_ref, ys_ref, fbuf, sem):
    i = pl.program_id(0)
    slot = i % 2

    @pl.when(i == 0)
    def _():
        _gather_rows(pos_cur, f_hbm, fbuf.at[0], sem.at[0])

    def compute():
        _wait_rows(f_hbm, fbuf.at[slot], sem.at[slot])
        y = xmid_ref[...] + mod_ref[0, 5:6, :] * _rms(fbuf[slot], gpost_ref[...])

        @pl.when(i < n_p)
        def _():
            yp_ref[...] = y

        @pl.when(i >= n_p)
        def _():
            ys_ref[...] = y

    @pl.when(i + 1 < pl.num_programs(0))
    def _():
        _gather_rows(pos_nxt, f_hbm, fbuf.at[1 - slot], sem.at[1 - slot], inline=True)
        compute()

    @pl.when(i + 1 == pl.num_programs(0))
    def _():
        compute()


def _final(rows, pos3, f_sorted, x_mid, mod, g_post):
    tm, d = rows.tm, D_MODEL
    smem = lambda imap: pl.BlockSpec((1, 1, tm), imap, memory_space=pltpu.SMEM)
    return pl.pallas_call(
        functools.partial(_final_kernel, rows.n_p),
        grid=(rows.n,),
        in_specs=[smem(lambda i: (i, 0, 0)),
                  smem(lambda i: (jnp.minimum(i + 1, rows.n - 1), 0, 0)),
                  pl.BlockSpec(memory_space=pl.ANY),
                  pl.BlockSpec((tm, d), lambda i: (i, 0)),
                  pl.BlockSpec((1, N_MOD, d), lambda i: (rows.seq(i), 0, 0)),
                  pl.BlockSpec((1, d), lambda i: (0, 0))],
        out_specs=[pl.BlockSpec((tm, d), lambda i: (rows.p_idx(i), 0)),
                   pl.BlockSpec((tm, d), lambda i: (rows.s_idx(i), 0))],
        out_shape=[jax.ShapeDtypeStruct((rows.tp, d), F32), jax.ShapeDtypeStruct((rows.ts, d), F32)],
        scratch_shapes=[pltpu.VMEM((2, tm, d), F32), pltpu.SemaphoreType.DMA((2,))],
        compiler_params=pltpu.CompilerParams(dimension_semantics=("arbitrary",),
                                             vmem_limit_bytes=VMEM_LIMIT),
        name="final",
    )(pos3, pos3, f_sorted, x_mid, mod, g_post)


def _moe_plan(meta, counts, tm):
    t = meta.shape[0]
    n_tiles = t // tm + N_BUCKETS
    cnt = counts[0, :N_BUCKETS].astype(jnp.int32)
    padded = (cnt + tm - 1) // tm * tm
    ends = jnp.cumsum(padded)
    buckets = jnp.arange(N_BUCKETS, dtype=jnp.int32)
    start_of = jnp.sum(jnp.where(meta[:, 0:1] == buckets[None, :], (ends - padded)[None, :], 0), axis=1)
    pos = start_of + meta[:, 1]
    inv = jnp.zeros((n_tiles * tm,), jnp.int32).at[pos].set(jnp.arange(t, dtype=jnp.int32))
    n_valid = ends[-1] // tm
    tile = jnp.minimum(jnp.arange(n_tiles, dtype=jnp.int32), n_valid - 1)
    bucket = jnp.sum((tile[:, None] * tm >= ends[None, :]).astype(jnp.int32), axis=1)
    group, pair = bucket // N_PAIRS, bucket % N_PAIRS
    ea = jnp.array([0, 0, 0, 1, 1, 2], jnp.int32)[pair]
    eb = jnp.array([1, 2, 3, 2, 3, 3], jnp.int32)[pair]
    base = group * EXPERTS_PER_GROUP
    return pos, inv, base + ea, base + eb, n_valid.reshape(1).astype(jnp.int32)


def _rotate_half_cols(w):
    half = MLA_ROPE_DIM // 2
    return jnp.concatenate([-w[..., half:], w[..., :half]], axis=-1)


def _rope_tables(s_max):
    half = MLA_ROPE_DIM // 2
    inv = ROPE_BASE ** (-jnp.arange(half, dtype=F32) / half)
    ang = jnp.arange(s_max, dtype=F32)[:, None] * inv[None, :]
    pad = jnp.zeros((s_max, LANES - MLA_ROPE_DIM), F32)
    cos, sin = jnp.cos(ang), jnp.sin(ang)
    return (jnp.concatenate([cos, cos, pad], axis=-1), jnp.concatenate([sin, sin, pad], axis=-1))


def _layer(x_prompt, x_sample, c_prompt, c_sample, layer_idx, w_ada, b_ada, g_pre_mix, g_post_mix, w_in,
           lambda_q1, lambda_k1, lambda_q2, lambda_k2, g_diff_sub, g_q_a, w_uq, g_kv_a, w_ukv, w_o,
           g_pre_ffn, g_post_ffn, w_router_group, b_router_group, w_router_expert, b_router_expert,
           w_gate, w_up, w_down, tm=256):
    bp, sp, d = x_prompt.shape
    bs, ss, _ = x_sample.shape
    rows = _Rows(bp, sp, bs, ss, tm)
    xp = x_prompt.reshape(bp * sp, d)
    xs = x_sample.reshape(bs * ss, d)
    lambda_init = 0.8 - 0.6 * math.exp(-0.3 * layer_idx)

    nb = bp + bs
    nb_pad = -(-nb // 8) * 8
    c_all = jnp.concatenate([c_prompt, c_sample, jnp.zeros((nb_pad - nb, d), F32)], axis=0)
    mod = _ada(c_all, w_ada, b_ada.reshape(1, -1)).reshape(nb_pad, N_MOD, d)

    w_kr = w_in[:, C_KR:C_KR + MLA_ROPE_DIM]
    w_kr = jnp.concatenate([w_kr, _rotate_half_cols(w_kr)], axis=1).astype(BF16)
    wq3 = w_uq.reshape(Q_LORA_RANK, MLA_HEADS, MLA_NOPE_DIM + MLA_ROPE_DIM)
    wq_rope = wq3[..., MLA_NOPE_DIM:]
    w_q = jnp.concatenate([wq3[..., :MLA_NOPE_DIM], wq_rope, _rotate_half_cols(wq_rope)], axis=-1)
    w_q = w_q.reshape(Q_LORA_RANK, MLA_HEADS * MLA_QK_PAD).astype(BF16)
    w_kv = w_ukv.astype(BF16)
    cos_t, sin_t = _rope_tables(max(sp, ss))

    dq, dk, dv, qm, km, vm = _inproj(rows, xp, xs, mod, g_pre_mix.reshape(1, d), w_in.astype(BF16), w_kr,
                                     g_q_a.reshape(1, -1), w_q, g_kv_a.reshape(1, -1), w_kv, cos_t, sin_t)

    slopes = jnp.array([2.0 ** (-8.0 * (i + 1) / DIFF_HEADS) for i in range(DIFF_HEADS)], dtype=F32)
    lam_p = jnp.zeros((8, LANES), F32).at[0:4, 0:DIFF_HEAD_DIM].set(
        jnp.stack([lambda_q1, lambda_k1, lambda_q2, lambda_k2]))
    g_sub = g_diff_sub.reshape(1, -1)
    oa_p = _diff_attn(dq, dk, dv, slopes, lam_p, g_sub, 0, bp, sp, lambda_init)
    oa_s = _diff_attn(dq, dk, dv, slopes, lam_p, g_sub, rows.tp, bs, ss, lambda_init)
    ob_p = _mla_attn(qm, km, vm, 0, bp, sp)
    ob_s = _mla_attn(qm, km, vm, rows.tp, bs, ss)

    w_r = jnp.concatenate([w_router_group, w_router_expert.reshape(d, N_EXPERTS),
                           jnp.zeros((d, LANES - N_GROUPS - N_EXPERTS), F32)], axis=1)
    b_r = jnp.concatenate([b_router_group, b_router_expert.reshape(N_EXPERTS),
                           jnp.zeros((LANES - N_GROUPS - N_EXPERTS,), F32)]).reshape(1, LANES)
    x_mid, hx, meta, counts = _oproj(rows, oa_p, oa_s, ob_p, ob_s, xp, xs, mod, w_o.astype(BF16),
                                     g_post_mix.reshape(1, d), g_pre_ffn.reshape(1, d), w_r, b_r)

    pos, inv, tile_ea, tile_eb, n_valid = _moe_plan(meta, counts, MOE_TM)
    f_sorted = _experts(hx, inv.reshape(-1, 1, MOE_TM), tile_ea, tile_eb, n_valid,
                        w_gate.astype(BF16), w_up.astype(BF16), w_down.astype(BF16))
    yp, ys = _final(rows, pos.reshape(rows.n, 1, tm), f_sorted, x_mid, mod, g_post_ffn.reshape(1, d))
    return yp.reshape(bp, sp, d), ys.reshape(bs, ss, d)


def kernel(x_prompt, x_sample, c_prompt, c_sample, w_ada, b_ada, g_pre_mix, g_post_mix, w_in, lambda_q1,
           lambda_k1, lambda_q2, lambda_k2, g_diff_sub, g_q_a, w_uq, g_kv_a, w_ukv, w_o, g_pre_ffn,
           g_post_ffn, w_router_group, b_router_group, w_router_expert, b_router_expert, w_gate, w_up,
           w_down):
    for l in range(w_ada.shape[0]):
        x_prompt, x_sample = _layer(
            x_prompt, x_sample, c_prompt, c_sample, l, w_ada[l], b_ada[l], g_pre_mix[l], g_post_mix[l],
            w_in[l], lambda_q1[l], lambda_k1[l], lambda_q2[l], lambda_k2[l], g_diff_sub[l], g_q_a[l],
            w_uq[l], g_kv_a[l], w_ukv[l], w_o[l], g_pre_ffn[l], g_post_ffn[l], w_router_group[l],
            b_router_group[l], w_router_expert[l], b_router_expert[l], w_gate[l], w_up[l], w_down[l])
    return x_prompt, x_sample
```

```python
import functools
import math

import jax
import jax.numpy as jnp
from jax import lax
from jax.experimental import pallas as pl
from jax.experimental.pallas import tpu as pltpu

F32 = jnp.float32
BF16 = jnp.bfloat16

D_MODEL = 2048
DIFF_HEADS = 8
DIFF_HEAD_DIM = 64
DIFF_WIDTH = DIFF_HEADS * 2 * DIFF_HEAD_DIM
MLA_HEADS = 8
MLA_NOPE_DIM = 128
MLA_ROPE_DIM = 64
MLA_V_DIM = 128
MLA_QK_PAD = 256
V_AUG = 256
Q_LORA_RANK = 512
KV_LORA_RANK = 256
ROPE_BASE = 10000.0
D_MLA_OUT = MLA_HEADS * MLA_V_DIM
N_GROUPS = 4
EXPERTS_PER_GROUP = 4
N_EXPERTS = N_GROUPS * EXPERTS_PER_GROUP
D_EXPERT = 512
N_MOD = 6
NORM_EPS = 1e-6
LANES = 128
ROUTER_LANE0 = N_GROUPS
N_PAIRS = 6
N_BUCKETS = N_GROUPS * N_PAIRS
HX_WIDTH = D_MODEL + LANES
MOE_TM = 256

C_DQ, C_DK, C_DV = 0, DIFF_WIDTH, 2 * DIFF_WIDTH
C_CQ = 3 * DIFF_WIDTH
C_CKV = C_CQ + Q_LORA_RANK
C_KR = C_CKV + KV_LORA_RANK
D_IN_PROJ = C_KR + MLA_ROPE_DIM

VMEM_LIMIT = 56 * 1024 * 1024
LOG2E = 1.4426950408889634
ATTN_TQ = 256
ATTN_TK = 512
ATTN_TQ_OUTER = 1024
ATTN_TKB = 2048
ATTN_UNROLL = 2
ATTN_GROUP = 4
ATTN_SCORE_BYTES = 16 * 1024 * 1024


def _rms(x, g):
    return x * lax.rsqrt(jnp.mean(x * x, axis=-1, keepdims=True) + NORM_EPS) * g


def _dot(a, b):
    return jnp.dot(a, b, preferred_element_type=F32)


def _dot_nt(a, b):
    return lax.dot_general(a, b, (((1,), (1,)), ((), ())), preferred_element_type=F32)


class _Rows:
    def __init__(self, bp, sp, bs, ss, tm):
        assert sp % tm == 0 and ss % tm == 0
        self.bp, self.sp, self.bs, self.ss, self.tm = bp, sp, bs, ss, tm
        self.n_p = bp * sp // tm
        self.n_s = bs * ss // tm
        self.n = self.n_p + self.n_s
        self.tp = bp * sp
        self.ts = bs * ss
        self.t = self.tp + self.ts

    def p_idx(self, i):
        return jnp.minimum(i, self.n_p - 1)

    def s_idx(self, i):
        return jnp.maximum(i - self.n_p, 0)

    def seq(self, i):
        return jnp.where(i < self.n_p, i // (self.sp // self.tm),
                         self.bp + (i - self.n_p) // (self.ss // self.tm))

    def pos(self, i):
        return jnp.where(i < self.n_p, i % (self.sp // self.tm),
                         (i - self.n_p) % (self.ss // self.tm))


def _ada_kernel(c_ref, w_ref, b_ref, o_ref):
    c = c_ref[...]
    a = c / (1.0 + jnp.exp(-c))
    o_ref[...] = jnp.dot(a, w_ref[...], preferred_element_type=F32,
                         precision=lax.Precision.HIGHEST) + b_ref[...]


def _ada(c_all, w_ada, b_ada):
    nb, d = c_all.shape
    n = w_ada.shape[1]
    tn = 1024
    return pl.pallas_call(
        _ada_kernel,
        grid=(n // tn,),
        in_specs=[pl.BlockSpec((nb, d), lambda j: (0, 0)),
                  pl.BlockSpec((d, tn), lambda j: (0, j)),
                  pl.BlockSpec((1, tn), lambda j: (0, j))],
        out_specs=pl.BlockSpec((nb, tn), lambda j: (0, j)),
        out_shape=jax.ShapeDtypeStruct((nb, n), F32),
        compiler_params=pltpu.CompilerParams(dimension_semantics=("arbitrary",),
                                             vmem_limit_bytes=VMEM_LIMIT),
        name="ada",
    )(c_all, w_ada, b_ada)


def _rope_half(x, cos, sin):
    return x * cos + pltpu.roll(x, MLA_ROPE_DIM, 1) * sin


def _inproj_kernel(n_p, xp_ref, xs_ref, mod_ref, gpre_ref, win_ref, wkr_ref, gqa_ref, wq_ref, gkva_ref,
                   wkv_ref, cos_ref, sin_ref, dq_ref, dk_ref, dv_ref, qm_ref, km_ref, vm_ref):
    i = pl.program_id(0)
    x = jnp.where(i < n_p, xp_ref[...], xs_ref[...])
    shift = mod_ref[0, 0:1, :]
    scale = mod_ref[0, 1:2, :]
    hb = (_rms(x, gpre_ref[...]) * (1.0 + scale) + shift).astype(BF16)

    dq_ref[...] = (_dot(hb, win_ref[:, C_DQ:C_DQ + DIFF_WIDTH]) * (DIFF_HEAD_DIM ** -0.5 * LOG2E)).astype(BF16)
    dk_ref[...] = _dot(hb, win_ref[:, C_DK:C_DK + DIFF_WIDTH]).astype(BF16)
    dv_ref[...] = _dot(hb, win_ref[:, C_DV:C_DV + DIFF_WIDTH]).astype(BF16)

    cos = cos_ref[...]
    sin = sin_ref[...]
    cq = _dot(hb, win_ref[:, C_CQ:C_CQ + Q_LORA_RANK])
    ckv = _dot(hb, win_ref[:, C_CKV:C_CKV + KV_LORA_RANK])
    kr = _rope_half(_dot(hb, wkr_ref[...]), cos, sin).astype(BF16)

    q = _dot(_rms(cq, gqa_ref[...]).astype(BF16), wq_ref[...])
    qscale = (MLA_NOPE_DIM + MLA_ROPE_DIM) ** -0.5 * LOG2E
    for h in range(MLA_HEADS):
        c0 = h * MLA_QK_PAD
        qm_ref[:, c0:c0 + LANES] = (q[:, c0:c0 + LANES] * qscale).astype(BF16)
        qm_ref[:, c0 + LANES:c0 + 2 * LANES] = (
            _rope_half(q[:, c0 + LANES:c0 + 2 * LANES], cos, sin) * qscale).astype(BF16)

    kv = _dot(_rms(ckv, gkva_ref[...]).astype(BF16), wkv_ref[...])
    for h in range(MLA_HEADS):
        c0 = h * MLA_QK_PAD
        km_ref[:, c0:c0 + LANES] = kv[:, c0:c0 + LANES].astype(BF16)
        km_ref[:, c0 + LANES:c0 + 2 * LANES] = kr
        vm_ref[:, h * MLA_V_DIM:(h + 1) * MLA_V_DIM] = kv[:, c0 + LANES:c0 + 2 * LANES].astype(BF16)


def _inproj(rows, xp, xs, mod, g_pre, w_in, w_kr, g_q_a, w_q, g_kv_a, w_kv, cos_t, sin_t):
    tm, d = rows.tm, D_MODEL
    const = lambda i: (0, 0)
    row = lambda i: (i, 0)
    one = pl.Buffered(1)
    out_w = (DIFF_WIDTH, DIFF_WIDTH, DIFF_WIDTH, MLA_HEADS * MLA_QK_PAD, MLA_HEADS * MLA_QK_PAD, D_MLA_OUT)
    return pl.pallas_call(
        functools.partial(_inproj_kernel, rows.n_p),
        grid=(rows.n,),
        in_specs=[pl.BlockSpec((tm, d), lambda i: (rows.p_idx(i), 0)),
                  pl.BlockSpec((tm, d), lambda i: (rows.s_idx(i), 0)),
                  pl.BlockSpec((1, N_MOD, d), lambda i: (rows.seq(i), 0, 0)),
                  pl.BlockSpec((1, d), const),
                  pl.BlockSpec((d, D_IN_PROJ), const, pipeline_mode=one),
                  pl.BlockSpec((d, 2 * MLA_ROPE_DIM), const),
                  pl.BlockSpec((1, Q_LORA_RANK), const),
                  pl.BlockSpec((Q_LORA_RANK, MLA_HEADS * MLA_QK_PAD), const, pipeline_mode=one),
                  pl.BlockSpec((1, KV_LORA_RANK), const),
                  pl.BlockSpec((KV_LORA_RANK, MLA_HEADS * MLA_QK_PAD), const, pipeline_mode=one),
                  pl.BlockSpec((tm, LANES), lambda i: (rows.pos(i), 0)),
                  pl.BlockSpec((tm, LANES), lambda i: (rows.pos(i), 0))],
        out_specs=[pl.BlockSpec((tm, w), row) for w in out_w],
        out_shape=[jax.ShapeDtypeStruct((rows.t, w), BF16) for w in out_w],
        compiler_params=pltpu.CompilerParams(dimension_semantics=("arbitrary",),
                                             vmem_limit_bytes=VMEM_LIMIT),
        name="inproj",
    )(xp, xs, mod, g_pre, w_in, w_kr, g_q_a, w_q, g_kv_a, w_kv, cos_t, sin_t)


def _lane_groups(x, op):
    out = x[:, 0:LANES]
    for g in range(1, x.shape[1] // LANES):
        out = op(out, x[:, g * LANES:(g + 1) * LANES])
    return out


def _aligned(x, m):
    return x if isinstance(x, int) else pl.multiple_of(x, m)


def _chunk(ref, j, tk):
    return ref[pl.ds(_aligned(j * tk, tk), tk), :]


def _score_block(unit, s_view, jb, cpb, tk, mpart):
    score_fn, shift_fn = unit
    for c in range(cpb):
        j = jb * cpb + c
        s = score_fn(j)
        s_view[jb, :, c * tk:(c + 1) * tk] = s
        part = _lane_groups(s, jnp.maximum)
        if shift_fn is not None:
            part = part + shift_fn(j)
        mpart = jnp.maximum(mpart, part)
    return mpart


def _value_block(unit, m, v_ref, s_view, jb, cpb, tk, acc):
    _, shift_fn = unit
    ps = []
    for c in range(cpb):
        j = jb * cpb + c
        mj = m if shift_fn is None else m - shift_fn(j)
        ps.append(jnp.exp2(s_view[jb, :, c * tk:(c + 1) * tk] - mj).astype(BF16))
    p_all = ps[0] if cpb == 1 else jnp.concatenate(ps, axis=1)
    return acc + _dot(p_all, _chunk(v_ref, jb, cpb * tk))


def _attn_phases(geom, v_ref, score=(), value=()):
    tq, tk, cpb, nkb = geom
    zeros = jnp.zeros((tq, 2 * LANES), F32)
    neg_inf = jnp.full((tq, LANES), -jnp.inf, F32)

    def body(jb, carry):
        mparts, accs = carry
        mparts = tuple(_score_block(u, view, jb, cpb, tk, mp) for (u, view), mp in zip(score, mparts))
        accs = tuple(_value_block(u, m, v_ref, view, jb, cpb, tk, acc) for (u, m, view), acc in zip(value, accs))
        return mparts, accs

    init = ((neg_inf,) * len(score), (zeros,) * len(value))
    mparts, accs = lax.fori_loop(0, nkb, body, init, unroll=min(ATTN_UNROLL, nkb))
    return ([jnp.max(mp, axis=-1, keepdims=True) for mp in mparts],
            [acc[:, 0:LANES] / acc[:, LANES:LANES + 1] for acc in accs])


def _fill_value_rows(v_ref, vaug):
    lane = lax.broadcasted_iota(jnp.int32, v_ref.shape, 1)
    vaug[:, 0:LANES] = v_ref[...]
    vaug[:, LANES:2 * LANES] = jnp.where(lane == 0, 1.0, 0.0).astype(BF16)


def _attn_grouped(geom, v_ref, units, s_buf):
    views = [s_buf.at[k] for k in range(len(units))]
    ms, _ = _attn_phases(geom, v_ref, score=tuple(zip(units, views)))
    _, outs = _attn_phases(geom, v_ref, value=tuple(zip(units, ms, views)))
    return outs


def _diff_attn_kernel(lambda_init, geom, n_sub, slopes_ref, lam_ref, gsub_ref, q_ref, k_ref, v_ref,
                      o_ref, s_buf, bias_buf, vaug):
    tq, tk, _, _ = geom
    h = pl.program_id(1)
    qo = pl.program_id(2)
    slope = slopes_ref[h] * LOG2E
    n_cross = tk // tq

    @pl.when(qo == 0)
    def _():
        _fill_value_rows(v_ref, vaug)
        r_minus_c = (lax.broadcasted_iota(jnp.int32, (tq, tk), 0)
                     - lax.broadcasted_iota(jnp.int32, (tq, tk), 1)).astype(F32)
        bias_buf[0] = r_minus_c * (-slope)
        bias_buf[1] = r_minus_c * slope
        for c in range(n_cross):
            bias_buf[2 + c] = jnp.abs(r_minus_c + float(c * tq)) * (-slope)

    lp = lam_ref[...]
    lam = (jnp.exp(jnp.sum(lp[0:1] * lp[1:2], axis=-1, keepdims=True))
           - jnp.exp(jnp.sum(lp[2:3] * lp[3:4], axis=-1, keepdims=True)) + lambda_init)
    gsub = gsub_ref[...]

    def unit(sub, half):
        qbase = (qo * n_sub + sub) * tq
        jc = qbase // tk
        cross = 2 + (qbase - jc * tk) // tq
        q = q_ref[pl.ds(_aligned(sub * tq, tq), tq), :]
        lane = lax.broadcasted_iota(jnp.int32, q.shape, 1)
        keep = (lane < DIFF_HEAD_DIM) if half == 0 else (lane >= DIFF_HEAD_DIM)
        qh = jnp.where(keep, q, jnp.zeros_like(q))
        shift = lambda j: jnp.where(j == jc, 0.0, -slope * jnp.abs(qbase - j * tk).astype(F32))
        which = lambda j: jnp.where(j < jc, 0, jnp.where(j > jc, 1, cross))
        return (lambda j: _dot_nt(qh, _chunk(k_ref, j, tk)) + bias_buf[which(j)]), shift

    group = s_buf.shape[0] // 2

    def grouped(g, carry):
        subs = [g * group + k for k in range(group)]
        outs = _attn_grouped(geom, vaug, [unit(sub, half) for sub in subs for half in range(2)], s_buf)
        for k, sub in enumerate(subs):
            o = outs[2 * k] - lam * outs[2 * k + 1]
            o_ref[pl.ds(_aligned(sub * tq, tq), tq), :] = (_rms(o, gsub) * (1.0 - lambda_init)).astype(BF16)
        return carry

    lax.fori_loop(0, n_sub // group, grouped, 0)


def _attn_tiles(s, row0, units_per_sub):
    tq = min(ATTN_TQ, s)
    tk = min(ATTN_TK, s)
    tkb = min(ATTN_TKB, s)
    tq_outer = min(ATTN_TQ_OUTER, s)
    assert s % tq_outer == 0 and tq_outer % tq == 0 and s % tkb == 0 and tkb % tk == 0 and tk % tq == 0
    assert row0 % s == 0
    nkb, n_sub = s // tkb, tq_outer // tq
    units = max(units_per_sub, min(ATTN_GROUP, ATTN_SCORE_BYTES // (tq * s * 4)))
    group = min(units // units_per_sub, n_sub)
    assert n_sub % group == 0
    return (tq, tk, tkb // tk, nkb), tq_outer, (group * units_per_sub, nkb, tq, tkb)


def _diff_attn(dq, dk, dv, slopes, lam_p, g_sub, row0, b, s, lambda_init):
    geom, tq_outer, s_buf_shape = _attn_tiles(s, row0, 2)
    tq, tk = geom[0], geom[1]
    n_sub, nqo = tq_outer // tq, s // tq_outer
    dh = 2 * DIFF_HEAD_DIM
    return pl.pallas_call(
        functools.partial(_diff_attn_kernel, lambda_init, geom, n_sub),
        grid=(b, DIFF_HEADS, nqo),
        in_specs=[pl.BlockSpec(memory_space=pltpu.SMEM),
                  pl.BlockSpec((8, LANES), lambda bi, h, qi: (0, 0)),
                  pl.BlockSpec((1, dh), lambda bi, h, qi: (0, 0)),
                  pl.BlockSpec((tq_outer, dh), lambda bi, h, qi: (row0 // tq_outer + bi * nqo + qi, h)),
                  pl.BlockSpec((s, dh), lambda bi, h, qi: (row0 // s + bi, h)),
                  pl.BlockSpec((s, dh), lambda bi, h, qi: (row0 // s + bi, h))],
        out_specs=pl.BlockSpec((tq_outer, dh), lambda bi, h, qi: (bi * nqo + qi, h)),
        out_shape=jax.ShapeDtypeStruct((b * s, DIFF_WIDTH), BF16),
        scratch_shapes=[pltpu.VMEM(s_buf_shape, F32), pltpu.VMEM((2 + tk // tq, tq, tk), F32),
                        pltpu.VMEM((s, V_AUG), BF16)],
        compiler_params=pltpu.CompilerParams(
            dimension_semantics=("arbitrary", "arbitrary", "arbitrary"), vmem_limit_bytes=VMEM_LIMIT),
        name="diff_attn",
    )(slopes, lam_p, g_sub, dq, dk, dv)


def _mla_attn_kernel(geom, n_sub, q_ref, k_ref, v_ref, o_ref, s_buf, vaug):
    tq, tk, _, _ = geom

    @pl.when(pl.program_id(2) == 0)
    def _():
        _fill_value_rows(v_ref, vaug)

    def unit(sub):
        q = q_ref[pl.ds(_aligned(sub * tq, tq), tq), :]
        return (lambda j: _dot_nt(q, _chunk(k_ref, j, tk))), None

    group = s_buf.shape[0]

    def grouped(g, carry):
        subs = [g * group + k for k in range(group)]
        for sub, o in zip(subs, _attn_grouped(geom, vaug, [unit(sub) for sub in subs], s_buf)):
            o_ref[pl.ds(_aligned(sub * tq, tq), tq), :] = o.astype(BF16)
        return carry

    lax.fori_loop(0, n_sub // group, grouped, 0)


def _mla_attn(qm, km, vm, row0, b, s):
    geom, tq_outer, s_buf_shape = _attn_tiles(s, row0, 1)
    n_sub, nqo = tq_outer // geom[0], s // tq_outer
    return pl.pallas_call(
        functools.partial(_mla_attn_kernel, geom, n_sub),
        grid=(b, MLA_HEADS, nqo),
        in_specs=[pl.BlockSpec((tq_outer, MLA_QK_PAD), lambda bi, h, qi: (row0 // tq_outer + bi * nqo + qi, h)),
                  pl.BlockSpec((s, MLA_QK_PAD), lambda bi, h, qi: (row0 // s + bi, h)),
                  pl.BlockSpec((s, MLA_V_DIM), lambda bi, h, qi: (row0 // s + bi, h))],
        out_specs=pl.BlockSpec((tq_outer, MLA_V_DIM), lambda bi, h, qi: (bi * nqo + qi, h)),
        out_shape=jax.ShapeDtypeStruct((b * s, D_MLA_OUT), BF16),
        scratch_shapes=[pltpu.VMEM(s_buf_shape, F32), pltpu.VMEM((s, V_AUG), BF16)],
        compiler_params=pltpu.CompilerParams(
            dimension_semantics=("arbitrary", "arbitrary", "arbitrary"), vmem_limit_bytes=VMEM_LIMIT),
        name="mla_attn",
    )(qm, km, vm)


def _route(logits):
    lane = lax.broadcasted_iota(jnp.int32, logits.shape, 1)
    neg = jnp.full(logits.shape, -jnp.inf, F32)
    big = jnp.full(logits.shape, LANES, jnp.int32)
    first = lambda mask: jnp.min(jnp.where(mask, lane, big), axis=-1, keepdims=True)

    gl = jnp.where(lane < N_GROUPS, logits, neg)
    gmax = jnp.max(gl, axis=-1, keepdims=True)
    g_idx = first(gl == gmax)
    g_w = 1.0 / jnp.sum(jnp.exp(gl - gmax), axis=-1, keepdims=True)

    lo = ROUTER_LANE0 + EXPERTS_PER_GROUP * g_idx
    el = jnp.where(lane >= lo, jnp.where(lane < lo + EXPERTS_PER_GROUP, logits, neg), neg)
    v1 = jnp.max(el, axis=-1, keepdims=True)
    i1 = first(el == v1)
    el2 = jnp.where(lane == i1, neg, el)
    v2 = jnp.max(el2, axis=-1, keepdims=True)
    i2 = first(el2 == v2)
    t = jnp.exp(v2 - v1)
    w1 = g_w / (1.0 + t)
    w2 = w1 * t
    first_low = i1 < i2
    ea = jnp.minimum(i1, i2) - lo
    eb = jnp.maximum(i1, i2) - lo
    pair = jnp.where(ea == 0, 0, jnp.where(ea == 1, 3, 5)) + eb - ea - 1
    bucket = g_idx * N_PAIRS + pair
    return bucket, jnp.where(first_low, w1, w2), jnp.where(first_low, w2, w1)


def _oproj_kernel(n_p, oap_ref, oas_ref, obp_ref, obs_ref, xp_ref, xs_ref, mod_ref, wo_ref, gpost_ref,
                  gpre_ref, wr_ref, br_ref, xmid_ref, hx_ref, meta_ref, counts_ref, cnt):
    i = pl.program_id(0)

    @pl.when(i == 0)
    def _():
        cnt[...] = jnp.zeros(cnt.shape, F32)

    is_p = i < n_p
    oa = jnp.where(is_p, oap_ref[...], oas_ref[...])
    ob = jnp.where(is_p, obp_ref[...], obs_ref[...])
    x = jnp.where(is_p, xp_ref[...], xs_ref[...])
    mix = _dot(oa, wo_ref[0:DIFF_WIDTH, :]) + _dot(ob, wo_ref[DIFF_WIDTH:DIFF_WIDTH + D_MLA_OUT, :])
    x_mid = x + mod_ref[0, 2:3, :] * _rms(mix, gpost_ref[...])
    xmid_ref[...] = x_mid
    h2 = _rms(x_mid, gpre_ref[...]) * (1.0 + mod_ref[0, 4:5, :]) + mod_ref[0, 3:4, :]
    w_r = wr_ref[...]
    w_hi = w_r.astype(BF16)
    w_hi_lo = jnp.concatenate([w_hi, (w_r - w_hi.astype(F32)).astype(BF16)], axis=1)
    h_hi = h2.astype(BF16)
    h_lo = (h2 - h_hi.astype(F32)).astype(BF16)
    hi_terms = _dot(h_hi, w_hi_lo)
    logits = hi_terms[:, 0:LANES] + hi_terms[:, LANES:2 * LANES] + _dot(h_lo, w_hi) + br_ref[...]
    bucket, wa, wb = _route(logits)

    tm = logits.shape[0]
    lane = lax.broadcasted_iota(jnp.int32, logits.shape, 1)
    hx_ref[:, 0:D_MODEL] = h2
    hx_ref[:, D_MODEL:HX_WIDTH] = jnp.where(lane == 0, wa, jnp.where(lane == 1, wb, jnp.zeros_like(logits)))

    onehot = jnp.where(lane == bucket, 1.0, 0.0)
    earlier = (lax.broadcasted_iota(jnp.int32, (tm, tm), 0) > lax.broadcasted_iota(jnp.int32, (tm, tm), 1))
    before = _dot(jnp.where(earlier, 1.0, 0.0).astype(BF16), onehot.astype(BF16))
    rank = jnp.sum(onehot * (before + cnt[0:1, :]), axis=-1, keepdims=True)
    cnt[...] = cnt[...] + jnp.sum(onehot, axis=0, keepdims=True)
    meta_ref[...] = jnp.where(lane == 0, bucket, jnp.where(lane == 1, rank.astype(jnp.int32), 0))
    counts_ref[...] = cnt[...]


def _oproj(rows, oa_p, oa_s, ob_p, ob_s, xp, xs, mod, w_o, g_post, g_pre, w_r, b_r):
    tm, d = rows.tm, D_MODEL
    const = lambda i: (0, 0)
    row = lambda i: (i, 0)
    prow = lambda i: (rows.p_idx(i), 0)
    srow = lambda i: (rows.s_idx(i), 0)
    return pl.pallas_call(
        functools.partial(_oproj_kernel, rows.n_p),
        grid=(rows.n,),
        in_specs=[pl.BlockSpec((tm, DIFF_WIDTH), prow), pl.BlockSpec((tm, DIFF_WIDTH), srow),
                  pl.BlockSpec((tm, D_MLA_OUT), prow), pl.BlockSpec((tm, D_MLA_OUT), srow),
                  pl.BlockSpec((tm, d), prow), pl.BlockSpec((tm, d), srow),
                  pl.BlockSpec((1, N_MOD, d), lambda i: (rows.seq(i), 0, 0)),
                  pl.BlockSpec((DIFF_WIDTH + D_MLA_OUT, d), const, pipeline_mode=pl.Buffered(1)),
                  pl.BlockSpec((1, d), const), pl.BlockSpec((1, d), const),
                  pl.BlockSpec((d, LANES), const), pl.BlockSpec((1, LANES), const)],
        out_specs=[pl.BlockSpec((tm, d), row), pl.BlockSpec((tm, HX_WIDTH), row),
                   pl.BlockSpec((tm, LANES), row), pl.BlockSpec((8, LANES), const)],
        out_shape=[jax.ShapeDtypeStruct((rows.t, d), F32), jax.ShapeDtypeStruct((rows.t, HX_WIDTH), F32),
                   jax.ShapeDtypeStruct((rows.t, LANES), jnp.int32), jax.ShapeDtypeStruct((8, LANES), F32)],
        scratch_shapes=[pltpu.VMEM((8, LANES), F32)],
        compiler_params=pltpu.CompilerParams(dimension_semantics=("arbitrary",),
                                             vmem_limit_bytes=VMEM_LIMIT),
        name="oproj",
    )(oa_p, oa_s, ob_p, ob_s, xp, xs, mod, w_o, g_post, g_pre, w_r, b_r)


def _gather_rows(idx_ref, src_hbm, dst, sem, inline=False):
    def body(r, carry):
        pltpu.make_async_copy(src_hbm.at[pl.ds(idx_ref[0, 0, r], 1), :], dst.at[pl.ds(r, 1), :], sem).start()
        return carry

    if inline:
        for r in range(dst.shape[0]):
            body(r, 0)
    else:
        lax.fori_loop(0, dst.shape[0], body, 0, unroll=8)


def _wait_rows(src_hbm, dst, sem):
    pltpu.make_async_copy(src_hbm.at[pl.ds(0, dst.shape[0]), :], dst, sem).wait()


def _experts_kernel(ea_ref, eb_ref, nv_ref, inv_cur, inv_nxt, hx_hbm, wga, wua, wda, wgb, wub, wdb,
                    f_ref, xbuf, sem):
    i = pl.program_id(0)
    n_valid = nv_ref[0]
    slot = i % 2

    @pl.when(i == 0)
    def _():
        _gather_rows(inv_cur, hx_hbm, xbuf.at[0], sem.at[0])

    def compute():
        _wait_rows(hx_hbm, xbuf.at[slot], sem.at[slot])
        x = xbuf[slot, :, 0:D_MODEL].astype(BF16)
        aux = xbuf[slot, :, D_MODEL:HX_WIDTH]

        def hidden(wg, wu, w):
            g = _dot(x, wg[0])
            return ((g / (1.0 + jnp.exp(-g))) * _dot(x, wu[0]) * w).astype(BF16)

        f_ref[...] = (_dot(hidden(wga, wua, aux[:, 0:1]), wda[0])
                      + _dot(hidden(wgb, wub, aux[:, 1:2]), wdb[0]))

    @pl.when(i + 1 < n_valid)
    def _():
        _gather_rows(inv_nxt, hx_hbm, xbuf.at[1 - slot], sem.at[1 - slot], inline=True)
        compute()

    @pl.when(i + 1 == n_valid)
    def _():
        compute()

    @pl.when(i >= n_valid)
    def _():
        f_ref[...] = jnp.zeros(f_ref.shape, F32)


def _experts(hx, inv3, tile_ea, tile_eb, n_valid, w_gate, w_up, w_down):
    n_tiles, _, tm = inv3.shape
    d = D_MODEL
    wspec = lambda shape, which: pl.BlockSpec(
        (1,) + shape, (lambda i, ea, eb, nv: (ea[i], 0, 0)) if which == 0 else (lambda i, ea, eb, nv: (eb[i], 0, 0)))
    smem = lambda imap: pl.BlockSpec((1, 1, tm), imap, memory_space=pltpu.SMEM)
    return pl.pallas_call(
        _experts_kernel,
        grid_spec=pltpu.PrefetchScalarGridSpec(
            num_scalar_prefetch=3,
            grid=(n_tiles,),
            in_specs=[smem(lambda i, ea, eb, nv: (i, 0, 0)),
                      smem(lambda i, ea, eb, nv: (jnp.minimum(i + 1, n_tiles - 1), 0, 0)),
                      pl.BlockSpec(memory_space=pl.ANY),
                      wspec((d, D_EXPERT), 0), wspec((d, D_EXPERT), 0), wspec((D_EXPERT, d), 0),
                      wspec((d, D_EXPERT), 1), wspec((d, D_EXPERT), 1), wspec((D_EXPERT, d), 1)],
            out_specs=pl.BlockSpec((tm, d), lambda i, ea, eb, nv: (i, 0)),
            scratch_shapes=[pltpu.VMEM((2, tm, HX_WIDTH), F32), pltpu.SemaphoreType.DMA((2,))]),
        out_shape=jax.ShapeDtypeStruct((n_tiles * tm, d), F32),
        compiler_params=pltpu.CompilerParams(dimension_semantics=("arbitrary",),
                                             vmem_limit_bytes=VMEM_LIMIT),
        name="experts",
    )(tile_ea, tile_eb, n_valid, inv3, inv3, hx, w_gate, w_up, w_down, w_gate, w_up, w_down)


def _final_kernel(n_p, pos_cur, pos_nxt, f_hbm, xmid_ref, mod_ref, gpost_ref, yp_ref, ys_ref, fbuf, sem):
    i = pl.program_id(0)
    slot = i % 2

    @pl.when(i == 0)
    def _():
        _gather_rows(pos_cur, f_hbm, fbuf.at[0], sem.at[0])

    def compute():
        _wait_rows(f_hbm, fbuf.at[slot], sem.at[slot])
        y = xmid_ref[...] + mod_ref[0, 5:6, :] * _rms(fbuf[slot], gpost_ref[...])

        @pl.when(i < n_p)
        def _():
            yp_ref[...] = y

        @pl.when(i >= n_p)
        def _():
            ys_ref[...] = y

    @pl.when(i + 1 < pl.num_programs(0))
    def _():
        _gather_rows(pos_nxt, f_hbm, fbuf.at[1 - slot], sem.at[1 - slot], inline=True)
        compute()

    @pl.when(i + 1 == pl.num_programs(0))
    def _():
        compute()


def _final(rows, pos3, f_sorted, x_mid, mod, g_post):
    tm, d = rows.tm, D_MODEL
    smem = lambda imap: pl.BlockSpec((1, 1, tm), imap, memory_space=pltpu.SMEM)
    return pl.pallas_call(
        functools.partial(_final_kernel, rows.n_p),
        grid=(rows.n,),
        in_specs=[smem(lambda i: (i, 0, 0)),
                  smem(lambda i: (jnp.minimum(i + 1, rows.n - 1), 0, 0)),
                  pl.BlockSpec(memory_space=pl.ANY),
                  pl.BlockSpec((tm, d), lambda i: (i, 0)),
                  pl.BlockSpec((1, N_MOD, d), lambda i: (rows.seq(i), 0, 0)),
                  pl.BlockSpec((1, d), lambda i: (0, 0))],
        out_specs=[pl.BlockSpec((tm, d), lambda i: (rows.p_idx(i), 0)),
                   pl.BlockSpec((tm, d), lambda i: (rows.s_idx(i), 0))],
        out_shape=[jax.ShapeDtypeStruct((rows.tp, d), F32), jax.ShapeDtypeStruct((rows.ts, d), F32)],
        scratch_shapes=[pltpu.VMEM((2, tm, d), F32), pltpu.SemaphoreType.DMA((2,))],
        compiler_params=pltpu.CompilerParams(dimension_semantics=("arbitrary",),
                                             vmem_limit_bytes=VMEM_LIMIT),
        name="final",
    )(pos3, pos3, f_sorted, x_mid, mod, g_post)


def _moe_plan(meta, counts, tm):
    t = meta.shape[0]
    n_tiles = t // tm + N_BUCKETS
    cnt = counts[0, :N_BUCKETS].astype(jnp.int32)
    padded = (cnt + tm - 1) // tm * tm
    ends = jnp.cumsum(padded)
    buckets = jnp.arange(N_BUCKETS, dtype=jnp.int32)
    start_of = jnp.sum(jnp.where(meta[:, 0:1] == buckets[None, :], (ends - padded)[None, :], 0), axis=1)
    pos = start_of + meta[:, 1]
    inv = jnp.zeros((n_tiles * tm,), jnp.int32).at[pos].set(jnp.arange(t, dtype=jnp.int32))
    n_valid = ends[-1] // tm
    tile = jnp.minimum(jnp.arange(n_tiles, dtype=jnp.int32), n_valid - 1)
    bucket = jnp.sum((tile[:, None] * tm >= ends[None, :]).astype(jnp.int32), axis=1)
    group, pair = bucket // N_PAIRS, bucket % N_PAIRS
    ea = jnp.array([0, 0, 0, 1, 1, 2], jnp.int32)[pair]
    eb = jnp.array([1, 2, 3, 2, 3, 3], jnp.int32)[pair]
    base = group * EXPERTS_PER_GROUP
    return pos, inv, base + ea, base + eb, n_valid.reshape(1).astype(jnp.int32)


def _rotate_half_cols(w):
    half = MLA_ROPE_DIM // 2
    return jnp.concatenate([-w[..., half:], w[..., :half]], axis=-1)


def _rope_tables(s_max):
    half = MLA_ROPE_DIM // 2
    inv = ROPE_BASE ** (-jnp.arange(half, dtype=F32) / half)
    ang = jnp.arange(s_max, dtype=F32)[:, None] * inv[None, :]
    pad = jnp.zeros((s_max, LANES - MLA_ROPE_DIM), F32)
    cos, sin = jnp.cos(ang), jnp.sin(ang)
    return (jnp.concatenate([cos, cos, pad], axis=-1), jnp.concatenate([sin, sin, pad], axis=-1))


def _layer(x_prompt, x_sample, c_prompt, c_sample, layer_idx, w_ada, b_ada, g_pre_mix, g_post_mix, w_in,
           lambda_q1, lambda_k1, lambda_q2, lambda_k2, g_diff_sub, g_q_a, w_uq, g_kv_a, w_ukv, w_o,
           g_pre_ffn, g_post_ffn, w_router_group, b_router_group, w_router_expert, b_router_expert,
           w_gate, w_up, w_down, tm=256):
    bp, sp, d = x_prompt.shape
    bs, ss, _ = x_sample.shape
    rows = _Rows(bp, sp, bs, ss, tm)
    xp = x_prompt.reshape(bp * sp, d)
    xs = x_sample.reshape(bs * ss, d)
    lambda_init = 0.8 - 0.6 * math.exp(-0.3 * layer_idx)

    nb = bp + bs
    nb_pad = -(-nb // 8) * 8
    c_all = jnp.concatenate([c_prompt, c_sample, jnp.zeros((nb_pad - nb, d), F32)], axis=0)
    mod = _ada(c_all, w_ada, b_ada.reshape(1, -1)).reshape(nb_pad, N_MOD, d)

    w_kr = w_in[:, C_KR:C_KR + MLA_ROPE_DIM]
    w_kr = jnp.concatenate([w_kr, _rotate_half_cols(w_kr)], axis=1).astype(BF16)
    wq3 = w_uq.reshape(Q_LORA_RANK, MLA_HEADS, MLA_NOPE_DIM + MLA_ROPE_DIM)
    wq_rope = wq3[..., MLA_NOPE_DIM:]
    w_q = jnp.concatenate([wq3[..., :MLA_NOPE_DIM], wq_rope, _rotate_half_cols(wq_rope)], axis=-1)
    w_q = w_q.reshape(Q_LORA_RANK, MLA_HEADS * MLA_QK_PAD).astype(BF16)
    w_kv = w_ukv.astype(BF16)
    cos_t, sin_t = _rope_tables(max(sp, ss))

    dq, dk, dv, qm, km, vm = _inproj(rows, xp, xs, mod, g_pre_mix.reshape(1, d), w_in.astype(BF16), w_kr,
                                     g_q_a.reshape(1, -1), w_q, g_kv_a.reshape(1, -1), w_kv, cos_t, sin_t)

    slopes = jnp.array([2.0 ** (-8.0 * (i + 1) / DIFF_HEADS) for i in range(DIFF_HEADS)], dtype=F32)
    lam_p = jnp.zeros((8, LANES), F32).at[0:4, 0:DIFF_HEAD_DIM].set(
        jnp.stack([lambda_q1, lambda_k1, lambda_q2, lambda_k2]))
    g_sub = g_diff_sub.reshape(1, -1)
    oa_p = _diff_attn(dq, dk, dv, slopes, lam_p, g_sub, 0, bp, sp, lambda_init)
    oa_s = _diff_attn(dq, dk, dv, slopes, lam_p, g_sub, rows.tp, bs, ss, lambda_init)
    ob_p = _mla_attn(qm, km, vm, 0, bp, sp)
    ob_s = _mla_attn(qm, km, vm, rows.tp, bs, ss)

    w_r = jnp.concatenate([w_router_group, w_router_expert.reshape(d, N_EXPERTS),
                           jnp.zeros((d, LANES - N_GROUPS - N_EXPERTS), F32)], axis=1)
    b_r = jnp.concatenate([b_router_group, b_router_expert.reshape(N_EXPERTS),
                           jnp.zeros((LANES - N_GROUPS - N_EXPERTS,), F32)]).reshape(1, LANES)
    x_mid, hx, meta, counts = _oproj(rows, oa_p, oa_s, ob_p, ob_s, xp, xs, mod, w_o.astype(BF16),
                                     g_post_mix.reshape(1, d), g_pre_ffn.reshape(1, d), w_r, b_r)

    pos, inv, tile_ea, tile_eb, n_valid = _moe_plan(meta, counts, MOE_TM)
    f_sorted = _experts(hx, inv.reshape(-1, 1, MOE_TM), tile_ea, tile_eb, n_valid,
                        w_gate.astype(BF16), w_up.astype(BF16), w_down.astype(BF16))
    yp, ys = _final(rows, pos.reshape(rows.n, 1, tm), f_sorted, x_mid, mod, g_post_ffn.reshape(1, d))
    return yp.reshape(bp, sp, d), ys.reshape(bs, ss, d)


def kernel(x_prompt, x_sample, c_prompt, c_sample, w_ada, b_ada, g_pre_mix, g_post_mix, w_in, lambda_q1,
           lambda_k1, lambda_q2, lambda_k2, g_diff_sub, g_q_a, w_uq, g_kv_a, w_ukv, w_o, g_pre_ffn,
           g_post_ffn, w_router_group, b_router_group, w_router_expert, b_router_expert, w_gate, w_up,
           w_down):
    for l in range(w_ada.shape[0]):
        x_prompt, x_sample = _layer(
            x_prompt, x_sample, c_prompt, c_sample, l, w_ada[l], b_ada[l], g_pre_mix[l], g_post_mix[l],
            w_in[l], lambda_q1[l], lambda_k1[l], lambda_q2[l], lambda_k2[l], g_diff_sub[l], g_q_a[l],
            w_uq[l], g_kv_a[l], w_ukv[l], w_o[l], g_pre_ffn[l], g_post_ffn[l], w_router_group[l],
            b_router_group[l], w_router_expert[l], b_router_expert[l], w_gate[l], w_up[l], w_down[l])
    return x_prompt, x_sample
```

```python
import functools
import math

import jax
import jax.numpy as jnp
from jax import lax
from jax.experimental import pallas as pl
from jax.experimental.pallas import tpu as pltpu

F32 = jnp.float32
BF16 = jnp.bfloat16

D_MODEL = 2048
DIFF_HEADS = 8
DIFF_HEAD_DIM = 64
DIFF_WIDTH = DIFF_HEADS * 2 * DIFF_HEAD_DIM
MLA_HEADS = 8
MLA_NOPE_DIM = 128
MLA_ROPE_DIM = 64
MLA_V_DIM = 128
MLA_QK_PAD = 256
V_AUG = 256
Q_LORA_RANK = 512
KV_LORA_RANK = 256
ROPE_BASE = 10000.0
D_MLA_OUT = MLA_HEADS * MLA_V_DIM
N_GROUPS = 4
EXPERTS_PER_GROUP = 4
N_EXPERTS = N_GROUPS * EXPERTS_PER_GROUP
D_EXPERT = 512
N_MOD = 6
NORM_EPS = 1e-6
LANES = 128
ROUTER_LANE0 = N_GROUPS
N_PAIRS = 6
N_BUCKETS = N_GROUPS * N_PAIRS
HX_WIDTH = D_MODEL + LANES
MOE_TM = 256

C_DQ, C_DK, C_DV = 0, DIFF_WIDTH, 2 * DIFF_WIDTH
C_CQ = 3 * DIFF_WIDTH
C_CKV = C_CQ + Q_LORA_RANK
C_KR = C_CKV + KV_LORA_RANK
D_IN_PROJ = C_KR + MLA_ROPE_DIM

VMEM_LIMIT = 56 * 1024 * 1024
LOG2E = 1.4426950408889634
ATTN_TQ = 256
ATTN_TK = 512
ATTN_TQ_OUTER = 1024
ATTN_TKB = 2048
ATTN_UNROLL = 2
ATTN_GROUP = 4
ATTN_SCORE_BYTES = 32 * 1024 * 1024


def _rms(x, g):
    return x * lax.rsqrt(jnp.mean(x * x, axis=-1, keepdims=True) + NORM_EPS) * g


def _dot(a, b):
    return jnp.dot(a, b, preferred_element_type=F32)


def _dot_nt(a, b):
    return lax.dot_general(a, b, (((1,), (1,)), ((), ())), preferred_element_type=F32)


class _Rows:
    def __init__(self, bp, sp, bs, ss, tm):
        assert sp % tm == 0 and ss % tm == 0
        self.bp, self.sp, self.bs, self.ss, self.tm = bp, sp, bs, ss, tm
        self.n_p = bp * sp // tm
        self.n_s = bs * ss // tm
        self.n = self.n_p + self.n_s
        self.tp = bp * sp
        self.ts = bs * ss
        self.t = self.tp + self.ts

    def p_idx(self, i):
        return jnp.minimum(i, self.n_p - 1)

    def s_idx(self, i):
        return jnp.maximum(i - self.n_p, 0)

    def seq(self, i):
        return jnp.where(i < self.n_p, i // (self.sp // self.tm),
                         self.bp + (i - self.n_p) // (self.ss // self.tm))

    def pos(self, i):
        return jnp.where(i < self.n_p, i % (self.sp // self.tm),
                         (i - self.n_p) % (self.ss // self.tm))


def _ada_kernel(c_ref, w_ref, b_ref, o_ref):
    c = c_ref[...]
    a = c / (1.0 + jnp.exp(-c))
    o_ref[...] = jnp.dot(a, w_ref[...], preferred_element_type=F32,
                         precision=lax.Precision.HIGHEST) + b_ref[...]


def _ada(c_all, w_ada, b_ada):
    nb, d = c_all.shape
    n = w_ada.shape[1]
    tn = 1024
    return pl.pallas_call(
        _ada_kernel,
        grid=(n // tn,),
        in_specs=[pl.BlockSpec((nb, d), lambda j: (0, 0)),
                  pl.BlockSpec((d, tn), lambda j: (0, j)),
                  pl.BlockSpec((1, tn), lambda j: (0, j))],
        out_specs=pl.BlockSpec((nb, tn), lambda j: (0, j)),
        out_shape=jax.ShapeDtypeStruct((nb, n), F32),
        compiler_params=pltpu.CompilerParams(dimension_semantics=("arbitrary",),
                                             vmem_limit_bytes=VMEM_LIMIT),
        name="ada",
    )(c_all, w_ada, b_ada)


def _rope_half(x, cos, sin):
    return x * cos + pltpu.roll(x, MLA_ROPE_DIM, 1) * sin


def _inproj_kernel(n_p, xp_ref, xs_ref, mod_ref, gpre_ref, win_ref, wkr_ref, gqa_ref, wq_ref, gkva_ref,
                   wkv_ref, cos_ref, sin_ref, dq_ref, dk_ref, dv_ref, qm_ref, km_ref, vm_ref):
    i = pl.program_id(0)
    x = jnp.where(i < n_p, xp_ref[...], xs_ref[...])
    shift = mod_ref[0, 0:1, :]
    scale = mod_ref[0, 1:2, :]
    hb = (_rms(x, gpre_ref[...]) * (1.0 + scale) + shift).astype(BF16)

    dq_ref[...] = (_dot(hb, win_ref[:, C_DQ:C_DQ + DIFF_WIDTH]) * (DIFF_HEAD_DIM ** -0.5 * LOG2E)).astype(BF16)
    dk_ref[...] = _dot(hb, win_ref[:, C_DK:C_DK + DIFF_WIDTH]).astype(BF16)
    dv_ref[...] = _dot(hb, win_ref[:, C_DV:C_DV + DIFF_WIDTH]).astype(BF16)

    cos = cos_ref[...]
    sin = sin_ref[...]
    cq = _dot(hb, win_ref[:, C_CQ:C_CQ + Q_LORA_RANK])
    ckv = _dot(hb, win_ref[:, C_CKV:C_CKV + KV_LORA_RANK])
    kr = _rope_half(_dot(hb, wkr_ref[...]), cos, sin).astype(BF16)

    q = _dot(_rms(cq, gqa_ref[...]).astype(BF16), wq_ref[...])
    qscale = (MLA_NOPE_DIM + MLA_ROPE_DIM) ** -0.5 * LOG2E
    for h in range(MLA_HEADS):
        c0 = h * MLA_QK_PAD
        qm_ref[:, c0:c0 + LANES] = (q[:, c0:c0 + LANES] * qscale).astype(BF16)
        qm_ref[:, c0 + LANES:c0 + 2 * LANES] = (
            _rope_half(q[:, c0 + LANES:c0 + 2 * LANES], cos, sin) * qscale).astype(BF16)

    kv = _dot(_rms(ckv, gkva_ref[...]).astype(BF16), wkv_ref[...])
    for h in range(MLA_HEADS):
        c0 = h * MLA_QK_PAD
        km_ref[:, c0:c0 + LANES] = kv[:, c0:c0 + LANES].astype(BF16)
        km_ref[:, c0 + LANES:c0 + 2 * LANES] = kr
        vm_ref[:, h * MLA_V_DIM:(h + 1) * MLA_V_DIM] = kv[:, c0 + LANES:c0 + 2 * LANES].astype(BF16)


def _inproj(rows, xp, xs, mod, g_pre, w_in, w_kr, g_q_a, w_q, g_kv_a, w_kv, cos_t, sin_t):
    tm, d = rows.tm, D_MODEL
    const = lambda i: (0, 0)
    row = lambda i: (i, 0)
    one = pl.Buffered(1)
    out_w = (DIFF_WIDTH, DIFF_WIDTH, DIFF_WIDTH, MLA_HEADS * MLA_QK_PAD, MLA_HEADS * MLA_QK_PAD, D_MLA_OUT)
    return pl.pallas_call(
        functools.partial(_inproj_kernel, rows.n_p),
        grid=(rows.n,),
        in_specs=[pl.BlockSpec((tm, d), lambda i: (rows.p_idx(i), 0)),
                  pl.BlockSpec((tm, d), lambda i: (rows.s_idx(i), 0)),
                  pl.BlockSpec((1, N_MOD, d), lambda i: (rows.seq(i), 0, 0)),
                  pl.BlockSpec((1, d), const),
                  pl.BlockSpec((d, D_IN_PROJ), const, pipeline_mode=one),
                  pl.BlockSpec((d, 2 * MLA_ROPE_DIM), const),
                  pl.BlockSpec((1, Q_LORA_RANK), const),
                  pl.BlockSpec((Q_LORA_RANK, MLA_HEADS * MLA_QK_PAD), const, pipeline_mode=one),
                  pl.BlockSpec((1, KV_LORA_RANK), const),
                  pl.BlockSpec((KV_LORA_RANK, MLA_HEADS * MLA_QK_PAD), const, pipeline_mode=one),
                  pl.BlockSpec((tm, LANES), lambda i: (rows.pos(i), 0)),
                  pl.BlockSpec((tm, LANES), lambda i: (rows.pos(i), 0))],
        out_specs=[pl.BlockSpec((tm, w), row) for w in out_w],
        out_shape=[jax.ShapeDtypeStruct((rows.t, w), BF16) for w in out_w],
        compiler_params=pltpu.CompilerParams(dimension_semantics=("arbitrary",),
                                             vmem_limit_bytes=VMEM_LIMIT),
        name="inproj",
    )(xp, xs, mod, g_pre, w_in, w_kr, g_q_a, w_q, g_kv_a, w_kv, cos_t, sin_t)


def _lane_groups(x, op):
    out = x[:, 0:LANES]
    for g in range(1, x.shape[1] // LANES):
        out = op(out, x[:, g * LANES:(g + 1) * LANES])
    return out


def _aligned(x, m):
    return x if isinstance(x, int) else pl.multiple_of(x, m)


def _chunk(ref, j, tk):
    return ref[pl.ds(_aligned(j * tk, tk), tk), :]


def _score_block(unit, s_view, jb, cpb, tk, mpart):
    score_fn, shift_fn = unit
    for c in range(cpb):
        j = jb * cpb + c
        s = score_fn(j)
        s_view[jb, :, c * tk:(c + 1) * tk] = s
        part = _lane_groups(s, jnp.maximum)
        if shift_fn is not None:
            part = part + shift_fn(j)
        mpart = jnp.maximum(mpart, part)
    return mpart


def _value_block(unit, m, v_ref, s_view, jb, cpb, tk, acc):
    _, shift_fn = unit
    ps = []
    for c in range(cpb):
        j = jb * cpb + c
        mj = m if shift_fn is None else m - shift_fn(j)
        ps.append(jnp.exp2(s_view[jb, :, c * tk:(c + 1) * tk] - mj).astype(BF16))
    p_all = ps[0] if cpb == 1 else jnp.concatenate(ps, axis=1)
    return acc + _dot(p_all, _chunk(v_ref, jb, cpb * tk))


def _attn_phases(geom, v_ref, score=(), value=()):
    tq, tk, cpb, nkb = geom
    zeros = jnp.zeros((tq, 2 * LANES), F32)
    neg_inf = jnp.full((tq, LANES), -jnp.inf, F32)

    def body(jb, carry):
        mparts, accs = carry
        mparts = tuple(_score_block(u, view, jb, cpb, tk, mp) for (u, view), mp in zip(score, mparts))
        accs = tuple(_value_block(u, m, v_ref, view, jb, cpb, tk, acc) for (u, m, view), acc in zip(value, accs))
        return mparts, accs

    init = ((neg_inf,) * len(score), (zeros,) * len(value))
    mparts, accs = lax.fori_loop(0, nkb, body, init, unroll=min(ATTN_UNROLL, nkb))
    return ([jnp.max(mp, axis=-1, keepdims=True) for mp in mparts],
            [acc[:, 0:LANES] / acc[:, LANES:LANES + 1] for acc in accs])


def _fill_value_rows(v_ref, vaug):
    lane = lax.broadcasted_iota(jnp.int32, v_ref.shape, 1)
    vaug[:, 0:LANES] = v_ref[...]
    vaug[:, LANES:2 * LANES] = jnp.where(lane == 0, 1.0, 0.0).astype(BF16)


def _attn_grouped(geom, v_ref, units, s_buf):
    views = [s_buf.at[k] for k in range(len(units))]
    ms, _ = _attn_phases(geom, v_ref, score=tuple(zip(units, views)))
    _, outs = _attn_phases(geom, v_ref, value=tuple(zip(units, ms, views)))
    return outs


def _diff_attn_kernel(lambda_init, geom, n_sub, slopes_ref, lam_ref, gsub_ref, q_ref, k_ref, v_ref,
                      o_ref, s_buf, bias_buf, vaug):
    tq, tk, _, _ = geom
    h = pl.program_id(1)
    qo = pl.program_id(2)
    slope = slopes_ref[h] * LOG2E
    n_cross = tk // tq

    @pl.when(qo == 0)
    def _():
        _fill_value_rows(v_ref, vaug)
        r_minus_c = (lax.broadcasted_iota(jnp.int32, (tq, tk), 0)
                     - lax.broadcasted_iota(jnp.int32, (tq, tk), 1)).astype(F32)
        bias_buf[0] = r_minus_c * (-slope)
        bias_buf[1] = r_minus_c * slope
        for c in range(n_cross):
            bias_buf[2 + c] = jnp.abs(r_minus_c + float(c * tq)) * (-slope)

    lp = lam_ref[...]
    lam = (jnp.exp(jnp.sum(lp[0:1] * lp[1:2], axis=-1, keepdims=True))
           - jnp.exp(jnp.sum(lp[2:3] * lp[3:4], axis=-1, keepdims=True)) + lambda_init)
    gsub = gsub_ref[...]

    def unit(sub, half):
        qbase = (qo * n_sub + sub) * tq
        jc = qbase // tk
        cross = 2 + (qbase - jc * tk) // tq
        q = q_ref[pl.ds(_aligned(sub * tq, tq), tq), :]
        lane = lax.broadcasted_iota(jnp.int32, q.shape, 1)
        keep = (lane < DIFF_HEAD_DIM) if half == 0 else (lane >= DIFF_HEAD_DIM)
        qh = jnp.where(keep, q, jnp.zeros_like(q))
        shift = lambda j: jnp.where(j == jc, 0.0, -slope * jnp.abs(qbase - j * tk).astype(F32))
        which = lambda j: jnp.where(j < jc, 0, jnp.where(j > jc, 1, cross))
        return (lambda j: _dot_nt(qh, _chunk(k_ref, j, tk)) + bias_buf[which(j)]), shift

    group = s_buf.shape[0] // 2

    def grouped(g, carry):
        subs = [g * group + k for k in range(group)]
        outs = _attn_grouped(geom, vaug, [unit(sub, half) for sub in subs for half in range(2)], s_buf)
        for k, sub in enumerate(subs):
            o = outs[2 * k] - lam * outs[2 * k + 1]
            o_ref[pl.ds(_aligned(sub * tq, tq), tq), :] = (_rms(o, gsub) * (1.0 - lambda_init)).astype(BF16)
        return carry

    lax.fori_loop(0, n_sub // group, grouped, 0)


def _attn_tiles(s, row0, units_per_sub):
    tq = min(ATTN_TQ, s)
    tk = min(ATTN_TK, s)
    tkb = min(ATTN_TKB, s)
    tq_outer = min(ATTN_TQ_OUTER, s)
    assert s % tq_outer == 0 and tq_outer % tq == 0 and s % tkb == 0 and tkb % tk == 0 and tk % tq == 0
    assert row0 % s == 0
    nkb, n_sub = s // tkb, tq_outer // tq
    units = max(units_per_sub, min(ATTN_GROUP, ATTN_SCORE_BYTES // (tq * s * 4)))
    group = min(units // units_per_sub, n_sub)
    assert n_sub % group == 0
    return (tq, tk, tkb // tk, nkb), tq_outer, (group * units_per_sub, nkb, tq, tkb)


def _diff_attn(dq, dk, dv, slopes, lam_p, g_sub, row0, b, s, lambda_init):
    geom, tq_outer, s_buf_shape = _attn_tiles(s, row0, 2)
    tq, tk = geom[0], geom[1]
    n_sub, nqo = tq_outer // tq, s // tq_outer
    dh = 2 * DIFF_HEAD_DIM
    return pl.pallas_call(
        functools.partial(_diff_attn_kernel, lambda_init, geom, n_sub),
        grid=(b, DIFF_HEADS, nqo),
        in_specs=[pl.BlockSpec(memory_space=pltpu.SMEM),
                  pl.BlockSpec((8, LANES), lambda bi, h, qi: (0, 0)),
                  pl.BlockSpec((1, dh), lambda bi, h, qi: (0, 0)),
                  pl.BlockSpec((tq_outer, dh), lambda bi, h, qi: (row0 // tq_outer + bi * nqo + qi, h)),
                  pl.BlockSpec((s, dh), lambda bi, h, qi: (row0 // s + bi, h)),
                  pl.BlockSpec((s, dh), lambda bi, h, qi: (row0 // s + bi, h))],
        out_specs=pl.BlockSpec((tq_outer, dh), lambda bi, h, qi: (bi * nqo + qi, h)),
        out_shape=jax.ShapeDtypeStruct((b * s, DIFF_WIDTH), BF16),
        scratch_shapes=[pltpu.VMEM(s_buf_shape, F32), pltpu.VMEM((2 + tk // tq, tq, tk), F32),
                        pltpu.VMEM((s, V_AUG), BF16)],
        compiler_params=pltpu.CompilerParams(
            dimension_semantics=("arbitrary", "arbitrary", "arbitrary"), vmem_limit_bytes=VMEM_LIMIT),
        name="diff_attn",
    )(slopes, lam_p, g_sub, dq, dk, dv)


def _mla_attn_kernel(geom, n_sub, q_ref, k_ref, v_ref, o_ref, s_buf, vaug):
    tq, tk, _, _ = geom

    @pl.when(pl.program_id(2) == 0)
    def _():
        _fill_value_rows(v_ref, vaug)

    def unit(sub):
        q = q_ref[pl.ds(_aligned(sub * tq, tq), tq), :]
        return (lambda j: _dot_nt(q, _chunk(k_ref, j, tk))), None

    group = s_buf.shape[0]

    def grouped(g, carry):
        subs = [g * group + k for k in range(group)]
        for sub, o in zip(subs, _attn_grouped(geom, vaug, [unit(sub) for sub in subs], s_buf)):
            o_ref[pl.ds(_aligned(sub * tq, tq), tq), :] = o.astype(BF16)
        return carry

    lax.fori_loop(0, n_sub // group, grouped, 0)


def _mla_attn(qm, km, vm, row0, b, s):
    geom, tq_outer, s_buf_shape = _attn_tiles(s, row0, 1)
    n_sub, nqo = tq_outer // geom[0], s // tq_outer
    return pl.pallas_call(
        functools.partial(_mla_attn_kernel, geom, n_sub),
        grid=(b, MLA_HEADS, nqo),
        in_specs=[pl.BlockSpec((tq_outer, MLA_QK_PAD), lambda bi, h, qi: (row0 // tq_outer + bi * nqo + qi, h)),
                  pl.BlockSpec((s, MLA_QK_PAD), lambda bi, h, qi: (row0 // s + bi, h)),
                  pl.BlockSpec((s, MLA_V_DIM), lambda bi, h, qi: (row0 // s + bi, h))],
        out_specs=pl.BlockSpec((tq_outer, MLA_V_DIM), lambda bi, h, qi: (bi * nqo + qi, h)),
        out_shape=jax.ShapeDtypeStruct((b * s, D_MLA_OUT), BF16),
        scratch_shapes=[pltpu.VMEM(s_buf_shape, F32), pltpu.VMEM((s, V_AUG), BF16)],
        compiler_params=pltpu.CompilerParams(
            dimension_semantics=("arbitrary", "arbitrary", "arbitrary"), vmem_limit_bytes=VMEM_LIMIT),
        name="mla_attn",
    )(qm, km, vm)


def _route(logits):
    lane = lax.broadcasted_iota(jnp.int32, logits.shape, 1)
    neg = jnp.full(logits.shape, -jnp.inf, F32)
    big = jnp.full(logits.shape, LANES, jnp.int32)
    first = lambda mask: jnp.min(jnp.where(mask, lane, big), axis=-1, keepdims=True)

    gl = jnp.where(lane < N_GROUPS, logits, neg)
    gmax = jnp.max(gl, axis=-1, keepdims=True)
    g_idx = first(gl == gmax)
    g_w = 1.0 / jnp.sum(jnp.exp(gl - gmax), axis=-1, keepdims=True)

    lo = ROUTER_LANE0 + EXPERTS_PER_GROUP * g_idx
    el = jnp.where(lane >= lo, jnp.where(lane < lo + EXPERTS_PER_GROUP, logits, neg), neg)
    v1 = jnp.max(el, axis=-1, keepdims=True)
    i1 = first(el == v1)
    el2 = jnp.where(lane == i1, neg, el)
    v2 = jnp.max(el2, axis=-1, keepdims=True)
    i2 = first(el2 == v2)
    t = jnp.exp(v2 - v1)
    w1 = g_w / (1.0 + t)
    w2 = w1 * t
    first_low = i1 < i2
    ea = jnp.minimum(i1, i2) - lo
    eb = jnp.maximum(i1, i2) - lo
    pair = jnp.where(ea == 0, 0, jnp.where(ea == 1, 3, 5)) + eb - ea - 1
    bucket = g_idx * N_PAIRS + pair
    return bucket, jnp.where(first_low, w1, w2), jnp.where(first_low, w2, w1)


def _oproj_kernel(n_p, oap_ref, oas_ref, obp_ref, obs_ref, xp_ref, xs_ref, mod_ref, wo_ref, gpost_ref,
                  gpre_ref, wr_ref, br_ref, xmid_ref, hx_ref, meta_ref, counts_ref, cnt):
    i = pl.program_id(0)

    @pl.when(i == 0)
    def _():
        cnt[...] = jnp.zeros(cnt.shape, F32)

    is_p = i < n_p
    oa = jnp.where(is_p, oap_ref[...], oas_ref[...])
    ob = jnp.where(is_p, obp_ref[...], obs_ref[...])
    x = jnp.where(is_p, xp_ref[...], xs_ref[...])
    mix = _dot(oa, wo_ref[0:DIFF_WIDTH, :]) + _dot(ob, wo_ref[DIFF_WIDTH:DIFF_WIDTH + D_MLA_OUT, :])
    x_mid = x + mod_ref[0, 2:3, :] * _rms(mix, gpost_ref[...])
    xmid_ref[...] = x_mid
    h2 = _rms(x_mid, gpre_ref[...]) * (1.0 + mod_ref[0, 4:5, :]) + mod_ref[0, 3:4, :]
    w_r = wr_ref[...]
    w_hi = w_r.astype(BF16)
    w_hi_lo = jnp.concatenate([w_hi, (w_r - w_hi.astype(F32)).astype(BF16)], axis=1)
    h_hi = h2.astype(BF16)
    h_lo = (h2 - h_hi.astype(F32)).astype(BF16)
    hi_terms = _dot(h_hi, w_hi_lo)
    logits = hi_terms[:, 0:LANES] + hi_terms[:, LANES:2 * LANES] + _dot(h_lo, w_hi) + br_ref[...]
    bucket, wa, wb = _route(logits)

    tm = logits.shape[0]
    lane = lax.broadcasted_iota(jnp.int32, logits.shape, 1)
    hx_ref[:, 0:D_MODEL] = h2
    hx_ref[:, D_MODEL:HX_WIDTH] = jnp.where(lane == 0, wa, jnp.where(lane == 1, wb, jnp.zeros_like(logits)))

    onehot = jnp.where(lane == bucket, 1.0, 0.0)
    earlier = (lax.broadcasted_iota(jnp.int32, (tm, tm), 0) > lax.broadcasted_iota(jnp.int32, (tm, tm), 1))
    before = _dot(jnp.where(earlier, 1.0, 0.0).astype(BF16), onehot.astype(BF16))
    rank = jnp.sum(onehot * (before + cnt[0:1, :]), axis=-1, keepdims=True)
    cnt[...] = cnt[...] + jnp.sum(onehot, axis=0, keepdims=True)
    meta_ref[...] = jnp.where(lane == 0, bucket, jnp.where(lane == 1, rank.astype(jnp.int32), 0))
    counts_ref[...] = cnt[...]


def _oproj(rows, oa_p, oa_s, ob_p, ob_s, xp, xs, mod, w_o, g_post, g_pre, w_r, b_r):
    tm, d = rows.tm, D_MODEL
    const = lambda i: (0, 0)
    row = lambda i: (i, 0)
    prow = lambda i: (rows.p_idx(i), 0)
    srow = lambda i: (rows.s_idx(i), 0)
    return pl.pallas_call(
        functools.partial(_oproj_kernel, rows.n_p),
        grid=(rows.n,),
        in_specs=[pl.BlockSpec((tm, DIFF_WIDTH), prow), pl.BlockSpec((tm, DIFF_WIDTH), srow),
                  pl.BlockSpec((tm, D_MLA_OUT), prow), pl.BlockSpec((tm, D_MLA_OUT), srow),
                  pl.BlockSpec((tm, d), prow), pl.BlockSpec((tm, d), srow),
                  pl.BlockSpec((1, N_MOD, d), lambda i: (rows.seq(i), 0, 0)),
                  pl.BlockSpec((DIFF_WIDTH + D_MLA_OUT, d), const, pipeline_mode=pl.Buffered(1)),
                  pl.BlockSpec((1, d), const), pl.BlockSpec((1, d), const),
                  pl.BlockSpec((d, LANES), const), pl.BlockSpec((1, LANES), const)],
        out_specs=[pl.BlockSpec((tm, d), row), pl.BlockSpec((tm, HX_WIDTH), row),
                   pl.BlockSpec((tm, LANES), row), pl.BlockSpec((8, LANES), const)],
        out_shape=[jax.ShapeDtypeStruct((rows.t, d), F32), jax.ShapeDtypeStruct((rows.t, HX_WIDTH), F32),
                   jax.ShapeDtypeStruct((rows.t, LANES), jnp.int32), jax.ShapeDtypeStruct((8, LANES), F32)],
        scratch_shapes=[pltpu.VMEM((8, LANES), F32)],
        compiler_params=pltpu.CompilerParams(dimension_semantics=("arbitrary",),
                                             vmem_limit_bytes=VMEM_LIMIT),
        name="oproj",
    )(oa_p, oa_s, ob_p, ob_s, xp, xs, mod, w_o, g_post, g_pre, w_r, b_r)


def _gather_rows(idx_ref, src_hbm, dst, sem, inline=False):
    def body(r, carry):
        pltpu.make_async_copy(src_hbm.at[pl.ds(idx_ref[0, 0, r], 1), :], dst.at[pl.ds(r, 1), :], sem).start()
        return carry

    if inline:
        for r in range(dst.shape[0]):
            body(r, 0)
    else:
        lax.fori_loop(0, dst.shape[0], body, 0, unroll=8)


def _wait_rows(src_hbm, dst, sem):
    pltpu.make_async_copy(src_hbm.at[pl.ds(0, dst.shape[0]), :], dst, sem).wait()


def _experts_kernel(ea_ref, eb_ref, nv_ref, inv_cur, inv_nxt, hx_hbm, wga, wua, wda, wgb, wub, wdb,
                    f_ref, xbuf, sem):
    i = pl.program_id(0)
    n_valid = nv_ref[0]
    slot = i % 2

    @pl.when(i == 0)
    def _():
        _gather_rows(inv_cur, hx_hbm, xbuf.at[0], sem.at[0])

    def compute():
        _wait_rows(hx_hbm, xbuf.at[slot], sem.at[slot])
        x = xbuf[slot, :, 0:D_MODEL].astype(BF16)
        aux = xbuf[slot, :, D_MODEL:HX_WIDTH]

        def hidden(wg, wu, w):
            g = _dot(x, wg[0])
            return ((g / (1.0 + jnp.exp(-g))) * _dot(x, wu[0]) * w).astype(BF16)

        f_ref[...] = (_dot(hidden(wga, wua, aux[:, 0:1]), wda[0])
                      + _dot(hidden(wgb, wub, aux[:, 1:2]), wdb[0]))

    @pl.when(i + 1 < n_valid)
    def _():
        _gather_rows(inv_nxt, hx_hbm, xbuf.at[1 - slot], sem.at[1 - slot], inline=True)
        compute()

    @pl.when(i + 1 == n_valid)
    def _():
        compute()

    @pl.when(i >= n_valid)
    def _():
        f_ref[...] = jnp.zeros(f_ref.shape, F32)


def _experts(hx, inv3, tile_ea, tile_eb, n_valid, w_gate, w_up, w_down):
    n_tiles, _, tm = inv3.shape
    d = D_MODEL
    wspec = lambda shape, which: pl.BlockSpec(
        (1,) + shape, (lambda i, ea, eb, nv: (ea[i], 0, 0)) if which == 0 else (lambda i, ea, eb, nv: (eb[i], 0, 0)))
    smem = lambda imap: pl.BlockSpec((1, 1, tm), imap, memory_space=pltpu.SMEM)
    return pl.pallas_call(
        _experts_kernel,
        grid_spec=pltpu.PrefetchScalarGridSpec(
            num_scalar_prefetch=3,
            grid=(n_tiles,),
            in_specs=[smem(lambda i, ea, eb, nv: (i, 0, 0)),
                      smem(lambda i, ea, eb, nv: (jnp.minimum(i + 1, n_tiles - 1), 0, 0)),
                      pl.BlockSpec(memory_space=pl.ANY),
                      wspec((d, D_EXPERT), 0), wspec((d, D_EXPERT), 0), wspec((D_EXPERT, d), 0),
                      wspec((d, D_EXPERT), 1), wspec((d, D_EXPERT), 1), wspec((D_EXPERT, d), 1)],
            out_specs=pl.BlockSpec((tm, d), lambda i, ea, eb, nv: (i, 0)),
            scratch_shapes=[pltpu.VMEM((2, tm, HX_WIDTH), F32), pltpu.SemaphoreType.DMA((2,))]),
        out_shape=jax.ShapeDtypeStruct((n_tiles * tm, d), F32),
        compiler_params=pltpu.CompilerParams(dimension_semantics=("arbitrary",),
                                             vmem_limit_bytes=VMEM_LIMIT),
        name="experts",
    )(tile_ea, tile_eb, n_valid, inv3, inv3, hx, w_gate, w_up, w_down, w_gate, w_up, w_down)


def _final_kernel(n_p, pos_cur, pos_nxt, f_hbm, xmid_ref, mod_ref, gpost_ref, yp_ref, ys_ref, fbuf, sem):
    i = pl.program_id(0)
    slot = i % 2

    @pl.when(i == 0)
    def _():
        _gather_rows(pos_cur, f_hbm, fbuf.at[0], sem.at[0])

    def compute():
        _wait_rows(f_hbm, fbuf.at[slot], sem.at[slot])
        y = xmid_ref[...] + mod_ref[0, 5:6, :] * _rms(fbuf[slot], gpost_ref[...])

        @pl.when(i < n_p)
        def _():
            yp_ref[...] = y

        @pl.when(i >= n_p)
        def _():
            ys_ref[...] = y

    @pl.when(i + 1 < pl.num_programs(0))
    def _():
        _gather_rows(pos_nxt, f_hbm, fbuf.at[1 - slot], sem.at[1 - slot], inline=True)
        compute()

    @pl.when(i + 1 == pl.num_programs(0))
    def _():
        compute()


def _final(rows, pos3, f_sorted, x_mid, mod, g_post):
    tm, d = rows.tm, D_MODEL
    smem = lambda imap: pl.BlockSpec((1, 1, tm), imap, memory_space=pltpu.SMEM)
    return pl.pallas_call(
        functools.partial(_final_kernel, rows.n_p),
        grid=(rows.n,),
        in_specs=[smem(lambda i: (i, 0, 0)),
                  smem(lambda i: (jnp.minimum(i + 1, rows.n - 1), 0, 0)),
                  pl.BlockSpec(memory_space=pl.ANY),
                  pl.BlockSpec((tm, d), lambda i: (i, 0)),
                  pl.BlockSpec((1, N_MOD, d), lambda i: (rows.seq(i), 0, 0)),
                  pl.BlockSpec((1, d), lambda i: (0, 0))],
        out_specs=[pl.BlockSpec((tm, d), lambda i: (rows.p_idx(i), 0)),
                   pl.BlockSpec((tm, d), lambda i: (rows.s_idx(i), 0))],
        out_shape=[jax.ShapeDtypeStruct((rows.tp, d), F32), jax.ShapeDtypeStruct((rows.ts, d), F32)],
        scratch_shapes=[pltpu.VMEM((2, tm, d), F32), pltpu.SemaphoreType.DMA((2,))],
        compiler_params=pltpu.CompilerParams(dimension_semantics=("arbitrary",),
                                             vmem_limit_bytes=VMEM_LIMIT),
        name="final",
    )(pos3, pos3, f_sorted, x_mid, mod, g_post)


def _moe_plan(meta, counts, tm):
    t = meta.shape[0]
    n_tiles = t // tm + N_BUCKETS
    cnt = counts[0, :N_BUCKETS].astype(jnp.int32)
    padded = (cnt + tm - 1) // tm * tm
    ends = jnp.cumsum(padded)
    buckets = jnp.arange(N_BUCKETS, dtype=jnp.int32)
    start_of = jnp.sum(jnp.where(meta[:, 0:1] == buckets[None, :], (ends - padded)[None, :], 0), axis=1)
    pos = start_of + meta[:, 1]
    inv = jnp.zeros((n_tiles * tm,), jnp.int32).at[pos].set(jnp.arange(t, dtype=jnp.int32))
    n_valid = ends[-1] // tm
    tile = jnp.minimum(jnp.arange(n_tiles, dtype=jnp.int32), n_valid - 1)
    bucket = jnp.sum((tile[:, None] * tm >= ends[None, :]).astype(jnp.int32), axis=1)
    group, pair = bucket // N_PAIRS, bucket % N_PAIRS
    ea = jnp.array([0, 0, 0, 1, 1, 2], jnp.int32)[pair]
    eb = jnp.array([1, 2, 3, 2, 3, 3], jnp.int32)[pair]
    base = group * EXPERTS_PER_GROUP
    return pos, inv, base + ea, base + eb, n_valid.reshape(1).astype(jnp.int32)


def _rotate_half_cols(w):
    half = MLA_ROPE_DIM // 2
    return jnp.concatenate([-w[..., half:], w[..., :half]], axis=-1)


def _rope_tables(s_max):
    half = MLA_ROPE_DIM // 2
    inv = ROPE_BASE ** (-jnp.arange(half, dtype=F32) / half)
    ang = jnp.arange(s_max, dtype=F32)[:, None] * inv[None, :]
    pad = jnp.zeros((s_max, LANES - MLA_ROPE_DIM), F32)
    cos, sin = jnp.cos(ang), jnp.sin(ang)
    return (jnp.concatenate([cos, cos, pad], axis=-1), jnp.concatenate([sin, sin, pad], axis=-1))


def _layer(x_prompt, x_sample, c_prompt, c_sample, layer_idx, w_ada, b_ada, g_pre_mix, g_post_mix, w_in,
           lambda_q1, lambda_k1, lambda_q2, lambda_k2, g_diff_sub, g_q_a, w_uq, g_kv_a, w_ukv, w_o,
           g_pre_ffn, g_post_ffn, w_router_group, b_router_group, w_router_expert, b_router_expert,
           w_gate, w_up, w_down, tm=256):
    bp, sp, d = x_prompt.shape
    bs, ss, _ = x_sample.shape
    rows = _Rows(bp, sp, bs, ss, tm)
    xp = x_prompt.reshape(bp * sp, d)
    xs = x_sample.reshape(bs * ss, d)
    lambda_init = 0.8 - 0.6 * math.exp(-0.3 * layer_idx)

    nb = bp + bs
    nb_pad = -(-nb // 8) * 8
    c_all = jnp.concatenate([c_prompt, c_sample, jnp.zeros((nb_pad - nb, d), F32)], axis=0)
    mod = _ada(c_all, w_ada, b_ada.reshape(1, -1)).reshape(nb_pad, N_MOD, d)

    w_kr = w_in[:, C_KR:C_KR + MLA_ROPE_DIM]
    w_kr = jnp.concatenate([w_kr, _rotate_half_cols(w_kr)], axis=1).astype(BF16)
    wq3 = w_uq.reshape(Q_LORA_RANK, MLA_HEADS, MLA_NOPE_DIM + MLA_ROPE_DIM)
    wq_rope = wq3[..., MLA_NOPE_DIM:]
    w_q = jnp.concatenate([wq3[..., :MLA_NOPE_DIM], wq_rope, _rotate_half_cols(wq_rope)], axis=-1)
    w_q = w_q.reshape(Q_LORA_RANK, MLA_HEADS * MLA_QK_PAD).astype(BF16)
    w_kv = w_ukv.astype(BF16)
    cos_t, sin_t = _rope_tables(max(sp, ss))

    dq, dk, dv, qm, km, vm = _inproj(rows, xp, xs, mod, g_pre_mix.reshape(1, d), w_in.astype(BF16), w_kr,
                                     g_q_a.reshape(1, -1), w_q, g_kv_a.reshape(1, -1), w_kv, cos_t, sin_t)

    slopes = jnp.array([2.0 ** (-8.0 * (i + 1) / DIFF_HEADS) for i in range(DIFF_HEADS)], dtype=F32)
    lam_p = jnp.zeros((8, LANES), F32).at[0:4, 0:DIFF_HEAD_DIM].set(
        jnp.stack([lambda_q1, lambda_k1, lambda_q2, lambda_k2]))
    g_sub = g_diff_sub.reshape(1, -1)
    oa_p = _diff_attn(dq, dk, dv, slopes, lam_p, g_sub, 0, bp, sp, lambda_init)
    oa_s = _diff_attn(dq, dk, dv, slopes, lam_p, g_sub, rows.tp, bs, ss, lambda_init)
    ob_p = _mla_attn(qm, km, vm, 0, bp, sp)
    ob_s = _mla_attn(qm, km, vm, rows.tp, bs, ss)

    w_r = jnp.concatenate([w_router_group, w_router_expert.reshape(d, N_EXPERTS),
                           jnp.zeros((d, LANES - N_GROUPS - N_EXPERTS), F32)], axis=1)
    b_r = jnp.concatenate([b_router_group, b_router_expert.reshape(N_EXPERTS),
                           jnp.zeros((LANES - N_GROUPS - N_EXPERTS,), F32)]).reshape(1, LANES)
    x_mid, hx, meta, counts = _oproj(rows, oa_p, oa_s, ob_p, ob_s, xp, xs, mod, w_o.astype(BF16),
                                     g_post_mix.reshape(1, d), g_pre_ffn.reshape(1, d), w_r, b_r)

    pos, inv, tile_ea, tile_eb, n_valid = _moe_plan(meta, counts, MOE_TM)
    f_sorted = _experts(hx, inv.reshape(-1, 1, MOE_TM), tile_ea, tile_eb, n_valid,
                        w_gate.astype(BF16), w_up.astype(BF16), w_down.astype(BF16))
    yp, ys = _final(rows, pos.reshape(rows.n, 1, tm), f_sorted, x_mid, mod, g_post_ffn.reshape(1, d))
    return yp.reshape(bp, sp, d), ys.reshape(bs, ss, d)


def kernel(x_prompt, x_sample, c_prompt, c_sample, w_ada, b_ada, g_pre_mix, g_post_mix, w_in, lambda_q1,
           lambda_k1, lambda_q2, lambda_k2, g_diff_sub, g_q_a, w_uq, g_kv_a, w_ukv, w_o, g_pre_ffn,
           g_post_ffn, w_router_group, b_router_group, w_router_expert, b_router_expert, w_gate, w_up,
           w_down):
    for l in range(w_ada.shape[0]):
        x_prompt, x_sample = _layer(
            x_prompt, x_sample, c_prompt, c_sample, l, w_ada[l], b_ada[l], g_pre_mix[l], g_post_mix[l],
            w_in[l], lambda_q1[l], lambda_k1[l], lambda_q2[l], lambda_k2[l], g_diff_sub[l], g_q_a[l],
            w_uq[l], g_kv_a[l], w_ukv[l], w_o[l], g_pre_ffn[l], g_post_ffn[l], w_router_group[l],
            b_router_group[l], w_router_expert[l], b_router_expert[l], w_gate[l], w_up[l], w_down[l])
    return x_prompt, x_sample
```

```python
import functools
import math

import jax
import jax.numpy as jnp
from jax import lax
from jax.experimental import pallas as pl
from jax.experimental.pallas import tpu as pltpu

F32 = jnp.float32
BF16 = jnp.bfloat16

D_MODEL = 2048
DIFF_HEADS = 8
DIFF_HEAD_DIM = 64
DIFF_WIDTH = DIFF_HEADS * 2 * DIFF_HEAD_DIM
MLA_HEADS = 8
MLA_NOPE_DIM = 128
MLA_ROPE_DIM = 64
MLA_V_DIM = 128
MLA_QK_PAD = 256
V_AUG = 256
Q_LORA_RANK = 512
KV_LORA_RANK = 256
ROPE_BASE = 10000.0
D_MLA_OUT = MLA_HEADS * MLA_V_DIM
N_GROUPS = 4
EXPERTS_PER_GROUP = 4
N_EXPERTS = N_GROUPS * EXPERTS_PER_GROUP
D_EXPERT = 512
N_MOD = 6
NORM_EPS = 1e-6
LANES = 128
ROUTER_LANE0 = N_GROUPS
N_PAIRS = 6
N_BUCKETS = N_GROUPS * N_PAIRS
HX_WIDTH = D_MODEL + LANES
MOE_TM = 512

C_DQ, C_DK, C_DV = 0, DIFF_WIDTH, 2 * DIFF_WIDTH
C_CQ = 3 * DIFF_WIDTH
C_CKV = C_CQ + Q_LORA_RANK
C_KR = C_CKV + KV_LORA_RANK
D_IN_PROJ = C_KR + MLA_ROPE_DIM

VMEM_LIMIT = 56 * 1024 * 1024
LOG2E = 1.4426950408889634
ATTN_TQ = 256
ATTN_TK = 512
ATTN_TQ_OUTER = 1024
ATTN_TKB = 2048
ATTN_UNROLL = 2
ATTN_GROUP = 4
ATTN_SCORE_BYTES = 32 * 1024 * 1024


def _rms(x, g):
    return x * lax.rsqrt(jnp.mean(x * x, axis=-1, keepdims=True) + NORM_EPS) * g


def _dot(a, b):
    return jnp.dot(a, b, preferred_element_type=F32)


def _dot_nt(a, b):
    return lax.dot_general(a, b, (((1,), (1,)), ((), ())), preferred_element_type=F32)


class _Rows:
    def __init__(self, bp, sp, bs, ss, tm):
        assert sp % tm == 0 and ss % tm == 0
        self.bp, self.sp, self.bs, self.ss, self.tm = bp, sp, bs, ss, tm
        self.n_p = bp * sp // tm
        self.n_s = bs * ss // tm
        self.n = self.n_p + self.n_s
        self.tp = bp * sp
        self.ts = bs * ss
        self.t = self.tp + self.ts

    def p_idx(self, i):
        return jnp.minimum(i, self.n_p - 1)

    def s_idx(self, i):
        return jnp.maximum(i - self.n_p, 0)

    def seq(self, i):
        return jnp.where(i < self.n_p, i // (self.sp // self.tm),
                         self.bp + (i - self.n_p) // (self.ss // self.tm))

    def pos(self, i):
        return jnp.where(i < self.n_p, i % (self.sp // self.tm),
                         (i - self.n_p) % (self.ss // self.tm))


def _ada_kernel(c_ref, w_ref, b_ref, o_ref):
    c = c_ref[...]
    a = c / (1.0 + jnp.exp(-c))
    o_ref[...] = jnp.dot(a, w_ref[...], preferred_element_type=F32,
                         precision=lax.Precision.HIGHEST) + b_ref[...]


def _ada(c_all, w_ada, b_ada):
    nb, d = c_all.shape
    n = w_ada.shape[1]
    tn = 1024
    return pl.pallas_call(
        _ada_kernel,
        grid=(n // tn,),
        in_specs=[pl.BlockSpec((nb, d), lambda j: (0, 0)),
                  pl.BlockSpec((d, tn), lambda j: (0, j)),
                  pl.BlockSpec((1, tn), lambda j: (0, j))],
        out_specs=pl.BlockSpec((nb, tn), lambda j: (0, j)),
        out_shape=jax.ShapeDtypeStruct((nb, n), F32),
        compiler_params=pltpu.CompilerParams(dimension_semantics=("arbitrary",),
                                             vmem_limit_bytes=VMEM_LIMIT),
        name="ada",
    )(c_all, w_ada, b_ada)


def _rope_half(x, cos, sin):
    return x * cos + pltpu.roll(x, MLA_ROPE_DIM, 1) * sin


def _inproj_kernel(n_p, xp_ref, xs_ref, mod_ref, gpre_ref, win_ref, wkr_ref, gqa_ref, wq_ref, gkva_ref,
                   wkv_ref, cos_ref, sin_ref, dq_ref, dk_ref, dv_ref, qm_ref, km_ref, vm_ref):
    i = pl.program_id(0)
    x = jnp.where(i < n_p, xp_ref[...], xs_ref[...])
    shift = mod_ref[0, 0:1, :]
    scale = mod_ref[0, 1:2, :]
    hb = (_rms(x, gpre_ref[...]) * (1.0 + scale) + shift).astype(BF16)

    dq_ref[...] = (_dot(hb, win_ref[:, C_DQ:C_DQ + DIFF_WIDTH]) * (DIFF_HEAD_DIM ** -0.5 * LOG2E)).astype(BF16)
    dk_ref[...] = _dot(hb, win_ref[:, C_DK:C_DK + DIFF_WIDTH]).astype(BF16)
    dv_ref[...] = _dot(hb, win_ref[:, C_DV:C_DV + DIFF_WIDTH]).astype(BF16)

    cos = cos_ref[...]
    sin = sin_ref[...]
    cq = _dot(hb, win_ref[:, C_CQ:C_CQ + Q_LORA_RANK])
    ckv = _dot(hb, win_ref[:, C_CKV:C_CKV + KV_LORA_RANK])
    kr = _rope_half(_dot(hb, wkr_ref[...]), cos, sin).astype(BF16)

    q = _dot(_rms(cq, gqa_ref[...]).astype(BF16), wq_ref[...])
    qscale = (MLA_NOPE_DIM + MLA_ROPE_DIM) ** -0.5 * LOG2E
    for h in range(MLA_HEADS):
        c0 = h * MLA_QK_PAD
        qm_ref[:, c0:c0 + LANES] = (q[:, c0:c0 + LANES] * qscale).astype(BF16)
        qm_ref[:, c0 + LANES:c0 + 2 * LANES] = (
            _rope_half(q[:, c0 + LANES:c0 + 2 * LANES], cos, sin) * qscale).astype(BF16)

    kv = _dot(_rms(ckv, gkva_ref[...]).astype(BF16), wkv_ref[...])
    for h in range(MLA_HEADS):
        c0 = h * MLA_QK_PAD
        km_ref[:, c0:c0 + LANES] = kv[:, c0:c0 + LANES].astype(BF16)
        km_ref[:, c0 + LANES:c0 + 2 * LANES] = kr
        vm_ref[:, h * MLA_V_DIM:(h + 1) * MLA_V_DIM] = kv[:, c0 + LANES:c0 + 2 * LANES].astype(BF16)


def _inproj(rows, xp, xs, mod, g_pre, w_in, w_kr, g_q_a, w_q, g_kv_a, w_kv, cos_t, sin_t):
    tm, d = rows.tm, D_MODEL
    const = lambda i: (0, 0)
    row = lambda i: (i, 0)
    one = pl.Buffered(1)
    out_w = (DIFF_WIDTH, DIFF_WIDTH, DIFF_WIDTH, MLA_HEADS * MLA_QK_PAD, MLA_HEADS * MLA_QK_PAD, D_MLA_OUT)
    return pl.pallas_call(
        functools.partial(_inproj_kernel, rows.n_p),
        grid=(rows.n,),
        in_specs=[pl.BlockSpec((tm, d), lambda i: (rows.p_idx(i), 0)),
                  pl.BlockSpec((tm, d), lambda i: (rows.s_idx(i), 0)),
                  pl.BlockSpec((1, N_MOD, d), lambda i: (rows.seq(i), 0, 0)),
                  pl.BlockSpec((1, d), const),
                  pl.BlockSpec((d, D_IN_PROJ), const, pipeline_mode=one),
                  pl.BlockSpec((d, 2 * MLA_ROPE_DIM), const),
                  pl.BlockSpec((1, Q_LORA_RANK), const),
                  pl.BlockSpec((Q_LORA_RANK, MLA_HEADS * MLA_QK_PAD), const, pipeline_mode=one),
                  pl.BlockSpec((1, KV_LORA_RANK), const),
                  pl.BlockSpec((KV_LORA_RANK, MLA_HEADS * MLA_QK_PAD), const, pipeline_mode=one),
                  pl.BlockSpec((tm, LANES), lambda i: (rows.pos(i), 0)),
                  pl.BlockSpec((tm, LANES), lambda i: (rows.pos(i), 0))],
        out_specs=[pl.BlockSpec((tm, w), row) for w in out_w],
        out_shape=[jax.ShapeDtypeStruct((rows.t, w), BF16) for w in out_w],
        compiler_params=pltpu.CompilerParams(dimension_semantics=("arbitrary",),
                                             vmem_limit_bytes=VMEM_LIMIT),
        name="inproj",
    )(xp, xs, mod, g_pre, w_in, w_kr, g_q_a, w_q, g_kv_a, w_kv, cos_t, sin_t)


def _lane_groups(x, op):
    out = x[:, 0:LANES]
    for g in range(1, x.shape[1] // LANES):
        out = op(out, x[:, g * LANES:(g + 1) * LANES])
    return out


def _aligned(x, m):
    return x if isinstance(x, int) else pl.multiple_of(x, m)


def _chunk(ref, j, tk):
    return ref[pl.ds(_aligned(j * tk, tk), tk), :]


def _score_block(unit, s_view, jb, cpb, tk, mpart):
    score_fn, shift_fn = unit
    for c in range(cpb):
        j = jb * cpb + c
        s = score_fn(j)
        s_view[jb, :, c * tk:(c + 1) * tk] = s
        part = _lane_groups(s, jnp.maximum)
        if shift_fn is not None:
            part = part + shift_fn(j)
        mpart = jnp.maximum(mpart, part)
    return mpart


def _value_block(unit, m, v_ref, s_view, jb, cpb, tk, acc):
    _, shift_fn = unit
    ps = []
    for c in range(cpb):
        j = jb * cpb + c
        mj = m if shift_fn is None else m - shift_fn(j)
        ps.append(jnp.exp2(s_view[jb, :, c * tk:(c + 1) * tk] - mj).astype(BF16))
    p_all = ps[0] if cpb == 1 else jnp.concatenate(ps, axis=1)
    return acc + _dot(p_all, _chunk(v_ref, jb, cpb * tk))


def _attn_phases(geom, v_ref, score=(), value=()):
    tq, tk, cpb, nkb = geom
    zeros = jnp.zeros((tq, 2 * LANES), F32)
    neg_inf = jnp.full((tq, LANES), -jnp.inf, F32)

    def body(jb, carry):
        mparts, accs = carry
        mparts = tuple(_score_block(u, view, jb, cpb, tk, mp) for (u, view), mp in zip(score, mparts))
        accs = tuple(_value_block(u, m, v_ref, view, jb, cpb, tk, acc) for (u, m, view), acc in zip(value, accs))
        return mparts, accs

    init = ((neg_inf,) * len(score), (zeros,) * len(value))
    mparts, accs = lax.fori_loop(0, nkb, body, init, unroll=min(ATTN_UNROLL, nkb))
    return ([jnp.max(mp, axis=-1, keepdims=True) for mp in mparts],
            [acc[:, 0:LANES] / acc[:, LANES:LANES + 1] for acc in accs])


def _fill_value_rows(v_ref, vaug):
    lane = lax.broadcasted_iota(jnp.int32, v_ref.shape, 1)
    vaug[:, 0:LANES] = v_ref[...]
    vaug[:, LANES:2 * LANES] = jnp.where(lane == 0, 1.0, 0.0).astype(BF16)


def _attn_grouped(geom, v_ref, units, s_buf):
    views = [s_buf.at[k] for k in range(len(units))]
    ms, _ = _attn_phases(geom, v_ref, score=tuple(zip(units, views)))
    _, outs = _attn_phases(geom, v_ref, value=tuple(zip(units, ms, views)))
    return outs


def _diff_attn_kernel(lambda_init, geom, n_sub, slopes_ref, lam_ref, gsub_ref, q_ref, k_ref, v_ref,
                      o_ref, s_buf, bias_buf, vaug):
    tq, tk, _, _ = geom
    h = pl.program_id(1)
    qo = pl.program_id(2)
    slope = slopes_ref[h] * LOG2E
    n_cross = tk // tq

    @pl.when(qo == 0)
    def _():
        _fill_value_rows(v_ref, vaug)
        r_minus_c = (lax.broadcasted_iota(jnp.int32, (tq, tk), 0)
                     - lax.broadcasted_iota(jnp.int32, (tq, tk), 1)).astype(F32)
        bias_buf[0] = r_minus_c * (-slope)
        bias_buf[1] = r_minus_c * slope
        for c in range(n_cross):
            bias_buf[2 + c] = jnp.abs(r_minus_c + float(c * tq)) * (-slope)

    lp = lam_ref[...]
    lam = (jnp.exp(jnp.sum(lp[0:1] * lp[1:2], axis=-1, keepdims=True))
           - jnp.exp(jnp.sum(lp[2:3] * lp[3:4], axis=-1, keepdims=True)) + lambda_init)
    gsub = gsub_ref[...]

    def unit(sub, half):
        qbase = (qo * n_sub + sub) * tq
        jc = qbase // tk
        cross = 2 + (qbase - jc * tk) // tq
        q = q_ref[pl.ds(_aligned(sub * tq, tq), tq), :]
        lane = lax.broadcasted_iota(jnp.int32, q.shape, 1)
        keep = (lane < DIFF_HEAD_DIM) if half == 0 else (lane >= DIFF_HEAD_DIM)
        qh = jnp.where(keep, q, jnp.zeros_like(q))
        shift = lambda j: jnp.where(j == jc, 0.0, -slope * jnp.abs(qbase - j * tk).astype(F32))
        which = lambda j: jnp.where(j < jc, 0, jnp.where(j > jc, 1, cross))
        return (lambda j: _dot_nt(qh, _chunk(k_ref, j, tk)) + bias_buf[which(j)]), shift

    group = s_buf.shape[0] // 2

    def grouped(g, carry):
        subs = [g * group + k for k in range(group)]
        outs = _attn_grouped(geom, vaug, [unit(sub, half) for sub in subs for half in range(2)], s_buf)
        for k, sub in enumerate(subs):
            o = outs[2 * k] - lam * outs[2 * k + 1]
            o_ref[pl.ds(_aligned(sub * tq, tq), tq), :] = (_rms(o, gsub) * (1.0 - lambda_init)).astype(BF16)
        return carry

    lax.fori_loop(0, n_sub // group, grouped, 0)


def _attn_tiles(s, row0, units_per_sub):
    tq = min(ATTN_TQ, s)
    tk = min(ATTN_TK, s)
    tkb = min(ATTN_TKB, s)
    tq_outer = min(ATTN_TQ_OUTER, s)
    assert s % tq_outer == 0 and tq_outer % tq == 0 and s % tkb == 0 and tkb % tk == 0 and tk % tq == 0
    assert row0 % s == 0
    nkb, n_sub = s // tkb, tq_outer // tq
    units = max(units_per_sub, min(ATTN_GROUP, ATTN_SCORE_BYTES // (tq * s * 4)))
    group = min(units // units_per_sub, n_sub)
    assert n_sub % group == 0
    return (tq, tk, tkb // tk, nkb), tq_outer, (group * units_per_sub, nkb, tq, tkb)


def _diff_attn(dq, dk, dv, slopes, lam_p, g_sub, row0, b, s, lambda_init):
    geom, tq_outer, s_buf_shape = _attn_tiles(s, row0, 2)
    tq, tk = geom[0], geom[1]
    n_sub, nqo = tq_outer // tq, s // tq_outer
    dh = 2 * DIFF_HEAD_DIM
    return pl.pallas_call(
        functools.partial(_diff_attn_kernel, lambda_init, geom, n_sub),
        grid=(b, DIFF_HEADS, nqo),
        in_specs=[pl.BlockSpec(memory_space=pltpu.SMEM),
                  pl.BlockSpec((8, LANES), lambda bi, h, qi: (0, 0)),
                  pl.BlockSpec((1, dh), lambda bi, h, qi: (0, 0)),
                  pl.BlockSpec((tq_outer, dh), lambda bi, h, qi: (row0 // tq_outer + bi * nqo + qi, h)),
                  pl.BlockSpec((s, dh), lambda bi, h, qi: (row0 // s + bi, h)),
                  pl.BlockSpec((s, dh), lambda bi, h, qi: (row0 // s + bi, h))],
        out_specs=pl.BlockSpec((tq_outer, dh), lambda bi, h, qi: (bi * nqo + qi, h)),
        out_shape=jax.ShapeDtypeStruct((b * s, DIFF_WIDTH), BF16),
        scratch_shapes=[pltpu.VMEM(s_buf_shape, F32), pltpu.VMEM((2 + tk // tq, tq, tk), F32),
                        pltpu.VMEM((s, V_AUG), BF16)],
        compiler_params=pltpu.CompilerParams(
            dimension_semantics=("arbitrary", "arbitrary", "arbitrary"), vmem_limit_bytes=VMEM_LIMIT),
        name="diff_attn",
    )(slopes, lam_p, g_sub, dq, dk, dv)


def _mla_attn_kernel(geom, n_sub, q_ref, k_ref, v_ref, o_ref, s_buf, vaug):
    tq, tk, _, _ = geom

    @pl.when(pl.program_id(2) == 0)
    def _():
        _fill_value_rows(v_ref, vaug)

    def unit(sub):
        q = q_ref[pl.ds(_aligned(sub * tq, tq), tq), :]
        return (lambda j: _dot_nt(q, _chunk(k_ref, j, tk))), None

    group = s_buf.shape[0]

    def grouped(g, carry):
        subs = [g * group + k for k in range(group)]
        for sub, o in zip(subs, _attn_grouped(geom, vaug, [unit(sub) for sub in subs], s_buf)):
            o_ref[pl.ds(_aligned(sub * tq, tq), tq), :] = o.astype(BF16)
        return carry

    lax.fori_loop(0, n_sub // group, grouped, 0)


def _mla_attn(qm, km, vm, row0, b, s):
    geom, tq_outer, s_buf_shape = _attn_tiles(s, row0, 1)
    n_sub, nqo = tq_outer // geom[0], s // tq_outer
    return pl.pallas_call(
        functools.partial(_mla_attn_kernel, geom, n_sub),
        grid=(b, MLA_HEADS, nqo),
        in_specs=[pl.BlockSpec((tq_outer, MLA_QK_PAD), lambda bi, h, qi: (row0 // tq_outer + bi * nqo + qi, h)),
                  pl.BlockSpec((s, MLA_QK_PAD), lambda bi, h, qi: (row0 // s + bi, h)),
                  pl.BlockSpec((s, MLA_V_DIM), lambda bi, h, qi: (row0 // s + bi, h))],
        out_specs=pl.BlockSpec((tq_outer, MLA_V_DIM), lambda bi, h, qi: (bi * nqo + qi, h)),
        out_shape=jax.ShapeDtypeStruct((b * s, D_MLA_OUT), BF16),
        scratch_shapes=[pltpu.VMEM(s_buf_shape, F32), pltpu.VMEM((s, V_AUG), BF16)],
        compiler_params=pltpu.CompilerParams(
            dimension_semantics=("arbitrary", "arbitrary", "arbitrary"), vmem_limit_bytes=VMEM_LIMIT),
        name="mla_attn",
    )(qm, km, vm)


def _route(logits):
    lane = lax.broadcasted_iota(jnp.int32, logits.shape, 1)
    neg = jnp.full(logits.shape, -jnp.inf, F32)
    big = jnp.full(logits.shape, LANES, jnp.int32)
    first = lambda mask: jnp.min(jnp.where(mask, lane, big), axis=-1, keepdims=True)

    gl = jnp.where(lane < N_GROUPS, logits, neg)
    gmax = jnp.max(gl, axis=-1, keepdims=True)
    g_idx = first(gl == gmax)
    g_w = 1.0 / jnp.sum(jnp.exp(gl - gmax), axis=-1, keepdims=True)

    lo = ROUTER_LANE0 + EXPERTS_PER_GROUP * g_idx
    el = jnp.where(lane >= lo, jnp.where(lane < lo + EXPERTS_PER_GROUP, logits, neg), neg)
    v1 = jnp.max(el, axis=-1, keepdims=True)
    i1 = first(el == v1)
    el2 = jnp.where(lane == i1, neg, el)
    v2 = jnp.max(el2, axis=-1, keepdims=True)
    i2 = first(el2 == v2)
    t = jnp.exp(v2 - v1)
    w1 = g_w / (1.0 + t)
    w2 = w1 * t
    first_low = i1 < i2
    ea = jnp.minimum(i1, i2) - lo
    eb = jnp.maximum(i1, i2) - lo
    pair = jnp.where(ea == 0, 0, jnp.where(ea == 1, 3, 5)) + eb - ea - 1
    bucket = g_idx * N_PAIRS + pair
    return bucket, jnp.where(first_low, w1, w2), jnp.where(first_low, w2, w1)


def _oproj_kernel(n_p, oap_ref, oas_ref, obp_ref, obs_ref, xp_ref, xs_ref, mod_ref, wo_ref, gpost_ref,
                  gpre_ref, wr_ref, br_ref, xmid_ref, hx_ref, meta_ref, counts_ref, cnt):
    i = pl.program_id(0)

    @pl.when(i == 0)
    def _():
        cnt[...] = jnp.zeros(cnt.shape, F32)

    is_p = i < n_p
    oa = jnp.where(is_p, oap_ref[...], oas_ref[...])
    ob = jnp.where(is_p, obp_ref[...], obs_ref[...])
    x = jnp.where(is_p, xp_ref[...], xs_ref[...])
    mix = _dot(oa, wo_ref[0:DIFF_WIDTH, :]) + _dot(ob, wo_ref[DIFF_WIDTH:DIFF_WIDTH + D_MLA_OUT, :])
    x_mid = x + mod_ref[0, 2:3, :] * _rms(mix, gpost_ref[...])
    xmid_ref[...] = x_mid
    h2 = _rms(x_mid, gpre_ref[...]) * (1.0 + mod_ref[0, 4:5, :]) + mod_ref[0, 3:4, :]
    w_r = wr_ref[...]
    w_hi = w_r.astype(BF16)
    w_hi_lo = jnp.concatenate([w_hi, (w_r - w_hi.astype(F32)).astype(BF16)], axis=1)
    h_hi = h2.astype(BF16)
    h_lo = (h2 - h_hi.astype(F32)).astype(BF16)
    hi_terms = _dot(h_hi, w_hi_lo)
    logits = hi_terms[:, 0:LANES] + hi_terms[:, LANES:2 * LANES] + _dot(h_lo, w_hi) + br_ref[...]
    bucket, wa, wb = _route(logits)

    tm = logits.shape[0]
    lane = lax.broadcasted_iota(jnp.int32, logits.shape, 1)
    hx_ref[:, 0:D_MODEL] = h2
    hx_ref[:, D_MODEL:HX_WIDTH] = jnp.where(lane == 0, wa, jnp.where(lane == 1, wb, jnp.zeros_like(logits)))

    onehot = jnp.where(lane == bucket, 1.0, 0.0)
    earlier = (lax.broadcasted_iota(jnp.int32, (tm, tm), 0) > lax.broadcasted_iota(jnp.int32, (tm, tm), 1))
    before = _dot(jnp.where(earlier, 1.0, 0.0).astype(BF16), onehot.astype(BF16))
    rank = jnp.sum(onehot * (before + cnt[0:1, :]), axis=-1, keepdims=True)
    cnt[...] = cnt[...] + jnp.sum(onehot, axis=0, keepdims=True)
    meta_ref[...] = jnp.where(lane == 0, bucket, jnp.where(lane == 1, rank.astype(jnp.int32), 0))
    counts_ref[...] = cnt[...]


def _oproj(rows, oa_p, oa_s, ob_p, ob_s, xp, xs, mod, w_o, g_post, g_pre, w_r, b_r):
    tm, d = rows.tm, D_MODEL
    const = lambda i: (0, 0)
    row = lambda i: (i, 0)
    prow = lambda i: (rows.p_idx(i), 0)
    srow = lambda i: (rows.s_idx(i), 0)
    return pl.pallas_call(
        functools.partial(_oproj_kernel, rows.n_p),
        grid=(rows.n,),
        in_specs=[pl.BlockSpec((tm, DIFF_WIDTH), prow), pl.BlockSpec((tm, DIFF_WIDTH), srow),
                  pl.BlockSpec((tm, D_MLA_OUT), prow), pl.BlockSpec((tm, D_MLA_OUT), srow),
                  pl.BlockSpec((tm, d), prow), pl.BlockSpec((tm, d), srow),
                  pl.BlockSpec((1, N_MOD, d), lambda i: (rows.seq(i), 0, 0)),
                  pl.BlockSpec((DIFF_WIDTH + D_MLA_OUT, d), const, pipeline_mode=pl.Buffered(1)),
                  pl.BlockSpec((1, d), const), pl.BlockSpec((1, d), const),
                  pl.BlockSpec((d, LANES), const), pl.BlockSpec((1, LANES), const)],
        out_specs=[pl.BlockSpec((tm, d), row), pl.BlockSpec((tm, HX_WIDTH), row),
                   pl.BlockSpec((tm, LANES), row), pl.BlockSpec((8, LANES), const)],
        out_shape=[jax.ShapeDtypeStruct((rows.t, d), F32), jax.ShapeDtypeStruct((rows.t, HX_WIDTH), F32),
                   jax.ShapeDtypeStruct((rows.t, LANES), jnp.int32), jax.ShapeDtypeStruct((8, LANES), F32)],
        scratch_shapes=[pltpu.VMEM((8, LANES), F32)],
        compiler_params=pltpu.CompilerParams(dimension_semantics=("arbitrary",),
                                             vmem_limit_bytes=VMEM_LIMIT),
        name="oproj",
    )(oa_p, oa_s, ob_p, ob_s, xp, xs, mod, w_o, g_post, g_pre, w_r, b_r)


def _gather_rows(idx_ref, src_hbm, dst, sem, inline=False):
    def body(r, carry):
        pltpu.make_async_copy(src_hbm.at[pl.ds(idx_ref[0, 0, r], 1), :], dst.at[pl.ds(r, 1), :], sem).start()
        return carry

    if inline:
        for r in range(dst.shape[0]):
            body(r, 0)
    else:
        lax.fori_loop(0, dst.shape[0], body, 0, unroll=8)


def _wait_rows(src_hbm, dst, sem):
    pltpu.make_async_copy(src_hbm.at[pl.ds(0, dst.shape[0]), :], dst, sem).wait()


def _experts_kernel(ea_ref, eb_ref, nv_ref, inv_cur, inv_nxt, hx_hbm, wga, wua, wda, wgb, wub, wdb,
                    f_ref, xbuf, sem):
    i = pl.program_id(0)
    n_valid = nv_ref[0]
    slot = i % 2

    @pl.when(i == 0)
    def _():
        _gather_rows(inv_cur, hx_hbm, xbuf.at[0], sem.at[0])

    def compute():
        _wait_rows(hx_hbm, xbuf.at[slot], sem.at[slot])
        x = xbuf[slot, :, 0:D_MODEL].astype(BF16)
        aux = xbuf[slot, :, D_MODEL:HX_WIDTH]

        def hidden(wg, wu, w):
            g = _dot(x, wg[0])
            return ((g / (1.0 + jnp.exp(-g))) * _dot(x, wu[0]) * w).astype(BF16)

        f_ref[...] = (_dot(hidden(wga, wua, aux[:, 0:1]), wda[0])
                      + _dot(hidden(wgb, wub, aux[:, 1:2]), wdb[0]))

    @pl.when(i + 1 < n_valid)
    def _():
        _gather_rows(inv_nxt, hx_hbm, xbuf.at[1 - slot], sem.at[1 - slot], inline=True)
        compute()

    @pl.when(i + 1 == n_valid)
    def _():
        compute()

    @pl.when(i >= n_valid)
    def _():
        f_ref[...] = jnp.zeros(f_ref.shape, F32)


def _experts(hx, inv3, tile_ea, tile_eb, n_valid, w_gate, w_up, w_down):
    n_tiles, _, tm = inv3.shape
    d = D_MODEL
    wspec = lambda shape, which: pl.BlockSpec(
        (1,) + shape, (lambda i, ea, eb, nv: (ea[i], 0, 0)) if which == 0 else (lambda i, ea, eb, nv: (eb[i], 0, 0)))
    smem = lambda imap: pl.BlockSpec((1, 1, tm), imap, memory_space=pltpu.SMEM)
    return pl.pallas_call(
        _experts_kernel,
        grid_spec=pltpu.PrefetchScalarGridSpec(
            num_scalar_prefetch=3,
            grid=(n_tiles,),
            in_specs=[smem(lambda i, ea, eb, nv: (i, 0, 0)),
                      smem(lambda i, ea, eb, nv: (jnp.minimum(i + 1, n_tiles - 1), 0, 0)),
                      pl.BlockSpec(memory_space=pl.ANY),
                      wspec((d, D_EXPERT), 0), wspec((d, D_EXPERT), 0), wspec((D_EXPERT, d), 0),
                      wspec((d, D_EXPERT), 1), wspec((d, D_EXPERT), 1), wspec((D_EXPERT, d), 1)],
            out_specs=pl.BlockSpec((tm, d), lambda i, ea, eb, nv: (i, 0)),
            scratch_shapes=[pltpu.VMEM((2, tm, HX_WIDTH), F32), pltpu.SemaphoreType.DMA((2,))]),
        out_shape=jax.ShapeDtypeStruct((n_tiles * tm, d), F32),
        compiler_params=pltpu.CompilerParams(dimension_semantics=("arbitrary",),
                                             vmem_limit_bytes=VMEM_LIMIT),
        name="experts",
    )(tile_ea, tile_eb, n_valid, inv3, inv3, hx, w_gate, w_up, w_down, w_gate, w_up, w_down)


def _final_kernel(n_p, pos_cur, pos_nxt, f_hbm, xmid_ref, mod_ref, gpost_ref, yp_ref, ys_ref, fbuf, sem):
    i = pl.program_id(0)
    slot = i % 2

    @pl.when(i == 0)
    def _():
        _gather_rows(pos_cur, f_hbm, fbuf.at[0], sem.at[0])

    def compute():
        _wait_rows(f_hbm, fbuf.at[slot], sem.at[slot])
        y = xmid_ref[...] + mod_ref[0, 5:6, :] * _rms(fbuf[slot], gpost_ref[...])

        @pl.when(i < n_p)
        def _():
            yp_ref[...] = y

        @pl.when(i >= n_p)
        def _():
            ys_ref[...] = y

    @pl.when(i + 1 < pl.num_programs(0))
    def _():
        _gather_rows(pos_nxt, f_hbm, fbuf.at[1 - slot], sem.at[1 - slot], inline=True)
        compute()

    @pl.when(i + 1 == pl.num_programs(0))
    def _():
        compute()


def _final(rows, pos3, f_sorted, x_mid, mod, g_post):
    tm, d = rows.tm, D_MODEL
    smem = lambda imap: pl.BlockSpec((1, 1, tm), imap, memory_space=pltpu.SMEM)
    return pl.pallas_call(
        functools.partial(_final_kernel, rows.n_p),
        grid=(rows.n,),
        in_specs=[smem(lambda i: (i, 0, 0)),
                  smem(lambda i: (jnp.minimum(i + 1, rows.n - 1), 0, 0)),
                  pl.BlockSpec(memory_space=pl.ANY),
                  pl.BlockSpec((tm, d), lambda i: (i, 0)),
                  pl.BlockSpec((1, N_MOD, d), lambda i: (rows.seq(i), 0, 0)),
                  pl.BlockSpec((1, d), lambda i: (0, 0))],
        out_specs=[pl.BlockSpec((tm, d), lambda i: (rows.p_idx(i), 0)),
                   pl.BlockSpec((tm, d), lambda i: (rows.s_idx(i), 0))],
        out_shape=[jax.ShapeDtypeStruct((rows.tp, d), F32), jax.ShapeDtypeStruct((rows.ts, d), F32)],
        scratch_shapes=[pltpu.VMEM((2, tm, d), F32), pltpu.SemaphoreType.DMA((2,))],
        compiler_params=pltpu.CompilerParams(dimension_semantics=("arbitrary",),
                                             vmem_limit_bytes=VMEM_LIMIT),
        name="final",
    )(pos3, pos3, f_sorted, x_mid, mod, g_post)


def _moe_plan(meta, counts, tm):
    t = meta.shape[0]
    n_tiles = t // tm + N_BUCKETS
    cnt = counts[0, :N_BUCKETS].astype(jnp.int32)
    padded = (cnt + tm - 1) // tm * tm
    ends = jnp.cumsum(padded)
    buckets = jnp.arange(N_BUCKETS, dtype=jnp.int32)
    start_of = jnp.sum(jnp.where(meta[:, 0:1] == buckets[None, :], (ends - padded)[None, :], 0), axis=1)
    pos = start_of + meta[:, 1]
    inv = jnp.zeros((n_tiles * tm,), jnp.int32).at[pos].set(jnp.arange(t, dtype=jnp.int32))
    n_valid = ends[-1] // tm
    tile = jnp.minimum(jnp.arange(n_tiles, dtype=jnp.int32), n_valid - 1)
    bucket = jnp.sum((tile[:, None] * tm >= ends[None, :]).astype(jnp.int32), axis=1)
    group, pair = bucket // N_PAIRS, bucket % N_PAIRS
    ea = jnp.array([0, 0, 0, 1, 1, 2], jnp.int32)[pair]
    eb = jnp.array([1, 2, 3, 2, 3, 3], jnp.int32)[pair]
    base = group * EXPERTS_PER_GROUP
    return pos, inv, base + ea, base + eb, n_valid.reshape(1).astype(jnp.int32)


def _rotate_half_cols(w):
    half = MLA_ROPE_DIM // 2
    return jnp.concatenate([-w[..., half:], w[..., :half]], axis=-1)


def _rope_tables(s_max):
    half = MLA_ROPE_DIM // 2
    inv = ROPE_BASE ** (-jnp.arange(half, dtype=F32) / half)
    ang = jnp.arange(s_max, dtype=F32)[:, None] * inv[None, :]
    pad = jnp.zeros((s_max, LANES - MLA_ROPE_DIM), F32)
    cos, sin = jnp.cos(ang), jnp.sin(ang)
    return (jnp.concatenate([cos, cos, pad], axis=-1), jnp.concatenate([sin, sin, pad], axis=-1))


def _layer(x_prompt, x_sample, c_prompt, c_sample, layer_idx, w_ada, b_ada, g_pre_mix, g_post_mix, w_in,
           lambda_q1, lambda_k1, lambda_q2, lambda_k2, g_diff_sub, g_q_a, w_uq, g_kv_a, w_ukv, w_o,
           g_pre_ffn, g_post_ffn, w_router_group, b_router_group, w_router_expert, b_router_expert,
           w_gate, w_up, w_down, tm=256):
    bp, sp, d = x_prompt.shape
    bs, ss, _ = x_sample.shape
    rows = _Rows(bp, sp, bs, ss, tm)
    xp = x_prompt.reshape(bp * sp, d)
    xs = x_sample.reshape(bs * ss, d)
    lambda_init = 0.8 - 0.6 * math.exp(-0.3 * layer_idx)

    nb = bp + bs
    nb_pad = -(-nb // 8) * 8
    c_all = jnp.concatenate([c_prompt, c_sample, jnp.zeros((nb_pad - nb, d), F32)], axis=0)
    mod = _ada(c_all, w_ada, b_ada.reshape(1, -1)).reshape(nb_pad, N_MOD, d)

    w_kr = w_in[:, C_KR:C_KR + MLA_ROPE_DIM]
    w_kr = jnp.concatenate([w_kr, _rotate_half_cols(w_kr)], axis=1).astype(BF16)
    wq3 = w_uq.reshape(Q_LORA_RANK, MLA_HEADS, MLA_NOPE_DIM + MLA_ROPE_DIM)
    wq_rope = wq3[..., MLA_NOPE_DIM:]
    w_q = jnp.concatenate([wq3[..., :MLA_NOPE_DIM], wq_rope, _rotate_half_cols(wq_rope)], axis=-1)
    w_q = w_q.reshape(Q_LORA_RANK, MLA_HEADS * MLA_QK_PAD).astype(BF16)
    w_kv = w_ukv.astype(BF16)
    cos_t, sin_t = _rope_tables(max(sp, ss))

    dq, dk, dv, qm, km, vm = _inproj(rows, xp, xs, mod, g_pre_mix.reshape(1, d), w_in.astype(BF16), w_kr,
                                     g_q_a.reshape(1, -1), w_q, g_kv_a.reshape(1, -1), w_kv, cos_t, sin_t)

    slopes = jnp.array([2.0 ** (-8.0 * (i + 1) / DIFF_HEADS) for i in range(DIFF_HEADS)], dtype=F32)
    lam_p = jnp.zeros((8, LANES), F32).at[0:4, 0:DIFF_HEAD_DIM].set(
        jnp.stack([lambda_q1, lambda_k1, lambda_q2, lambda_k2]))
    g_sub = g_diff_sub.reshape(1, -1)
    oa_p = _diff_attn(dq, dk, dv, slopes, lam_p, g_sub, 0, bp, sp, lambda_init)
    oa_s = _diff_attn(dq, dk, dv, slopes, lam_p, g_sub, rows.tp, bs, ss, lambda_init)
    ob_p = _mla_attn(qm, km, vm, 0, bp, sp)
    ob_s = _mla_attn(qm, km, vm, rows.tp, bs, ss)

    w_r = jnp.concatenate([w_router_group, w_router_expert.reshape(d, N_EXPERTS),
                           jnp.zeros((d, LANES - N_GROUPS - N_EXPERTS), F32)], axis=1)
    b_r = jnp.concatenate([b_router_group, b_router_expert.reshape(N_EXPERTS),
                           jnp.zeros((LANES - N_GROUPS - N_EXPERTS,), F32)]).reshape(1, LANES)
    x_mid, hx, meta, counts = _oproj(rows, oa_p, oa_s, ob_p, ob_s, xp, xs, mod, w_o.astype(BF16),
                                     g_post_mix.reshape(1, d), g_pre_ffn.reshape(1, d), w_r, b_r)

    pos, inv, tile_ea, tile_eb, n_valid = _moe_plan(meta, counts, MOE_TM)
    f_sorted = _experts(hx, inv.reshape(-1, 1, MOE_TM), tile_ea, tile_eb, n_valid,
                        w_gate.astype(BF16), w_up.astype(BF16), w_down.astype(BF16))
    yp, ys = _final(rows, pos.reshape(rows.n, 1, tm), f_sorted, x_mid, mod, g_post_ffn.reshape(1, d))
    return yp.reshape(bp, sp, d), ys.reshape(bs, ss, d)


def kernel(x_prompt, x_sample, c_prompt, c_sample, w_ada, b_ada, g_pre_mix, g_post_mix, w_in, lambda_q1,
           lambda_k1, lambda_q2, lambda_k2, g_diff_sub, g_q_a, w_uq, g_kv_a, w_ukv, w_o, g_pre_ffn,
           g_post_ffn, w_router_group, b_router_group, w_router_expert, b_router_expert, w_gate, w_up,
           w_down):
    for l in range(w_ada.shape[0]):
        x_prompt, x_sample = _layer(
            x_prompt, x_sample, c_prompt, c_sample, l, w_ada[l], b_ada[l], g_pre_mix[l], g_post_mix[l],
            w_in[l], lambda_q1[l], lambda_k1[l], lambda_q2[l], lambda_k2[l], g_diff_sub[l], g_q_a[l],
            w_uq[l], g_kv_a[l], w_ukv[l], w_o[l], g_pre_ffn[l], g_post_ffn[l], w_router_group[l],
            b_router_group[l], w_router_expert[l], b_router_expert[l], w_gate[l], w_up[l], w_down[l])
    return x_prompt, x_sample
```

```python
import functools
import math

import jax
import jax.numpy as jnp
from jax import lax
from jax.experimental import pallas as pl
from jax.experimental.pallas import tpu as pltpu

F32 = jnp.float32
BF16 = jnp.bfloat16

D_MODEL = 2048
DIFF_HEADS = 8
DIFF_HEAD_DIM = 64
DIFF_WIDTH = DIFF_HEADS * 2 * DIFF_HEAD_DIM
MLA_HEADS = 8
MLA_NOPE_DIM = 128
MLA_ROPE_DIM = 64
MLA_V_DIM = 128
MLA_QK_PAD = 256
V_AUG = 256
Q_LORA_RANK = 512
KV_LORA_RANK = 256
ROPE_BASE = 10000.0
D_MLA_OUT = MLA_HEADS * MLA_V_DIM
N_GROUPS = 4
EXPERTS_PER_GROUP = 4
N_EXPERTS = N_GROUPS * EXPERTS_PER_GROUP
D_EXPERT = 512
N_MOD = 6
NORM_EPS = 1e-6
LANES = 128
ROUTER_LANE0 = N_GROUPS
N_PAIRS = 6
N_BUCKETS = N_GROUPS * N_PAIRS
HX_WIDTH = D_MODEL + LANES
MOE_TM = 256

C_DQ, C_DK, C_DV = 0, DIFF_WIDTH, 2 * DIFF_WIDTH
C_CQ = 3 * DIFF_WIDTH
C_CKV = C_CQ + Q_LORA_RANK
C_KR = C_CKV + KV_LORA_RANK
D_IN_PROJ = C_KR + MLA_ROPE_DIM

VMEM_LIMIT = 56 * 1024 * 1024
LOG2E = 1.4426950408889634
ATTN_TQ = 256
ATTN_TK = 512
ATTN_TQ_OUTER = 1024
ATTN_TKB = 2048
ATTN_UNROLL = 2
ATTN_GROUP = 4
ATTN_SCORE_BYTES = 32 * 1024 * 1024


def _rms(x, g):
    return x * lax.rsqrt(jnp.mean(x * x, axis=-1, keepdims=True) + NORM_EPS) * g


def _dot(a, b):
    return jnp.dot(a, b, preferred_element_type=F32)


def _dot_nt(a, b):
    return lax.dot_general(a, b, (((1,), (1,)), ((), ())), preferred_element_type=F32)


class _Rows:
    def __init__(self, bp, sp, bs, ss, tm):
        assert sp % tm == 0 and ss % tm == 0
        self.bp, self.sp, self.bs, self.ss, self.tm = bp, sp, bs, ss, tm
        self.n_p = bp * sp // tm
        self.n_s = bs * ss // tm
        self.n = self.n_p + self.n_s
        self.tp = bp * sp
        self.ts = bs * ss
        self.t = self.tp + self.ts

    def p_idx(self, i):
        return jnp.minimum(i, self.n_p - 1)

    def s_idx(self, i):
        return jnp.maximum(i - self.n_p, 0)

    def seq(self, i):
        return jnp.where(i < self.n_p, i // (self.sp // self.tm),
                         self.bp + (i - self.n_p) // (self.ss // self.tm))

    def pos(self, i):
        return jnp.where(i < self.n_p, i % (self.sp // self.tm),
                         (i - self.n_p) % (self.ss // self.tm))


def _ada_kernel(c_ref, w_ref, b_ref, o_ref):
    c = c_ref[...]
    a = c / (1.0 + jnp.exp(-c))
    o_ref[...] = jnp.dot(a, w_ref[...], preferred_element_type=F32,
                         precision=lax.Precision.HIGHEST) + b_ref[...]


def _ada(c_all, w_ada, b_ada):
    nb, d = c_all.shape
    n = w_ada.shape[1]
    tn = 2048
    return pl.pallas_call(
        _ada_kernel,
        grid=(n // tn,),
        in_specs=[pl.BlockSpec((nb, d), lambda j: (0, 0)),
                  pl.BlockSpec((d, tn), lambda j: (0, j)),
                  pl.BlockSpec((1, tn), lambda j: (0, j))],
        out_specs=pl.BlockSpec((nb, tn), lambda j: (0, j)),
        out_shape=jax.ShapeDtypeStruct((nb, n), F32),
        compiler_params=pltpu.CompilerParams(dimension_semantics=("arbitrary",),
                                             vmem_limit_bytes=VMEM_LIMIT),
        name="ada",
    )(c_all, w_ada, b_ada)


def _rope_half(x, cos, sin):
    return x * cos + pltpu.roll(x, MLA_ROPE_DIM, 1) * sin


def _inproj_kernel(n_p, xp_ref, xs_ref, mod_ref, gpre_ref, win_ref, wkr_ref, gqa_ref, wq_ref, gkva_ref,
                   wkv_ref, cos_ref, sin_ref, dq_ref, dk_ref, dv_ref, qm_ref, km_ref, vm_ref):
    i = pl.program_id(0)
    x = jnp.where(i < n_p, xp_ref[...], xs_ref[...])
    shift = mod_ref[0, 0:1, :]
    scale = mod_ref[0, 1:2, :]
    hb = (_rms(x, gpre_ref[...] * (1.0 + scale)) + shift).astype(BF16)

    dq_ref[...] = (_dot(hb, win_ref[:, C_DQ:C_DQ + DIFF_WIDTH]) * (DIFF_HEAD_DIM ** -0.5 * LOG2E)).astype(BF16)
    dk_ref[...] = _dot(hb, win_ref[:, C_DK:C_DK + DIFF_WIDTH]).astype(BF16)
    dv_ref[...] = _dot(hb, win_ref[:, C_DV:C_DV + DIFF_WIDTH]).astype(BF16)

    cos = cos_ref[...]
    sin = sin_ref[...]
    cq = _dot(hb, win_ref[:, C_CQ:C_CQ + Q_LORA_RANK])
    ckv = _dot(hb, win_ref[:, C_CKV:C_CKV + KV_LORA_RANK])
    kr = _rope_half(_dot(hb, wkr_ref[...]), cos, sin).astype(BF16)

    q = _dot(_rms(cq, gqa_ref[...]).astype(BF16), wq_ref[...])
    qscale = (MLA_NOPE_DIM + MLA_ROPE_DIM) ** -0.5 * LOG2E
    for h in range(MLA_HEADS):
        c0 = h * MLA_QK_PAD
        qm_ref[:, c0:c0 + LANES] = (q[:, c0:c0 + LANES] * qscale).astype(BF16)
        qm_ref[:, c0 + LANES:c0 + 2 * LANES] = (
            _rope_half(q[:, c0 + LANES:c0 + 2 * LANES], cos, sin) * qscale).astype(BF16)

    kv = _dot(_rms(ckv, gkva_ref[...]).astype(BF16), wkv_ref[...])
    for h in range(MLA_HEADS):
        c0 = h * MLA_QK_PAD
        km_ref[:, c0:c0 + LANES] = kv[:, c0:c0 + LANES].astype(BF16)
        km_ref[:, c0 + LANES:c0 + 2 * LANES] = kr
        vm_ref[:, h * MLA_V_DIM:(h + 1) * MLA_V_DIM] = kv[:, c0 + LANES:c0 + 2 * LANES].astype(BF16)


def _inproj(rows, xp, xs, mod, g_pre, w_in, w_kr, g_q_a, w_q, g_kv_a, w_kv, cos_t, sin_t):
    tm, d = rows.tm, D_MODEL
    const = lambda i: (0, 0)
    row = lambda i: (i, 0)
    one = pl.Buffered(1)
    out_w = (DIFF_WIDTH, DIFF_WIDTH, DIFF_WIDTH, MLA_HEADS * MLA_QK_PAD, MLA_HEADS * MLA_QK_PAD, D_MLA_OUT)
    return pl.pallas_call(
        functools.partial(_inproj_kernel, rows.n_p),
        grid=(rows.n,),
        in_specs=[pl.BlockSpec((tm, d), lambda i: (rows.p_idx(i), 0)),
                  pl.BlockSpec((tm, d), lambda i: (rows.s_idx(i), 0)),
                  pl.BlockSpec((1, N_MOD, d), lambda i: (rows.seq(i), 0, 0)),
                  pl.BlockSpec((1, d), const),
                  pl.BlockSpec((d, D_IN_PROJ), const, pipeline_mode=one),
                  pl.BlockSpec((d, 2 * MLA_ROPE_DIM), const),
                  pl.BlockSpec((1, Q_LORA_RANK), const),
                  pl.BlockSpec((Q_LORA_RANK, MLA_HEADS * MLA_QK_PAD), const, pipeline_mode=one),
                  pl.BlockSpec((1, KV_LORA_RANK), const),
                  pl.BlockSpec((KV_LORA_RANK, MLA_HEADS * MLA_QK_PAD), const, pipeline_mode=one),
                  pl.BlockSpec((tm, LANES), lambda i: (rows.pos(i), 0)),
                  pl.BlockSpec((tm, LANES), lambda i: (rows.pos(i), 0))],
        out_specs=[pl.BlockSpec((tm, w), row) for w in out_w],
        out_shape=[jax.ShapeDtypeStruct((rows.t, w), BF16) for w in out_w],
        compiler_params=pltpu.CompilerParams(dimension_semantics=("arbitrary",),
                                             vmem_limit_bytes=VMEM_LIMIT),
        name="inproj",
    )(xp, xs, mod, g_pre, w_in, w_kr, g_q_a, w_q, g_kv_a, w_kv, cos_t, sin_t)


def _lane_groups(x, op):
    out = x[:, 0:LANES]
    for g in range(1, x.shape[1] // LANES):
        out = op(out, x[:, g * LANES:(g + 1) * LANES])
    return out


def _aligned(x, m):
    return x if isinstance(x, int) else pl.multiple_of(x, m)


def _chunk(ref, j, tk):
    return ref[pl.ds(_aligned(j * tk, tk), tk), :]


def _score_block(unit, s_view, jb, cpb, tk, mpart):
    score_fn, shift_fn = unit
    for c in range(cpb):
        j = jb * cpb + c
        s = score_fn(j)
        s_view[jb, :, c * tk:(c + 1) * tk] = s
        part = _lane_groups(s, jnp.maximum)
        if shift_fn is not None:
            part = part + shift_fn(j)
        mpart = jnp.maximum(mpart, part)
    return mpart


def _value_block(unit, m, v_ref, s_view, jb, cpb, tk, acc):
    _, shift_fn = unit
    ps = []
    for c in range(cpb):
        j = jb * cpb + c
        mj = m if shift_fn is None else m - shift_fn(j)
        ps.append(jnp.exp2(s_view[jb, :, c * tk:(c + 1) * tk] - mj).astype(BF16))
    p_all = ps[0] if cpb == 1 else jnp.concatenate(ps, axis=1)
    return acc + _dot(p_all, _chunk(v_ref, jb, cpb * tk))


def _attn_phases(geom, v_ref, score=(), value=()):
    tq, tk, cpb, nkb = geom
    zeros = jnp.zeros((tq, 2 * LANES), F32)
    neg_inf = jnp.full((tq, LANES), -jnp.inf, F32)

    def body(jb, carry):
        mparts, accs = carry
        mparts = tuple(_score_block(u, view, jb, cpb, tk, mp) for (u, view), mp in zip(score, mparts))
        accs = tuple(_value_block(u, m, v_ref, view, jb, cpb, tk, acc) for (u, m, view), acc in zip(value, accs))
        return mparts, accs

    init = ((neg_inf,) * len(score), (zeros,) * len(value))
    mparts, accs = lax.fori_loop(0, nkb, body, init, unroll=min(ATTN_UNROLL, nkb))
    return ([jnp.max(mp, axis=-1, keepdims=True) for mp in mparts],
            [acc[:, 0:LANES] / acc[:, LANES:LANES + 1] for acc in accs])


def _fill_value_rows(v_ref, vaug):
    lane = lax.broadcasted_iota(jnp.int32, v_ref.shape, 1)
    vaug[:, 0:LANES] = v_ref[...]
    vaug[:, LANES:2 * LANES] = jnp.where(lane == 0, 1.0, 0.0).astype(BF16)


def _attn_grouped(geom, v_ref, units, s_buf):
    views = [s_buf.at[k] for k in range(len(units))]
    ms, _ = _attn_phases(geom, v_ref, score=tuple(zip(units, views)))
    _, outs = _attn_phases(geom, v_ref, value=tuple(zip(units, ms, views)))
    return outs


def _diff_attn_kernel(lambda_init, geom, n_sub, slopes_ref, lam_ref, gsub_ref, q_ref, k_ref, v_ref,
                      o_ref, s_buf, bias_buf, vaug):
    tq, tk, _, _ = geom
    h = pl.program_id(1)
    qo = pl.program_id(2)
    slope = slopes_ref[h] * LOG2E
    n_cross = tk // tq

    @pl.when(qo == 0)
    def _():
        _fill_value_rows(v_ref, vaug)
        r_minus_c = (lax.broadcasted_iota(jnp.int32, (tq, tk), 0)
                     - lax.broadcasted_iota(jnp.int32, (tq, tk), 1)).astype(F32)
        bias_buf[0] = r_minus_c * (-slope)
        bias_buf[1] = r_minus_c * slope
        for c in range(n_cross):
            bias_buf[2 + c] = jnp.abs(r_minus_c + float(c * tq)) * (-slope)

    lp = lam_ref[...]
    lam = (jnp.exp(jnp.sum(lp[0:1] * lp[1:2], axis=-1, keepdims=True))
           - jnp.exp(jnp.sum(lp[2:3] * lp[3:4], axis=-1, keepdims=True)) + lambda_init)
    gsub = gsub_ref[...] * (1.0 - lambda_init)

    def unit(sub, half):
        qbase = (qo * n_sub + sub) * tq
        jc = qbase // tk
        cross = 2 + (qbase - jc * tk) // tq
        q = q_ref[pl.ds(_aligned(sub * tq, tq), tq), :]
        lane = lax.broadcasted_iota(jnp.int32, q.shape, 1)
        keep = (lane < DIFF_HEAD_DIM) if half == 0 else (lane >= DIFF_HEAD_DIM)
        qh = jnp.where(keep, q, jnp.zeros_like(q))
        shift = lambda j: jnp.where(j == jc, 0.0, -slope * jnp.abs(qbase - j * tk).astype(F32))
        which = lambda j: jnp.where(j < jc, 0, jnp.where(j > jc, 1, cross))
        return (lambda j: _dot_nt(qh, _chunk(k_ref, j, tk)) + bias_buf[which(j)]), shift

    group = s_buf.shape[0] // 2

    def grouped(g, carry):
        subs = [g * group + k for k in range(group)]
        outs = _attn_grouped(geom, vaug, [unit(sub, half) for sub in subs for half in range(2)], s_buf)
        for k, sub in enumerate(subs):
            o = outs[2 * k] - lam * outs[2 * k + 1]
            o_ref[pl.ds(_aligned(sub * tq, tq), tq), :] = _rms(o, gsub).astype(BF16)
        return carry

    lax.fori_loop(0, n_sub // group, grouped, 0)


def _attn_tiles(s, row0, units_per_sub):
    tq = min(ATTN_TQ, s)
    tk = min(ATTN_TK, s)
    tkb = min(ATTN_TKB, s)
    tq_outer = min(ATTN_TQ_OUTER, s)
    assert s % tq_outer == 0 and tq_outer % tq == 0 and s % tkb == 0 and tkb % tk == 0 and tk % tq == 0
    assert row0 % s == 0
    nkb, n_sub = s // tkb, tq_outer // tq
    units = max(units_per_sub, min(ATTN_GROUP, ATTN_SCORE_BYTES // (tq * s * 4)))
    group = min(units // units_per_sub, n_sub)
    assert n_sub % group == 0
    return (tq, tk, tkb // tk, nkb), tq_outer, (group * units_per_sub, nkb, tq, tkb)


def _diff_attn(dq, dk, dv, slopes, lam_p, g_sub, row0, b, s, lambda_init):
    geom, tq_outer, s_buf_shape = _attn_tiles(s, row0, 2)
    tq, tk = geom[0], geom[1]
    n_sub, nqo = tq_outer // tq, s // tq_outer
    dh = 2 * DIFF_HEAD_DIM
    return pl.pallas_call(
        functools.partial(_diff_attn_kernel, lambda_init, geom, n_sub),
        grid=(b, DIFF_HEADS, nqo),
        in_specs=[pl.BlockSpec(memory_space=pltpu.SMEM),
                  pl.BlockSpec((8, LANES), lambda bi, h, qi: (0, 0)),
                  pl.BlockSpec((1, dh), lambda bi, h, qi: (0, 0)),
                  pl.BlockSpec((tq_outer, dh), lambda bi, h, qi: (row0 // tq_outer + bi * nqo + qi, h)),
                  pl.BlockSpec((s, dh), lambda bi, h, qi: (row0 // s + bi, h)),
                  pl.BlockSpec((s, dh), lambda bi, h, qi: (row0 // s + bi, h))],
        out_specs=pl.BlockSpec((tq_outer, dh), lambda bi, h, qi: (bi * nqo + qi, h)),
        out_shape=jax.ShapeDtypeStruct((b * s, DIFF_WIDTH), BF16),
        scratch_shapes=[pltpu.VMEM(s_buf_shape, F32), pltpu.VMEM((2 + tk // tq, tq, tk), F32),
                        pltpu.VMEM((s, V_AUG), BF16)],
        compiler_params=pltpu.CompilerParams(
            dimension_semantics=("arbitrary", "arbitrary", "arbitrary"), vmem_limit_bytes=VMEM_LIMIT),
        name="diff_attn",
    )(slopes, lam_p, g_sub, dq, dk, dv)


def _mla_attn_kernel(geom, n_sub, q_ref, k_ref, v_ref, o_ref, s_buf, vaug):
    tq, tk, _, _ = geom

    @pl.when(pl.program_id(2) == 0)
    def _():
        _fill_value_rows(v_ref, vaug)

    def unit(sub):
        q = q_ref[pl.ds(_aligned(sub * tq, tq), tq), :]
        return (lambda j: _dot_nt(q, _chunk(k_ref, j, tk))), None

    group = s_buf.shape[0]

    def grouped(g, carry):
        subs = [g * group + k for k in range(group)]
        for sub, o in zip(subs, _attn_grouped(geom, vaug, [unit(sub) for sub in subs], s_buf)):
            o_ref[pl.ds(_aligned(sub * tq, tq), tq), :] = o.astype(BF16)
        return carry

    lax.fori_loop(0, n_sub // group, grouped, 0)


def _mla_attn(qm, km, vm, row0, b, s):
    geom, tq_outer, s_buf_shape = _attn_tiles(s, row0, 1)
    n_sub, nqo = tq_outer // geom[0], s // tq_outer
    return pl.pallas_call(
        functools.partial(_mla_attn_kernel, geom, n_sub),
        grid=(b, MLA_HEADS, nqo),
        in_specs=[pl.BlockSpec((tq_outer, MLA_QK_PAD), lambda bi, h, qi: (row0 // tq_outer + bi * nqo + qi, h)),
                  pl.BlockSpec((s, MLA_QK_PAD), lambda bi, h, qi: (row0 // s + bi, h)),
                  pl.BlockSpec((s, MLA_V_DIM), lambda bi, h, qi: (row0 // s + bi, h))],
        out_specs=pl.BlockSpec((tq_outer, MLA_V_DIM), lambda bi, h, qi: (bi * nqo + qi, h)),
        out_shape=jax.ShapeDtypeStruct((b * s, D_MLA_OUT), BF16),
        scratch_shapes=[pltpu.VMEM(s_buf_shape, F32), pltpu.VMEM((s, V_AUG), BF16)],
        compiler_params=pltpu.CompilerParams(
            dimension_semantics=("arbitrary", "arbitrary", "arbitrary"), vmem_limit_bytes=VMEM_LIMIT),
        name="mla_attn",
    )(qm, km, vm)


def _route(logits):
    lane = lax.broadcasted_iota(jnp.int32, logits.shape, 1)
    neg = jnp.full(logits.shape, -jnp.inf, F32)
    big = jnp.full(logits.shape, LANES, jnp.int32)
    first = lambda mask: jnp.min(jnp.where(mask, lane, big), axis=-1, keepdims=True)

    gl = jnp.where(lane < N_GROUPS, logits, neg)
    gmax = jnp.max(gl, axis=-1, keepdims=True)
    g_idx = first(gl == gmax)
    g_w = 1.0 / jnp.sum(jnp.exp(gl - gmax), axis=-1, keepdims=True)

    lo = ROUTER_LANE0 + EXPERTS_PER_GROUP * g_idx
    el = jnp.where(lane >= lo, jnp.where(lane < lo + EXPERTS_PER_GROUP, logits, neg), neg)
    v1 = jnp.max(el, axis=-1, keepdims=True)
    i1 = first(el == v1)
    el2 = jnp.where(lane == i1, neg, el)
    v2 = jnp.max(el2, axis=-1, keepdims=True)
    i2 = first(el2 == v2)
    t = jnp.exp(v2 - v1)
    w1 = g_w / (1.0 + t)
    w2 = w1 * t
    first_low = i1 < i2
    ea = jnp.minimum(i1, i2) - lo
    eb = jnp.maximum(i1, i2) - lo
    pair = jnp.where(ea == 0, 0, jnp.where(ea == 1, 3, 5)) + eb - ea - 1
    bucket = g_idx * N_PAIRS + pair
    return bucket, jnp.where(first_low, w1, w2), jnp.where(first_low, w2, w1)


def _oproj_kernel(n_p, oap_ref, oas_ref, obp_ref, obs_ref, xp_ref, xs_ref, mod_ref, wo_ref, gpost_ref,
                  gpre_ref, wr_ref, br_ref, xmid_ref, hx_ref, meta_ref, counts_ref, cnt):
    i = pl.program_id(0)

    @pl.when(i == 0)
    def _():
        cnt[...] = jnp.zeros(cnt.shape, F32)

    is_p = i < n_p
    oa = jnp.where(is_p, oap_ref[...], oas_ref[...])
    ob = jnp.where(is_p, obp_ref[...], obs_ref[...])
    x = jnp.where(is_p, xp_ref[...], xs_ref[...])
    mix = _dot(oa, wo_ref[0:DIFF_WIDTH, :]) + _dot(ob, wo_ref[DIFF_WIDTH:DIFF_WIDTH + D_MLA_OUT, :])
    x_mid = x + _rms(mix, gpost_ref[...] * mod_ref[0, 2:3, :])
    xmid_ref[...] = x_mid
    h2 = _rms(x_mid, gpre_ref[...] * (1.0 + mod_ref[0, 4:5, :])) + mod_ref[0, 3:4, :]
    w_r = wr_ref[...]
    w_hi = w_r.astype(BF16)
    w_hi_lo = jnp.concatenate([w_hi, (w_r - w_hi.astype(F32)).astype(BF16)], axis=1)
    h_hi = h2.astype(BF16)
    h_lo = (h2 - h_hi.astype(F32)).astype(BF16)
    hi_terms = _dot(h_hi, w_hi_lo)
    logits = hi_terms[:, 0:LANES] + hi_terms[:, LANES:2 * LANES] + _dot(h_lo, w_hi) + br_ref[...]
    bucket, wa, wb = _route(logits)

    tm = logits.shape[0]
    lane = lax.broadcasted_iota(jnp.int32, logits.shape, 1)
    hx_ref[:, 0:D_MODEL] = h2
    hx_ref[:, D_MODEL:HX_WIDTH] = jnp.where(lane == 0, wa, jnp.where(lane == 1, wb, jnp.zeros_like(logits)))

    onehot = jnp.where(lane == bucket, 1.0, 0.0)
    earlier = (lax.broadcasted_iota(jnp.int32, (tm, tm), 0) > lax.broadcasted_iota(jnp.int32, (tm, tm), 1))
    before = _dot(jnp.where(earlier, 1.0, 0.0).astype(BF16), onehot.astype(BF16))
    rank = jnp.sum(onehot * (before + cnt[0:1, :]), axis=-1, keepdims=True)
    cnt[...] = cnt[...] + jnp.sum(onehot, axis=0, keepdims=True)
    meta_ref[...] = jnp.where(lane == 0, bucket, jnp.where(lane == 1, rank.astype(jnp.int32), 0))
    counts_ref[...] = cnt[...]


def _oproj(rows, oa_p, oa_s, ob_p, ob_s, xp, xs, mod, w_o, g_post, g_pre, w_r, b_r):
    tm, d = rows.tm, D_MODEL
    const = lambda i: (0, 0)
    row = lambda i: (i, 0)
    prow = lambda i: (rows.p_idx(i), 0)
    srow = lambda i: (rows.s_idx(i), 0)
    return pl.pallas_call(
        functools.partial(_oproj_kernel, rows.n_p),
        grid=(rows.n,),
        in_specs=[pl.BlockSpec((tm, DIFF_WIDTH), prow), pl.BlockSpec((tm, DIFF_WIDTH), srow),
                  pl.BlockSpec((tm, D_MLA_OUT), prow), pl.BlockSpec((tm, D_MLA_OUT), srow),
                  pl.BlockSpec((tm, d), prow), pl.BlockSpec((tm, d), srow),
                  pl.BlockSpec((1, N_MOD, d), lambda i: (rows.seq(i), 0, 0)),
                  pl.BlockSpec((DIFF_WIDTH + D_MLA_OUT, d), const, pipeline_mode=pl.Buffered(1)),
                  pl.BlockSpec((1, d), const), pl.BlockSpec((1, d), const),
                  pl.BlockSpec((d, LANES), const), pl.BlockSpec((1, LANES), const)],
        out_specs=[pl.BlockSpec((tm, d), row), pl.BlockSpec((tm, HX_WIDTH), row),
                   pl.BlockSpec((tm, LANES), row), pl.BlockSpec((8, LANES), const)],
        out_shape=[jax.ShapeDtypeStruct((rows.t, d), F32), jax.ShapeDtypeStruct((rows.t, HX_WIDTH), F32),
                   jax.ShapeDtypeStruct((rows.t, LANES), jnp.int32), jax.ShapeDtypeStruct((8, LANES), F32)],
        scratch_shapes=[pltpu.VMEM((8, LANES), F32)],
        compiler_params=pltpu.CompilerParams(dimension_semantics=("arbitrary",),
                                             vmem_limit_bytes=VMEM_LIMIT),
        name="oproj",
    )(oa_p, oa_s, ob_p, ob_s, xp, xs, mod, w_o, g_post, g_pre, w_r, b_r)


def _gather_rows(idx_ref, src_hbm, dst, sem, inline=False):
    def body(r, carry):
        pltpu.make_async_copy(src_hbm.at[pl.ds(idx_ref[0, 0, r], 1), :], dst.at[pl.ds(r, 1), :], sem).start()
        return carry

    if inline:
        for r in range(dst.shape[0]):
            body(r, 0)
    else:
        lax.fori_loop(0, dst.shape[0], body, 0, unroll=8)


def _wait_rows(src_hbm, dst, sem):
    pltpu.make_async_copy(src_hbm.at[pl.ds(0, dst.shape[0]), :], dst, sem).wait()


def _experts_kernel(ea_ref, eb_ref, nv_ref, inv_cur, inv_nxt, hx_hbm, wga, wua, wda, wgb, wub, wdb,
                    f_ref, xbuf, sem):
    i = pl.program_id(0)
    n_valid = nv_ref[0]
    slot = i % 2

    @pl.when(i == 0)
    def _():
        _gather_rows(inv_cur, hx_hbm, xbuf.at[0], sem.at[0])

    def compute():
        _wait_rows(hx_hbm, xbuf.at[slot], sem.at[slot])
        x = xbuf[slot, :, 0:D_MODEL].astype(BF16)
        aux = xbuf[slot, :, D_MODEL:HX_WIDTH]

        def hidden(wg, wu, w):
            g = _dot(x, wg[0])
            return ((g / (1.0 + jnp.exp(-g))) * _dot(x, wu[0]) * w).astype(BF16)

        f_ref[...] = (_dot(hidden(wga, wua, aux[:, 0:1]), wda[0])
                      + _dot(hidden(wgb, wub, aux[:, 1:2]), wdb[0]))

    @pl.when(i + 1 < n_valid)
    def _():
        _gather_rows(inv_nxt, hx_hbm, xbuf.at[1 - slot], sem.at[1 - slot], inline=True)
        compute()

    @pl.when(i + 1 == n_valid)
    def _():
        compute()

    @pl.when(i >= n_valid)
    def _():
        f_ref[...] = jnp.zeros(f_ref.shape, F32)


def _experts(hx, inv3, tile_ea, tile_eb, n_valid, w_gate, w_up, w_down):
    n_tiles, _, tm = inv3.shape
    d = D_MODEL
    wspec = lambda shape, which: pl.BlockSpec(
        (1,) + shape, (lambda i, ea, eb, nv: (ea[i], 0, 0)) if which == 0 else (lambda i, ea, eb, nv: (eb[i], 0, 0)))
    smem = lambda imap: pl.BlockSpec((1, 1, tm), imap, memory_space=pltpu.SMEM)
    return pl.pallas_call(
        _experts_kernel,
        grid_spec=pltpu.PrefetchScalarGridSpec(
            num_scalar_prefetch=3,
            grid=(n_tiles,),
            in_specs=[smem(lambda i, ea, eb, nv: (i, 0, 0)),
                      smem(lambda i, ea, eb, nv: (jnp.minimum(i + 1, n_tiles - 1), 0, 0)),
                      pl.BlockSpec(memory_space=pl.ANY),
                      wspec((d, D_EXPERT), 0), wspec((d, D_EXPERT), 0), wspec((D_EXPERT, d), 0),
                      wspec((d, D_EXPERT), 1), wspec((d, D_EXPERT), 1), wspec((D_EXPERT, d), 1)],
            out_specs=pl.BlockSpec((tm, d), lambda i, ea, eb, nv: (i, 0)),
            scratch_shapes=[pltpu.VMEM((2, tm, HX_WIDTH), F32), pltpu.SemaphoreType.DMA((2,))]),
        out_shape=jax.ShapeDtypeStruct((n_tiles * tm, d), F32),
        compiler_params=pltpu.CompilerParams(dimension_semantics=("arbitrary",),
                                             vmem_limit_bytes=VMEM_LIMIT),
        name="experts",
    )(tile_ea, tile_eb, n_valid, inv3, inv3, hx, w_gate, w_up, w_down, w_gate, w_up, w_down)


def _final_kernel(n_p, pos_cur, pos_nxt, f_hbm, xmid_ref, mod_ref, gpost_ref, yp_ref, ys_ref, fbuf, sem):
    i = pl.program_id(0)
    slot = i % 2

    @pl.when(i == 0)
    def _():
        _gather_rows(pos_cur, f_hbm, fbuf.at[0], sem.at[0])

    def compute():
        _wait_rows(f_hbm, fbuf.at[slot], sem.at[slot])
        y = xmid_ref[...] + _rms(fbuf[slot], gpost_ref[...] * mod_ref[0, 5:6, :])

        @pl.when(i < n_p)
        def _():
            yp_ref[...] = y

        @pl.when(i >= n_p)
        def _():
            ys_ref[...] = y

    @pl.when(i + 1 < pl.num_programs(0))
    def _():
        _gather_rows(pos_nxt, f_hbm, fbuf.at[1 - slot], sem.at[1 - slot], inline=True)
        compute()

    @pl.when(i + 1 == pl.num_programs(0))
    def _():
        compute()


def _final(rows, pos3, f_sorted, x_mid, mod, g_post):
    tm, d = rows.tm, D_MODEL
    smem = lambda imap: pl.BlockSpec((1, 1, tm), imap, memory_space=pltpu.SMEM)
    return pl.pallas_call(
        functools.partial(_final_kernel, rows.n_p),
        grid=(rows.n,),
        in_specs=[smem(lambda i: (i, 0, 0)),
                  smem(lambda i: (jnp.minimum(i + 1, rows.n - 1), 0, 0)),
                  pl.BlockSpec(memory_space=pl.ANY),
                  pl.BlockSpec((tm, d), lambda i: (i, 0)),
                  pl.BlockSpec((1, N_MOD, d), lambda i: (rows.seq(i), 0, 0)),
                  pl.BlockSpec((1, d), lambda i: (0, 0))],
        out_specs=[pl.BlockSpec((tm, d), lambda i: (rows.p_idx(i), 0)),
                   pl.BlockSpec((tm, d), lambda i: (rows.s_idx(i), 0))],
        out_shape=[jax.ShapeDtypeStruct((rows.tp, d), F32), jax.ShapeDtypeStruct((rows.ts, d), F32)],
        scratch_shapes=[pltpu.VMEM((2, tm, d), F32), pltpu.SemaphoreType.DMA((2,))],
        compiler_params=pltpu.CompilerParams(dimension_semantics=("arbitrary",),
                                             vmem_limit_bytes=VMEM_LIMIT),
        name="final",
    )(pos3, pos3, f_sorted, x_mid, mod, g_post)


def _moe_plan(meta, counts, tm):
    t = meta.shape[0]
    n_tiles = t // tm + N_BUCKETS
    cnt = counts[0, :N_BUCKETS].astype(jnp.int32)
    padded = (cnt + tm - 1) // tm * tm
    ends = jnp.cumsum(padded)
    buckets = jnp.arange(N_BUCKETS, dtype=jnp.int32)
    start_of = jnp.sum(jnp.where(meta[:, 0:1] == buckets[None, :], (ends - padded)[None, :], 0), axis=1)
    pos = start_of + meta[:, 1]
    inv = jnp.zeros((n_tiles * tm,), jnp.int32).at[pos].set(jnp.arange(t, dtype=jnp.int32))
    n_valid = ends[-1] // tm
    tile = jnp.minimum(jnp.arange(n_tiles, dtype=jnp.int32), n_valid - 1)
    bucket = jnp.sum((tile[:, None] * tm >= ends[None, :]).astype(jnp.int32), axis=1)
    group, pair = bucket // N_PAIRS, bucket % N_PAIRS
    ea = jnp.array([0, 0, 0, 1, 1, 2], jnp.int32)[pair]
    eb = jnp.array([1, 2, 3, 2, 3, 3], jnp.int32)[pair]
    base = group * EXPERTS_PER_GROUP
    return pos, inv, base + ea, base + eb, n_valid.reshape(1).astype(jnp.int32)


def _rotate_half_cols(w):
    half = MLA_ROPE_DIM // 2
    return jnp.concatenate([-w[..., half:], w[..., :half]], axis=-1)


def _rope_tables(s_max):
    half = MLA_ROPE_DIM // 2
    inv = ROPE_BASE ** (-jnp.arange(half, dtype=F32) / half)
    ang = jnp.arange(s_max, dtype=F32)[:, None] * inv[None, :]
    pad = jnp.zeros((s_max, LANES - MLA_ROPE_DIM), F32)
    cos, sin = jnp.cos(ang), jnp.sin(ang)
    return (jnp.concatenate([cos, cos, pad], axis=-1), jnp.concatenate([sin, sin, pad], axis=-1))


def _layer(x_prompt, x_sample, c_prompt, c_sample, layer_idx, w_ada, b_ada, g_pre_mix, g_post_mix, w_in,
           lambda_q1, lambda_k1, lambda_q2, lambda_k2, g_diff_sub, g_q_a, w_uq, g_kv_a, w_ukv, w_o,
           g_pre_ffn, g_post_ffn, w_router_group, b_router_group, w_router_expert, b_router_expert,
           w_gate, w_up, w_down, tm=256):
    bp, sp, d = x_prompt.shape
    bs, ss, _ = x_sample.shape
    rows = _Rows(bp, sp, bs, ss, tm)
    xp = x_prompt.reshape(bp * sp, d)
    xs = x_sample.reshape(bs * ss, d)
    lambda_init = 0.8 - 0.6 * math.exp(-0.3 * layer_idx)

    nb = bp + bs
    nb_pad = -(-nb // 8) * 8
    c_all = jnp.concatenate([c_prompt, c_sample, jnp.zeros((nb_pad - nb, d), F32)], axis=0)
    mod = _ada(c_all, w_ada, b_ada.reshape(1, -1)).reshape(nb_pad, N_MOD, d)

    w_kr = w_in[:, C_KR:C_KR + MLA_ROPE_DIM]
    w_kr = jnp.concatenate([w_kr, _rotate_half_cols(w_kr)], axis=1).astype(BF16)
    wq3 = w_uq.reshape(Q_LORA_RANK, MLA_HEADS, MLA_NOPE_DIM + MLA_ROPE_DIM)
    wq_rope = wq3[..., MLA_NOPE_DIM:]
    w_q = jnp.concatenate([wq3[..., :MLA_NOPE_DIM], wq_rope, _rotate_half_cols(wq_rope)], axis=-1)
    w_q = w_q.reshape(Q_LORA_RANK, MLA_HEADS * MLA_QK_PAD).astype(BF16)
    w_kv = w_ukv.astype(BF16)
    cos_t, sin_t = _rope_tables(max(sp, ss))

    dq, dk, dv, qm, km, vm = _inproj(rows, xp, xs, mod, g_pre_mix.reshape(1, d), w_in.astype(BF16), w_kr,
                                     g_q_a.reshape(1, -1), w_q, g_kv_a.reshape(1, -1), w_kv, cos_t, sin_t)

    slopes = jnp.array([2.0 ** (-8.0 * (i + 1) / DIFF_HEADS) for i in range(DIFF_HEADS)], dtype=F32)
    lam_p = jnp.zeros((8, LANES), F32).at[0:4, 0:DIFF_HEAD_DIM].set(
        jnp.stack([lambda_q1, lambda_k1, lambda_q2, lambda_k2]))
    g_sub = g_diff_sub.reshape(1, -1)
    oa_p = _diff_attn(dq, dk, dv, slopes, lam_p, g_sub, 0, bp, sp, lambda_init)
    oa_s = _diff_attn(dq, dk, dv, slopes, lam_p, g_sub, rows.tp, bs, ss, lambda_init)
    ob_p = _mla_attn(qm, km, vm, 0, bp, sp)
    ob_s = _mla_attn(qm, km, vm, rows.tp, bs, ss)

    w_r = jnp.concatenate([w_router_group, w_router_expert.reshape(d, N_EXPERTS),
                           jnp.zeros((d, LANES - N_GROUPS - N_EXPERTS), F32)], axis=1)
    b_r = jnp.concatenate([b_router_group, b_router_expert.reshape(N_EXPERTS),
                           jnp.zeros((LANES - N_GROUPS - N_EXPERTS,), F32)]).reshape(1, LANES)
    x_mid, hx, meta, counts = _oproj(rows, oa_p, oa_s, ob_p, ob_s, xp, xs, mod, w_o.astype(BF16),
                                     g_post_mix.reshape(1, d), g_pre_ffn.reshape(1, d), w_r, b_r)

    pos, inv, tile_ea, tile_eb, n_valid = _moe_plan(meta, counts, MOE_TM)
    f_sorted = _experts(hx, inv.reshape(-1, 1, MOE_TM), tile_ea, tile_eb, n_valid,
                        w_gate.astype(BF16), w_up.astype(BF16), w_down.astype(BF16))
    yp, ys = _final(rows, pos.reshape(rows.n, 1, tm), f_sorted, x_mid, mod, g_post_ffn.reshape(1, d))
    return yp.reshape(bp, sp, d), ys.reshape(bs, ss, d)


def kernel(x_prompt, x_sample, c_prompt, c_sample, w_ada, b_ada, g_pre_mix, g_post_mix, w_in, lambda_q1,
           lambda_k1, lambda_q2, lambda_k2, g_diff_sub, g_q_a, w_uq, g_kv_a, w_ukv, w_o, g_pre_ffn,
           g_post_ffn, w_router_group, b_router_group, w_router_expert, b_router_expert, w_gate, w_up,
           w_down):
    for l in range(w_ada.shape[0]):
        x_prompt, x_sample = _layer(
            x_prompt, x_sample, c_prompt, c_sample, l, w_ada[l], b_ada[l], g_pre_mix[l], g_post_mix[l],
            w_in[l], lambda_q1[l], lambda_k1[l], lambda_q2[l], lambda_k2[l], g_diff_sub[l], g_q_a[l],
            w_uq[l], g_kv_a[l], w_ukv[l], w_o[l], g_pre_ffn[l], g_post_ffn[l], w_router_group[l],
            b_router_group[l], w_router_expert[l], b_router_expert[l], w_gate[l], w_up[l], w_down[l])
    return x_prompt, x_sample
```

```python
import functools
import math

import jax
import jax.numpy as jnp
from jax import lax
from jax.experimental import pallas as pl
from jax.experimental.pallas import tpu as pltpu

F32 = jnp.float32
BF16 = jnp.bfloat16

D_MODEL = 2048
DIFF_HEADS = 8
DIFF_HEAD_DIM = 64
DIFF_WIDTH = DIFF_HEADS * 2 * DIFF_HEAD_DIM
MLA_HEADS = 8
MLA_NOPE_DIM = 128
MLA_ROPE_DIM = 64
MLA_V_DIM = 128
MLA_QK_PAD = 256
V_AUG = 256
Q_LORA_RANK = 512
KV_LORA_RANK = 256
ROPE_BASE = 10000.0
D_MLA_OUT = MLA_HEADS * MLA_V_DIM
N_GROUPS = 4
EXPERTS_PER_GROUP = 4
N_EXPERTS = N_GROUPS * EXPERTS_PER_GROUP
D_EXPERT = 512
N_MOD = 6
NORM_EPS = 1e-6
LANES = 128
ROUTER_LANE0 = N_GROUPS
N_PAIRS = 6
N_BUCKETS = N_GROUPS * N_PAIRS
HX_WIDTH = D_MODEL + LANES
MOE_TM = 256

C_DQ, C_DK, C_DV = 0, DIFF_WIDTH, 2 * DIFF_WIDTH
C_CQ = 3 * DIFF_WIDTH
C_CKV = C_CQ + Q_LORA_RANK
C_KR = C_CKV + KV_LORA_RANK
D_IN_PROJ = C_KR + MLA_ROPE_DIM

VMEM_LIMIT = 56 * 1024 * 1024
LOG2E = 1.4426950408889634
ATTN_TQ = 256
ATTN_TK = 512
ATTN_TQ_OUTER = 1024
ATTN_TKB = 2048
ATTN_UNROLL = 2
ATTN_GROUP = 4
ATTN_SCORE_BYTES = 32 * 1024 * 1024


def _rms(x, g):
    return x * lax.rsqrt(jnp.mean(x * x, axis=-1, keepdims=True) + NORM_EPS) * g


def _dot(a, b):
    return jnp.dot(a, b, preferred_element_type=F32)


def _dot_nt(a, b):
    return lax.dot_general(a, b, (((1,), (1,)), ((), ())), preferred_element_type=F32)


class _Rows:
    def __init__(self, bp, sp, bs, ss, tm):
        assert sp % tm == 0 and ss % tm == 0
        self.bp, self.sp, self.bs, self.ss, self.tm = bp, sp, bs, ss, tm
        self.n_p = bp * sp // tm
        self.n_s = bs * ss // tm
        self.n = self.n_p + self.n_s
        self.tp = bp * sp
        self.ts = bs * ss
        self.t = self.tp + self.ts

    def p_idx(self, i):
        return jnp.minimum(i, self.n_p - 1)

    def s_idx(self, i):
        return jnp.maximum(i - self.n_p, 0)

    def seq(self, i):
        return jnp.where(i < self.n_p, i // (self.sp // self.tm),
                         self.bp + (i - self.n_p) // (self.ss // self.tm))

    def pos(self, i):
        return jnp.where(i < self.n_p, i % (self.sp // self.tm),
                         (i - self.n_p) % (self.ss // self.tm))


def _ada_kernel(c_ref, w_ref, b_ref, o_ref):
    c = c_ref[...]
    a = c / (1.0 + jnp.exp(-c))
    o_ref[...] = jnp.dot(a, w_ref[...], preferred_element_type=F32,
                         precision=lax.Precision.HIGHEST) + b_ref[...]


def _ada(c_all, w_ada, b_ada):
    nb, d = c_all.shape
    n = w_ada.shape[1]
    tn = 2048
    return pl.pallas_call(
        _ada_kernel,
        grid=(n // tn,),
        in_specs=[pl.BlockSpec((nb, d), lambda j: (0, 0)),
                  pl.BlockSpec((d, tn), lambda j: (0, j)),
                  pl.BlockSpec((1, tn), lambda j: (0, j))],
        out_specs=pl.BlockSpec((nb, tn), lambda j: (0, j)),
        out_shape=jax.ShapeDtypeStruct((nb, n), F32),
        compiler_params=pltpu.CompilerParams(dimension_semantics=("arbitrary",),
                                             vmem_limit_bytes=VMEM_LIMIT),
        name="ada",
    )(c_all, w_ada, b_ada)


def _rope_half(x, cos, sin):
    return x * cos + pltpu.roll(x, MLA_ROPE_DIM, 1) * sin


def _inproj_kernel(n_p, xp_ref, xs_ref, mod_ref, gpre_ref, win_ref, wkr_ref, gqa_ref, wq_ref, gkva_ref,
                   wkv_ref, cos_ref, sin_ref, dq_ref, dk_ref, dv_ref, qm_ref, km_ref, vm_ref):
    i = pl.program_id(0)
    x = jnp.where(i < n_p, xp_ref[...], xs_ref[...])
    shift = mod_ref[0, 0:1, :]
    scale = mod_ref[0, 1:2, :]
    hb = (_rms(x, gpre_ref[...] * (1.0 + scale)) + shift).astype(BF16)

    dq_ref[...] = (_dot(hb, win_ref[:, C_DQ:C_DQ + DIFF_WIDTH]) * (DIFF_HEAD_DIM ** -0.5 * LOG2E)).astype(BF16)
    dk_ref[...] = _dot(hb, win_ref[:, C_DK:C_DK + DIFF_WIDTH]).astype(BF16)
    dv_ref[...] = _dot(hb, win_ref[:, C_DV:C_DV + DIFF_WIDTH]).astype(BF16)

    cos = cos_ref[...]
    sin = sin_ref[...]
    cq = _dot(hb, win_ref[:, C_CQ:C_CQ + Q_LORA_RANK])
    ckv = _dot(hb, win_ref[:, C_CKV:C_CKV + KV_LORA_RANK])
    kr = _rope_half(_dot(hb, wkr_ref[...]), cos, sin).astype(BF16)

    q = _dot(_rms(cq, gqa_ref[...]).astype(BF16), wq_ref[...])
    qscale = (MLA_NOPE_DIM + MLA_ROPE_DIM) ** -0.5 * LOG2E
    for h in range(MLA_HEADS):
        c0 = h * MLA_QK_PAD
        qm_ref[:, c0:c0 + LANES] = (q[:, c0:c0 + LANES] * qscale).astype(BF16)
        qm_ref[:, c0 + LANES:c0 + 2 * LANES] = (
            _rope_half(q[:, c0 + LANES:c0 + 2 * LANES], cos, sin) * qscale).astype(BF16)

    kv = _dot(_rms(ckv, gkva_ref[...]).astype(BF16), wkv_ref[...])
    for h in range(MLA_HEADS):
        c0 = h * MLA_QK_PAD
        km_ref[:, c0:c0 + LANES] = kv[:, c0:c0 + LANES].astype(BF16)
        km_ref[:, c0 + LANES:c0 + 2 * LANES] = kr
        vm_ref[:, h * MLA_V_DIM:(h + 1) * MLA_V_DIM] = kv[:, c0 + LANES:c0 + 2 * LANES].astype(BF16)


def _inproj(rows, xp, xs, mod, g_pre, w_in, w_kr, g_q_a, w_q, g_kv_a, w_kv, cos_t, sin_t):
    tm, d = rows.tm, D_MODEL
    const = lambda i: (0, 0)
    row = lambda i: (i, 0)
    one = pl.Buffered(1)
    out_w = (DIFF_WIDTH, DIFF_WIDTH, DIFF_WIDTH, MLA_HEADS * MLA_QK_PAD, MLA_HEADS * MLA_QK_PAD, D_MLA_OUT)
    return pl.pallas_call(
        functools.partial(_inproj_kernel, rows.n_p),
        grid=(rows.n,),
        in_specs=[pl.BlockSpec((tm, d), lambda i: (rows.p_idx(i), 0)),
                  pl.BlockSpec((tm, d), lambda i: (rows.s_idx(i), 0)),
                  pl.BlockSpec((1, N_MOD, d), lambda i: (rows.seq(i), 0, 0)),
                  pl.BlockSpec((1, d), const),
                  pl.BlockSpec((d, D_IN_PROJ), const, pipeline_mode=one),
                  pl.BlockSpec((d, 2 * MLA_ROPE_DIM), const),
                  pl.BlockSpec((1, Q_LORA_RANK), const),
                  pl.BlockSpec((Q_LORA_RANK, MLA_HEADS * MLA_QK_PAD), const, pipeline_mode=one),
                  pl.BlockSpec((1, KV_LORA_RANK), const),
                  pl.BlockSpec((KV_LORA_RANK, MLA_HEADS * MLA_QK_PAD), const, pipeline_mode=one),
                  pl.BlockSpec((tm, LANES), lambda i: (rows.pos(i), 0)),
                  pl.BlockSpec((tm, LANES), lambda i: (rows.pos(i), 0))],
        out_specs=[pl.BlockSpec((tm, w), row) for w in out_w],
        out_shape=[jax.ShapeDtypeStruct((rows.t, w), BF16) for w in out_w],
        compiler_params=pltpu.CompilerParams(dimension_semantics=("arbitrary",),
                                             vmem_limit_bytes=VMEM_LIMIT),
        name="inproj",
    )(xp, xs, mod, g_pre, w_in, w_kr, g_q_a, w_q, g_kv_a, w_kv, cos_t, sin_t)


def _lane_groups(x, op):
    out = x[:, 0:LANES]
    for g in range(1, x.shape[1] // LANES):
        out = op(out, x[:, g * LANES:(g + 1) * LANES])
    return out


def _aligned(x, m):
    return x if isinstance(x, int) else pl.multiple_of(x, m)


def _chunk(ref, j, tk):
    return ref[pl.ds(_aligned(j * tk, tk), tk), :]


def _score_block(unit, s_view, jb, cpb, tk, mpart):
    score_fn, shift_fn = unit
    for c in range(cpb):
        j = jb * cpb + c
        s = score_fn(j)
        s_view[jb, :, c * tk:(c + 1) * tk] = s
        part = _lane_groups(s, jnp.maximum)
        if shift_fn is not None:
            part = part + shift_fn(j)
        mpart = jnp.maximum(mpart, part)
    return mpart


def _value_block(unit, m, v_ref, s_view, jb, cpb, tk, acc):
    _, shift_fn = unit
    ps = []
    for c in range(cpb):
        j = jb * cpb + c
        mj = m if shift_fn is None else m - shift_fn(j)
        ps.append(jnp.exp2(s_view[jb, :, c * tk:(c + 1) * tk] - mj).astype(BF16))
    p_all = ps[0] if cpb == 1 else jnp.concatenate(ps, axis=1)
    return acc + _dot(p_all, _chunk(v_ref, jb, cpb * tk))


def _attn_phases(geom, v_ref, score=(), value=()):
    tq, tk, cpb, nkb = geom
    zeros = jnp.zeros((tq, 2 * LANES), F32)
    neg_inf = jnp.full((tq, LANES), -jnp.inf, F32)

    def body(jb, carry):
        mparts, accs = carry
        mparts = tuple(_score_block(u, view, jb, cpb, tk, mp) for (u, view), mp in zip(score, mparts))
        accs = tuple(_value_block(u, m, v_ref, view, jb, cpb, tk, acc) for (u, m, view), acc in zip(value, accs))
        return mparts, accs

    init = ((neg_inf,) * len(score), (zeros,) * len(value))
    mparts, accs = lax.fori_loop(0, nkb, body, init, unroll=min(ATTN_UNROLL, nkb))
    return ([jnp.max(mp, axis=-1, keepdims=True) for mp in mparts],
            [acc[:, 0:LANES] / acc[:, LANES:LANES + 1] for acc in accs])


def _fill_value_rows(v_ref, vaug):
    lane = lax.broadcasted_iota(jnp.int32, v_ref.shape, 1)
    vaug[:, 0:LANES] = v_ref[...]
    vaug[:, LANES:2 * LANES] = jnp.where(lane == 0, 1.0, 0.0).astype(BF16)


def _attn_grouped(geom, v_ref, units, s_buf):
    views = [s_buf.at[k] for k in range(len(units))]
    ms, _ = _attn_phases(geom, v_ref, score=tuple(zip(units, views)))
    _, outs = _attn_phases(geom, v_ref, value=tuple(zip(units, ms, views)))
    return outs


def _diff_attn_kernel(lambda_init, geom, n_sub, slopes_ref, lam_ref, gsub_ref, q_ref, k_ref, v_ref,
                      o_ref, s_buf, bias_buf, vaug):
    tq, tk, _, _ = geom
    h = pl.program_id(1)
    qo = pl.program_id(2)
    slope = slopes_ref[h] * LOG2E
    n_cross = tk // tq

    @pl.when(qo == 0)
    def _():
        _fill_value_rows(v_ref, vaug)
        r_minus_c = (lax.broadcasted_iota(jnp.int32, (tq, tk), 0)
                     - lax.broadcasted_iota(jnp.int32, (tq, tk), 1)).astype(F32)
        bias_buf[0] = r_minus_c * (-slope)
        bias_buf[1] = r_minus_c * slope
        for c in range(n_cross):
            bias_buf[2 + c] = jnp.abs(r_minus_c + float(c * tq)) * (-slope)

    lp = lam_ref[...]
    lam = (jnp.exp(jnp.sum(lp[0:1] * lp[1:2], axis=-1, keepdims=True))
           - jnp.exp(jnp.sum(lp[2:3] * lp[3:4], axis=-1, keepdims=True)) + lambda_init)
    gsub = gsub_ref[...] * (1.0 - lambda_init)

    def unit(sub, half):
        qbase = (qo * n_sub + sub) * tq
        jc = qbase // tk
        cross = 2 + (qbase - jc * tk) // tq
        q = q_ref[pl.ds(_aligned(sub * tq, tq), tq), :]
        lane = lax.broadcasted_iota(jnp.int32, q.shape, 1)
        keep = (lane < DIFF_HEAD_DIM) if half == 0 else (lane >= DIFF_HEAD_DIM)
        qh = jnp.where(keep, q, jnp.zeros_like(q))
        shift = lambda j: jnp.where(j == jc, 0.0, -slope * jnp.abs(qbase - j * tk).astype(F32))
        which = lambda j: jnp.where(j < jc, 0, jnp.where(j > jc, 1, cross))
        return (lambda j: _dot_nt(qh, _chunk(k_ref, j, tk)) + bias_buf[which(j)]), shift

    group = s_buf.shape[0] // 2

    def grouped(g, carry):
        subs = [g * group + k for k in range(group)]
        outs = _attn_grouped(geom, vaug, [unit(sub, half) for sub in subs for half in range(2)], s_buf)
        for k, sub in enumerate(subs):
            o = outs[2 * k] - lam * outs[2 * k + 1]
            o_ref[pl.ds(_aligned(sub * tq, tq), tq), :] = _rms(o, gsub).astype(BF16)
        return carry

    lax.fori_loop(0, n_sub // group, grouped, 0)


def _attn_tiles(s, row0, units_per_sub):
    tq = min(ATTN_TQ, s)
    tk = min(ATTN_TK, s)
    tkb = min(ATTN_TKB, s)
    tq_outer = min(ATTN_TQ_OUTER, s)
    assert s % tq_outer == 0 and tq_outer % tq == 0 and s % tkb == 0 and tkb % tk == 0 and tk % tq == 0
    assert row0 % s == 0
    nkb, n_sub = s // tkb, tq_outer // tq
    units = max(units_per_sub, min(ATTN_GROUP, ATTN_SCORE_BYTES // (tq * s * 4)))
    group = min(units // units_per_sub, n_sub)
    assert n_sub % group == 0
    return (tq, tk, tkb // tk, nkb), tq_outer, (group * units_per_sub, nkb, tq, tkb)


def _diff_attn(dq, dk, dv, slopes, lam_p, g_sub, row0, b, s, lambda_init):
    geom, tq_outer, s_buf_shape = _attn_tiles(s, row0, 2)
    tq, tk = geom[0], geom[1]
    n_sub, nqo = tq_outer // tq, s // tq_outer
    dh = 2 * DIFF_HEAD_DIM
    return pl.pallas_call(
        functools.partial(_diff_attn_kernel, lambda_init, geom, n_sub),
        grid=(b, DIFF_HEADS, nqo),
        in_specs=[pl.BlockSpec(memory_space=pltpu.SMEM),
                  pl.BlockSpec((8, LANES), lambda bi, h, qi: (0, 0)),
                  pl.BlockSpec((1, dh), lambda bi, h, qi: (0, 0)),
                  pl.BlockSpec((tq_outer, dh), lambda bi, h, qi: (row0 // tq_outer + bi * nqo + qi, h)),
                  pl.BlockSpec((s, dh), lambda bi, h, qi: (row0 // s + bi, h)),
                  pl.BlockSpec((s, dh), lambda bi, h, qi: (row0 // s + bi, h))],
        out_specs=pl.BlockSpec((tq_outer, dh), lambda bi, h, qi: (bi * nqo + qi, h)),
        out_shape=jax.ShapeDtypeStruct((b * s, DIFF_WIDTH), BF16),
        scratch_shapes=[pltpu.VMEM(s_buf_shape, F32), pltpu.VMEM((2 + tk // tq, tq, tk), F32),
                        pltpu.VMEM((s, V_AUG), BF16)],
        compiler_params=pltpu.CompilerParams(
            dimension_semantics=("arbitrary", "arbitrary", "arbitrary"), vmem_limit_bytes=VMEM_LIMIT),
        name="diff_attn",
    )(slopes, lam_p, g_sub, dq, dk, dv)


def _mla_attn_kernel(geom, n_sub, q_ref, k_ref, v_ref, o_ref, s_buf, vaug):
    tq, tk, _, _ = geom

    @pl.when(pl.program_id(2) == 0)
    def _():
        _fill_value_rows(v_ref, vaug)

    def unit(sub):
        q = q_ref[pl.ds(_aligned(sub * tq, tq), tq), :]
        return (lambda j: _dot_nt(q, _chunk(k_ref, j, tk))), None

    group = s_buf.shape[0]

    def grouped(g, carry):
        subs = [g * group + k for k in range(group)]
        for sub, o in zip(subs, _attn_grouped(geom, vaug, [unit(sub) for sub in subs], s_buf)):
            o_ref[pl.ds(_aligned(sub * tq, tq), tq), :] = o.astype(BF16)
        return carry

    lax.fori_loop(0, n_sub // group, grouped, 0)


def _mla_attn(qm, km, vm, row0, b, s):
    geom, tq_outer, s_buf_shape = _attn_tiles(s, row0, 1)
    n_sub, nqo = tq_outer // geom[0], s // tq_outer
    return pl.pallas_call(
        functools.partial(_mla_attn_kernel, geom, n_sub),
        grid=(b, MLA_HEADS, nqo),
        in_specs=[pl.BlockSpec((tq_outer, MLA_QK_PAD), lambda bi, h, qi: (row0 // tq_outer + bi * nqo + qi, h)),
                  pl.BlockSpec((s, MLA_QK_PAD), lambda bi, h, qi: (row0 // s + bi, h)),
                  pl.BlockSpec((s, MLA_V_DIM), lambda bi, h, qi: (row0 // s + bi, h))],
        out_specs=pl.BlockSpec((tq_outer, MLA_V_DIM), lambda bi, h, qi: (bi * nqo + qi, h)),
        out_shape=jax.ShapeDtypeStruct((b * s, D_MLA_OUT), BF16),
        scratch_shapes=[pltpu.VMEM(s_buf_shape, F32), pltpu.VMEM((s, V_AUG), BF16)],
        compiler_params=pltpu.CompilerParams(
            dimension_semantics=("arbitrary", "arbitrary", "arbitrary"), vmem_limit_bytes=VMEM_LIMIT),
        name="mla_attn",
    )(qm, km, vm)


def _route(logits):
    lane = lax.broadcasted_iota(jnp.int32, logits.shape, 1)
    neg = jnp.full(logits.shape, -jnp.inf, F32)
    big = jnp.full(logits.shape, LANES, jnp.int32)
    first = lambda mask: jnp.min(jnp.where(mask, lane, big), axis=-1, keepdims=True)

    gl = jnp.where(lane < N_GROUPS, logits, neg)
    gmax = jnp.max(gl, axis=-1, keepdims=True)
    g_idx = first(gl == gmax)
    g_w = 1.0 / jnp.sum(jnp.exp(gl - gmax), axis=-1, keepdims=True)

    lo = ROUTER_LANE0 + EXPERTS_PER_GROUP * g_idx
    el = jnp.where(lane >= lo, jnp.where(lane < lo + EXPERTS_PER_GROUP, logits, neg), neg)
    v1 = jnp.max(el, axis=-1, keepdims=True)
    i1 = first(el == v1)
    el2 = jnp.where(lane == i1, neg, el)
    v2 = jnp.max(el2, axis=-1, keepdims=True)
    i2 = first(el2 == v2)
    t = jnp.exp(v2 - v1)
    w1 = g_w / (1.0 + t)
    w2 = w1 * t
    first_low = i1 < i2
    ea = jnp.minimum(i1, i2) - lo
    eb = jnp.maximum(i1, i2) - lo
    pair = jnp.where(ea == 0, 0, jnp.where(ea == 1, 3, 5)) + eb - ea - 1
    bucket = g_idx * N_PAIRS + pair
    return bucket, jnp.where(first_low, w1, w2), jnp.where(first_low, w2, w1)


def _oproj_kernel(n_p, oap_ref, oas_ref, obp_ref, obs_ref, xp_ref, xs_ref, mod_ref, wo_ref, gpost_ref,
                  gpre_ref, wr_ref, br_ref, xmid_ref, hx_ref, meta_ref, counts_ref, cnt):
    i = pl.program_id(0)

    @pl.when(i == 0)
    def _():
        cnt[...] = jnp.zeros(cnt.shape, F32)

    is_p = i < n_p
    oa = jnp.where(is_p, oap_ref[...], oas_ref[...])
    ob = jnp.where(is_p, obp_ref[...], obs_ref[...])
    x = jnp.where(is_p, xp_ref[...], xs_ref[...])
    mix = _dot(oa, wo_ref[0:DIFF_WIDTH, :]) + _dot(ob, wo_ref[DIFF_WIDTH:DIFF_WIDTH + D_MLA_OUT, :])
    x_mid = x + _rms(mix, gpost_ref[...] * mod_ref[0, 2:3, :])
    xmid_ref[...] = x_mid
    h2 = _rms(x_mid, gpre_ref[...] * (1.0 + mod_ref[0, 4:5, :])) + mod_ref[0, 3:4, :]
    w_r = wr_ref[...]
    w_hi = w_r.astype(BF16)
    w_hi_lo = jnp.concatenate([w_hi, (w_r - w_hi.astype(F32)).astype(BF16)], axis=1)
    h_hi = h2.astype(BF16)
    h_lo = (h2 - h_hi.astype(F32)).astype(BF16)
    hi_terms = _dot(h_hi, w_hi_lo)
    logits = hi_terms[:, 0:LANES] + hi_terms[:, LANES:2 * LANES] + _dot(h_lo, w_hi) + br_ref[...]
    bucket, wa, wb = _route(logits)

    tm = logits.shape[0]
    lane = lax.broadcasted_iota(jnp.int32, logits.shape, 1)
    hx_ref[:, 0:D_MODEL] = h2
    hx_ref[:, D_MODEL:HX_WIDTH] = jnp.where(lane == 0, wa, jnp.where(lane == 1, wb, jnp.zeros_like(logits)))

    onehot = jnp.where(lane == bucket, 1.0, 0.0)
    earlier = (lax.broadcasted_iota(jnp.int32, (tm, tm), 0) > lax.broadcasted_iota(jnp.int32, (tm, tm), 1))
    before = _dot(jnp.where(earlier, 1.0, 0.0).astype(BF16), onehot.astype(BF16))
    rank = jnp.sum(onehot * (before + cnt[0:1, :]), axis=-1, keepdims=True)
    cnt[...] = cnt[...] + jnp.sum(onehot, axis=0, keepdims=True)
    meta_ref[...] = jnp.where(lane == 0, bucket, jnp.where(lane == 1, rank.astype(jnp.int32), 0))
    counts_ref[...] = cnt[...]


def _oproj(rows, oa_p, oa_s, ob_p, ob_s, xp, xs, mod, w_o, g_post, g_pre, w_r, b_r):
    tm, d = rows.tm, D_MODEL
    const = lambda i: (0, 0)
    row = lambda i: (i, 0)
    prow = lambda i: (rows.p_idx(i), 0)
    srow = lambda i: (rows.s_idx(i), 0)
    return pl.pallas_call(
        functools.partial(_oproj_kernel, rows.n_p),
        grid=(rows.n,),
        in_specs=[pl.BlockSpec((tm, DIFF_WIDTH), prow), pl.BlockSpec((tm, DIFF_WIDTH), srow),
                  pl.BlockSpec((tm, D_MLA_OUT), prow), pl.BlockSpec((tm, D_MLA_OUT), srow),
                  pl.BlockSpec((tm, d), prow), pl.BlockSpec((tm, d), srow),
                  pl.BlockSpec((1, N_MOD, d), lambda i: (rows.seq(i), 0, 0)),
                  pl.BlockSpec((DIFF_WIDTH + D_MLA_OUT, d), const, pipeline_mode=pl.Buffered(1)),
                  pl.BlockSpec((1, d), const), pl.BlockSpec((1, d), const),
                  pl.BlockSpec((d, LANES), const), pl.BlockSpec((1, LANES), const)],
        out_specs=[pl.BlockSpec((tm, d), row), pl.BlockSpec((tm, HX_WIDTH), row),
                   pl.BlockSpec((tm, LANES), row), pl.BlockSpec((8, LANES), const)],
        out_shape=[jax.ShapeDtypeStruct((rows.t, d), F32), jax.ShapeDtypeStruct((rows.t, HX_WIDTH), F32),
                   jax.ShapeDtypeStruct((rows.t, LANES), jnp.int32), jax.ShapeDtypeStruct((8, LANES), F32)],
        scratch_shapes=[pltpu.VMEM((8, LANES), F32)],
        compiler_params=pltpu.CompilerParams(dimension_semantics=("arbitrary",),
                                             vmem_limit_bytes=VMEM_LIMIT),
        name="oproj",
    )(oa_p, oa_s, ob_p, ob_s, xp, xs, mod, w_o, g_post, g_pre, w_r, b_r)


def _gather_rows(idx_ref, src_hbm, dst, sem, inline=False):
    def body(r, carry):
        pltpu.make_async_copy(src_hbm.at[pl.ds(idx_ref[0, 0, r], 1), :], dst.at[pl.ds(r, 1), :], sem).start()
        return carry

    if inline:
        for r in range(dst.shape[0]):
            body(r, 0)
    else:
        lax.fori_loop(0, dst.shape[0], body, 0, unroll=8)


def _wait_rows(src_hbm, dst, sem):
    pltpu.make_async_copy(src_hbm.at[pl.ds(0, dst.shape[0]), :], dst, sem).wait()


def _experts_kernel(ea_ref, eb_ref, nv_ref, inv_cur, inv_nxt, inv_nx2, hx_hbm, wga, wua, wda, wgb, wub, wdb,
                    f_ref, xbuf, sem):
    i = pl.program_id(0)
    n_valid = nv_ref[0]
    slot = i % 3

    @pl.when(i == 0)
    def _():
        _gather_rows(inv_cur, hx_hbm, xbuf.at[0], sem.at[0])

    @pl.when(jnp.logical_and(i == 0, n_valid > 1))
    def _():
        _gather_rows(inv_nxt, hx_hbm, xbuf.at[1], sem.at[1])

    def compute():
        _wait_rows(hx_hbm, xbuf.at[slot], sem.at[slot])
        x = xbuf[slot, :, 0:D_MODEL].astype(BF16)
        aux = xbuf[slot, :, D_MODEL:HX_WIDTH]

        def hidden(wg, wu, w):
            g = _dot(x, wg[0])
            return ((g / (1.0 + jnp.exp(-g))) * _dot(x, wu[0]) * w).astype(BF16)

        f_ref[...] = (_dot(hidden(wga, wua, aux[:, 0:1]), wda[0])
                      + _dot(hidden(wgb, wub, aux[:, 1:2]), wdb[0]))

    ahead = (i + 2) % 3

    @pl.when(i + 2 < n_valid)
    def _():
        compute()
        _gather_rows(inv_nx2, hx_hbm, xbuf.at[ahead], sem.at[ahead], inline=True)

    @pl.when(jnp.logical_and(i < n_valid, i + 2 >= n_valid))
    def _():
        compute()

    @pl.when(i >= n_valid)
    def _():
        f_ref[...] = jnp.zeros(f_ref.shape, F32)


def _experts(hx, inv3, tile_ea, tile_eb, n_valid, w_gate, w_up, w_down):
    n_tiles, _, tm = inv3.shape
    d = D_MODEL
    wspec = lambda shape, which: pl.BlockSpec(
        (1,) + shape, (lambda i, ea, eb, nv: (ea[i], 0, 0)) if which == 0 else (lambda i, ea, eb, nv: (eb[i], 0, 0)))
    smem = lambda imap: pl.BlockSpec((1, 1, tm), imap, memory_space=pltpu.SMEM)
    return pl.pallas_call(
        _experts_kernel,
        grid_spec=pltpu.PrefetchScalarGridSpec(
            num_scalar_prefetch=3,
            grid=(n_tiles,),
            in_specs=[smem(lambda i, ea, eb, nv: (i, 0, 0)),
                      smem(lambda i, ea, eb, nv: (jnp.minimum(i + 1, n_tiles - 1), 0, 0)),
                      smem(lambda i, ea, eb, nv: (jnp.minimum(i + 2, n_tiles - 1), 0, 0)),
                      pl.BlockSpec(memory_space=pl.ANY),
                      wspec((d, D_EXPERT), 0), wspec((d, D_EXPERT), 0), wspec((D_EXPERT, d), 0),
                      wspec((d, D_EXPERT), 1), wspec((d, D_EXPERT), 1), wspec((D_EXPERT, d), 1)],
            out_specs=pl.BlockSpec((tm, d), lambda i, ea, eb, nv: (i, 0)),
            scratch_shapes=[pltpu.VMEM((3, tm, HX_WIDTH), F32), pltpu.SemaphoreType.DMA((3,))]),
        out_shape=jax.ShapeDtypeStruct((n_tiles * tm, d), F32),
        compiler_params=pltpu.CompilerParams(dimension_semantics=("arbitrary",),
                                             vmem_limit_bytes=VMEM_LIMIT),
        name="experts",
    )(tile_ea, tile_eb, n_valid, inv3, inv3, inv3, hx, w_gate, w_up, w_down, w_gate, w_up, w_down)


def _final_kernel(n_p, pos_cur, pos_nxt, f_hbm, xmid_ref, mod_ref, gpost_ref, yp_ref, ys_ref, fbuf, sem):
    i = pl.program_id(0)
    slot = i % 2

    @pl.when(i == 0)
    def _():
        _gather_rows(pos_cur, f_hbm, fbuf.at[0], sem.at[0])

    def compute():
        _wait_rows(f_hbm, fbuf.at[slot], sem.at[slot])
        y = xmid_ref[...] + _rms(fbuf[slot], gpost_ref[...] * mod_ref[0, 5:6, :])

        @pl.when(i < n_p)
        def _():
            yp_ref[...] = y

        @pl.when(i >= n_p)
        def _():
            ys_ref[...] = y

    @pl.when(i + 1 < pl.num_programs(0))
    def _():
        _gather_rows(pos_nxt, f_hbm, fbuf.at[1 - slot], sem.at[1 - slot], inline=True)
        compute()

    @pl.when(i + 1 == pl.num_programs(0))
    def _():
        compute()


def _final(rows, pos3, f_sorted, x_mid, mod, g_post):
    tm, d = rows.tm, D_MODEL
    smem = lambda imap: pl.BlockSpec((1, 1, tm), imap, memory_space=pltpu.SMEM)
    return pl.pallas_call(
        functools.partial(_final_kernel, rows.n_p),
        grid=(rows.n,),
        in_specs=[smem(lambda i: (i, 0, 0)),
                  smem(lambda i: (jnp.minimum(i + 1, rows.n - 1), 0, 0)),
                  pl.BlockSpec(memory_space=pl.ANY),
                  pl.BlockSpec((tm, d), lambda i: (i, 0)),
                  pl.BlockSpec((1, N_MOD, d), lambda i: (rows.seq(i), 0, 0)),
                  pl.BlockSpec((1, d), lambda i: (0, 0))],
        out_specs=[pl.BlockSpec((tm, d), lambda i: (rows.p_idx(i), 0)),
                   pl.BlockSpec((tm, d), lambda i: (rows.s_idx(i), 0))],
        out_shape=[jax.ShapeDtypeStruct((rows.tp, d), F32), jax.ShapeDtypeStruct((rows.ts, d), F32)],
        scratch_shapes=[pltpu.VMEM((2, tm, d), F32), pltpu.SemaphoreType.DMA((2,))],
        compiler_params=pltpu.CompilerParams(dimension_semantics=("arbitrary",),
                                             vmem_limit_bytes=VMEM_LIMIT),
        name="final",
    )(pos3, pos3, f_sorted, x_mid, mod, g_post)


def _moe_plan(meta, counts, tm):
    t = meta.shape[0]
    n_tiles = t // tm + N_BUCKETS
    cnt = counts[0, :N_BUCKETS].astype(jnp.int32)
    padded = (cnt + tm - 1) // tm * tm
    ends = jnp.cumsum(padded)
    buckets = jnp.arange(N_BUCKETS, dtype=jnp.int32)
    start_of = jnp.sum(jnp.where(meta[:, 0:1] == buckets[None, :], (ends - padded)[None, :], 0), axis=1)
    pos = start_of + meta[:, 1]
    inv = jnp.zeros((n_tiles * tm,), jnp.int32).at[pos].set(jnp.arange(t, dtype=jnp.int32))
    n_valid = ends[-1] // tm
    tile = jnp.minimum(jnp.arange(n_tiles, dtype=jnp.int32), n_valid - 1)
    bucket = jnp.sum((tile[:, None] * tm >= ends[None, :]).astype(jnp.int32), axis=1)
    group, pair = bucket // N_PAIRS, bucket % N_PAIRS
    ea = jnp.array([0, 0, 0, 1, 1, 2], jnp.int32)[pair]
    eb = jnp.array([1, 2, 3, 2, 3, 3], jnp.int32)[pair]
    base = group * EXPERTS_PER_GROUP
    return pos, inv, base + ea, base + eb, n_valid.reshape(1).astype(jnp.int32)


def _rotate_half_cols(w):
    half = MLA_ROPE_DIM // 2
    return jnp.concatenate([-w[..., half:], w[..., :half]], axis=-1)


def _rope_tables(s_max):
    half = MLA_ROPE_DIM // 2
    inv = ROPE_BASE ** (-jnp.arange(half, dtype=F32) / half)
    ang = jnp.arange(s_max, dtype=F32)[:, None] * inv[None, :]
    pad = jnp.zeros((s_max, LANES - MLA_ROPE_DIM), F32)
    cos, sin = jnp.cos(ang), jnp.sin(ang)
    return (jnp.concatenate([cos, cos, pad], axis=-1), jnp.concatenate([sin, sin, pad], axis=-1))


def _layer(x_prompt, x_sample, c_prompt, c_sample, layer_idx, w_ada, b_ada, g_pre_mix, g_post_mix, w_in,
           lambda_q1, lambda_k1, lambda_q2, lambda_k2, g_diff_sub, g_q_a, w_uq, g_kv_a, w_ukv, w_o,
           g_pre_ffn, g_post_ffn, w_router_group, b_router_group, w_router_expert, b_router_expert,
           w_gate, w_up, w_down, tm=256):
    bp, sp, d = x_prompt.shape
    bs, ss, _ = x_sample.shape
    rows = _Rows(bp, sp, bs, ss, tm)
    xp = x_prompt.reshape(bp * sp, d)
    xs = x_sample.reshape(bs * ss, d)
    lambda_init = 0.8 - 0.6 * math.exp(-0.3 * layer_idx)

    nb = bp + bs
    nb_pad = -(-nb // 8) * 8
    c_all = jnp.concatenate([c_prompt, c_sample, jnp.zeros((nb_pad - nb, d), F32)], axis=0)
    mod = _ada(c_all, w_ada, b_ada.reshape(1, -1)).reshape(nb_pad, N_MOD, d)

    w_kr = w_in[:, C_KR:C_KR + MLA_ROPE_DIM]
    w_kr = jnp.concatenate([w_kr, _rotate_half_cols(w_kr)], axis=1).astype(BF16)
    wq3 = w_uq.reshape(Q_LORA_RANK, MLA_HEADS, MLA_NOPE_DIM + MLA_ROPE_DIM)
    wq_rope = wq3[..., MLA_NOPE_DIM:]
    w_q = jnp.concatenate([wq3[..., :MLA_NOPE_DIM], wq_rope, _rotate_half_cols(wq_rope)], axis=-1)
    w_q = w_q.reshape(Q_LORA_RANK, MLA_HEADS * MLA_QK_PAD).astype(BF16)
    w_kv = w_ukv.astype(BF16)
    cos_t, sin_t = _rope_tables(max(sp, ss))

    dq, dk, dv, qm, km, vm = _inproj(rows, xp, xs, mod, g_pre_mix.reshape(1, d), w_in.astype(BF16), w_kr,
                                     g_q_a.reshape(1, -1), w_q, g_kv_a.reshape(1, -1), w_kv, cos_t, sin_t)

    slopes = jnp.array([2.0 ** (-8.0 * (i + 1) / DIFF_HEADS) for i in range(DIFF_HEADS)], dtype=F32)
    lam_p = jnp.zeros((8, LANES), F32).at[0:4, 0:DIFF_HEAD_DIM].set(
        jnp.stack([lambda_q1, lambda_k1, lambda_q2, lambda_k2]))
    g_sub = g_diff_sub.reshape(1, -1)
    oa_p = _diff_attn(dq, dk, dv, slopes, lam_p, g_sub, 0, bp, sp, lambda_init)
    oa_s = _diff_attn(dq, dk, dv, slopes, lam_p, g_sub, rows.tp, bs, ss, lambda_init)
    ob_p = _mla_attn(qm, km, vm, 0, bp, sp)
    ob_s = _mla_attn(qm, km, vm, rows.tp, bs, ss)

    w_r = jnp.concatenate([w_router_group, w_router_expert.reshape(d, N_EXPERTS),
                           jnp.zeros((d, LANES - N_GROUPS - N_EXPERTS), F32)], axis=1)
    b_r = jnp.concatenate([b_router_group, b_router_expert.reshape(N_EXPERTS),
                           jnp.zeros((LANES - N_GROUPS - N_EXPERTS,), F32)]).reshape(1, LANES)
    x_mid, hx, meta, counts = _oproj(rows, oa_p, oa_s, ob_p, ob_s, xp, xs, mod, w_o.astype(BF16),
                                     g_post_mix.reshape(1, d), g_pre_ffn.reshape(1, d), w_r, b_r)

    pos, inv, tile_ea, tile_eb, n_valid = _moe_plan(meta, counts, MOE_TM)
    f_sorted = _experts(hx, inv.reshape(-1, 1, MOE_TM), tile_ea, tile_eb, n_valid,
                        w_gate.astype(BF16), w_up.astype(BF16), w_down.astype(BF16))
    yp, ys = _final(rows, pos.reshape(rows.n, 1, tm), f_sorted, x_mid, mod, g_post_ffn.reshape(1, d))
    return yp.reshape(bp, sp, d), ys.reshape(bs, ss, d)


def kernel(x_prompt, x_sample, c_prompt, c_sample, w_ada, b_ada, g_pre_mix, g_post_mix, w_in, lambda_q1,
           lambda_k1, lambda_q2, lambda_k2, g_diff_sub, g_q_a, w_uq, g_kv_a, w_ukv, w_o, g_pre_ffn,
           g_post_ffn, w_router_group, b_router_group, w_router_expert, b_router_expert, w_gate, w_up,
           w_down):
    for l in range(w_ada.shape[0]):
        x_prompt, x_sample = _layer(
            x_prompt, x_sample, c_prompt, c_sample, l, w_ada[l], b_ada[l], g_pre_mix[l], g_post_mix[l],
            w_in[l], lambda_q1[l], lambda_k1[l], lambda_q2[l], lambda_k2[l], g_diff_sub[l], g_q_a[l],
            w_uq[l], g_kv_a[l], w_ukv[l], w_o[l], g_pre_ffn[l], g_post_ffn[l], w_router_group[l],
            b_router_group[l], w_router_expert[l], b_router_expert[l], w_gate[l], w_up[l], w_down[l])
    return x_prompt, x_sample
```
